```python
import jax
import jax.numpy as jnp
from jax import lax
import numpy as np

D_MODEL = 1024
BATCH = 32
SEQ = 256
DEPTH = 2
DEC_BATCH = 4
DEC_SEQ = 1024
PAST_LEN = 512

GRID_W = 64
N_EVEN = (DEPTH + 1) // 2
N_ODD = DEPTH // 2
MIX_HALF = D_MODEL // 2
RET_HEADS = 4
RET_DV = MIX_HALF // RET_HEADS
RET_DK = RET_DV // 2
RET_CHUNK = 128
NA_HEADS = 4
NA_DH = MIX_HALF // NA_HEADS
NA_WIN_R = 8
NA_WIN_C = 16
CONV_CH = MIX_HALF
ATT_HEADS = 4
ATT_KV_HEADS = 2
ATT_DH = MIX_HALF // ATT_HEADS
ROPE_THETA = 10000.0
Q_BLOCK = 128
N_EXPERTS = 64
EXPERT_HIDDEN = D_MODEL // 8
SHARED_HIDDEN = EXPERT_HIDDEN
TOP_K = 8
N_GROUPS = 8
TOPK_GROUPS = 4
ROUTED_SCALE = 2.5
DEEPNORM_ALPHA = (2 * DEPTH) ** 0.25
DEEPNORM_BETA = (8 * DEPTH) ** -0.25
LN_EPS = 1e-5
RMS_EPS = 1e-6
EVEN_SIZES = (RET_HEADS * RET_DK, RET_HEADS * RET_DK, RET_HEADS * RET_DV, RET_HEADS * RET_DV,
              NA_HEADS * NA_DH, NA_HEADS * NA_DH, NA_HEADS * NA_DH)
ODD_SIZES = (CONV_CH, CONV_CH, CONV_CH, ATT_HEADS * ATT_DH, ATT_KV_HEADS * ATT_DH, ATT_KV_HEADS * ATT_DH)
IN_EVEN = sum(EVEN_SIZES)
IN_ODD = sum(ODD_SIZES)
MIX_EVEN = RET_HEADS * RET_DV + NA_HEADS * NA_DH
MIX_ODD = CONV_CH + ATT_HEADS * ATT_DH

kernel_name = 'hybrid_prefix_diffusion_trunk_step'


def _split(p, sizes):
    out, s = [], 0
    for n in sizes:
        out.append(p[..., s:s + n])
        s += n
    return out


def layer_norm(x, g, b):
    xf = x.astype(jnp.float32)
    mu = xf.mean(-1, keepdims=True)
    var = jnp.square(xf - mu).mean(-1, keepdims=True)
    return ((xf - mu) * lax.rsqrt(var + LN_EPS) * g + b).astype(x.dtype)


def rms_norm(x, g):
    xf = x.astype(jnp.float32)
    return (xf * lax.rsqrt(jnp.square(xf).mean(-1, keepdims=True) + RMS_EPS) * g).astype(x.dtype)


def head_norm(x):
    xf = x.astype(jnp.float32)
    mu = xf.mean(-1, keepdims=True)
    var = jnp.square(xf - mu).mean(-1, keepdims=True)
    return (xf - mu) * lax.rsqrt(var + LN_EPS)


def adaln(cond, w_mod, b_mod):
    m = jax.nn.silu(cond) @ w_mod + b_mod
    return jnp.split(m, 6, axis=-1)


def modulate(x, shift, scale):
    return x * (1.0 + scale) + shift


def post_norm(x, y, gate, g, b):
    return layer_norm(DEEPNORM_ALPHA * x + (1.0 + gate) * y, g, b)


def attend_blocked(q, k, v):
    B, Lq, Hq, dh = q.shape
    hkv = k.shape[2]
    g = Hq // hkv
    nb = Lq // Q_BLOCK
    scale = dh ** -0.5
    qb = q.reshape(B, nb, Q_BLOCK, hkv, g, dh).transpose(1, 0, 2, 3, 4, 5)

    def blk(qi):
        s = jnp.einsum('bqkgd,blkd->bkgql', qi, k) * scale
        p = jax.nn.softmax(s.astype(jnp.float32), axis=-1).astype(v.dtype)
        return jnp.einsum('bkgql,blkd->bqkgd', p, v)

    o = lax.map(blk, qb)
    return o.transpose(1, 0, 2, 3, 4, 5).reshape(B, Lq, Hq * dh)


def axial_rope(x):
    B, N, H, dh = x.shape
    t = jnp.arange(N)
    row = (t // GRID_W).astype(jnp.float32)
    col = (t % GRID_W).astype(jnp.float32)
    n_freq = dh // 4
    freqs = ROPE_THETA ** (-jnp.arange(n_freq, dtype=jnp.float32) / n_freq)
    ang = jnp.concatenate([row[:, None] * freqs, col[:, None] * freqs], axis=-1)
    cos = jnp.cos(ang)[None, :, None, :]
    sin = jnp.sin(ang)[None, :, None, :]
    x1 = x[..., 0::2].astype(jnp.float32)
    x2 = x[..., 1::2].astype(jnp.float32)
    out = jnp.stack([x1 * cos - x2 * sin, x1 * sin + x2 * cos], axis=-1).reshape(x.shape)
    return out.astype(x.dtype)


def retention_scan(q, k, v, log_gamma, s0):
    q, k, v, s0 = (t.astype(jnp.float32) for t in (q, k, v, s0))
    B, H, L, _ = q.shape
    C = RET_CHUNK
    n = L // C

    def chunks(t):
        return t.reshape(B, H, n, C, t.shape[-1]).transpose(2, 0, 1, 3, 4)

    pos = jnp.arange(C, dtype=jnp.float32)
    lg = log_gamma.astype(jnp.float32)[:, None]
    rel = pos[:, None] - pos[None, :]
    intra = jnp.where(rel >= 0, jnp.exp(lg[..., None] * jnp.maximum(rel, 0.0)), 0.0)
    q_decay = jnp.exp(lg * (pos + 1.0))
    k_decay = jnp.exp(lg * (C - 1.0 - pos))
    chunk_decay = jnp.exp(lg[:, 0] * C)

    def step(s, qkv):
        qc, kc, vc = qkv
        att = jnp.einsum('bhid,bhjd->bhij', qc, kc) * intra
        o = jnp.einsum('bhij,bhjv->bhiv', att, vc) + jnp.einsum('bhid,bhdv->bhiv', qc * q_decay[..., None], s)
        s = s * chunk_decay[:, None, None] + jnp.einsum('bhjd,bhjv->bhdv', kc * k_decay[..., None], vc)
        return s, o

    s, o = lax.scan(step, s0, (chunks(q), chunks(k), chunks(v)))
    o = o.transpose(1, 2, 0, 3, 4).reshape(B, H, L, -1)
    return o, s


def retention_branch(q, k, v, g, decay_logit, s0):
    B, L = q.shape[:2]
    log_gamma = jax.nn.log_sigmoid(decay_logit.astype(jnp.float32))
    qt, kt, vt = (t.transpose(0, 2, 1, 3) for t in (q, k, v))
    o_f, s_f = retention_scan(qt, kt, vt, log_gamma[0], s0[:, 0])
    o_b, s_b = retention_scan(jnp.flip(qt, 2), jnp.flip(kt, 2), jnp.flip(vt, 2), log_gamma[1], s0[:, 1])
    o = (o_f + jnp.flip(o_b, 2)).transpose(0, 2, 1, 3)
    out = (head_norm(o).reshape(B, L, -1) * jax.nn.silu(g.astype(jnp.float32))).astype(g.dtype)
    return out, jnp.stack([s_f, s_b], axis=1)


def na_latent(q, k, v, kc, vc, rpb):
    B, N, H, dh = q.shape
    rows = N // GRID_W
    wr = min(NA_WIN_R, rows)
    wc = NA_WIN_C
    nk = wr * GRID_W
    qg = q.reshape(B, rows, GRID_W, H, dh)
    kg = k.reshape(B, rows, GRID_W, H, dh)
    vg = v.reshape(B, rows, GRID_W, H, dh)
    col = jnp.arange(GRID_W)
    col_start = jnp.clip(col - wc // 2, 0, GRID_W - wc)
    col_in = (col[None, :] >= col_start[:, None]) & (col[None, :] < col_start[:, None] + wc)
    col_off = jnp.clip(col[None, :] - col[:, None] + NA_WIN_C - 1, 0, 2 * NA_WIN_C - 2)
    band_mask = jnp.broadcast_to(col_in[:, None, :], (GRID_W, wr, GRID_W)).reshape(GRID_W, nk)
    scale = dh ** -0.5

    def one_row(r):
        rs = jnp.clip(r - wr // 2, 0, rows - wr)
        qr = lax.dynamic_index_in_dim(qg, r, axis=1, keepdims=False)
        kb = lax.dynamic_slice_in_dim(kg, rs, wr, axis=1).reshape(B, nk, H, dh)
        vb = lax.dynamic_slice_in_dim(vg, rs, wr, axis=1).reshape(B, nk, H, dh)
        row_off = rs + jnp.arange(wr) - r + NA_WIN_R - 1
        bias = rpb[:, row_off[None, :, None], col_off[:, None, :]].reshape(H, GRID_W, nk)
        s_band = jnp.einsum('bqhd,bkhd->bhqk', qr, kb) * scale + bias
        s_band = jnp.where(band_mask, s_band, -jnp.inf)
        s_ctx = jnp.einsum('bqhd,blhd->bhql', qr, kc) * scale
        s = jnp.concatenate([s_band.astype(jnp.float32), s_ctx.astype(jnp.float32)], axis=-1)
        p = jax.nn.softmax(s, axis=-1).astype(v.dtype)
        return (jnp.einsum('bhqk,bkhd->bqhd', p[..., :nk], vb)
                + jnp.einsum('bhql,blhd->bqhd', p[..., nk:], vc))

    out = lax.map(one_row, jnp.arange(rows))
    return out.transpose(1, 0, 2, 3, 4).reshape(B, N, H * dh)


def even_project(h, w_in):
    B, L, _ = h.shape
    qr, kr, vr, gr, qn, kn, vn = _split(h @ w_in, EVEN_SIZES)
    qr = qr.reshape(B, L, RET_HEADS, RET_DK)
    kr = kr.reshape(B, L, RET_HEADS, RET_DK) * (RET_DK ** -0.5)
    vr = vr.reshape(B, L, RET_HEADS, RET_DV)
    qn, kn, vn = (t.reshape(B, L, NA_HEADS, NA_DH) for t in (qn, kn, vn))
    return qr, kr, vr, gr, qn, kn, vn


def even_mixer_ctx(h, w_in, w_out, decay_logit):
    qr, kr, vr, gr, qn, kn, vn = even_project(h, w_in)
    s0 = jnp.zeros((h.shape[0], 2, RET_HEADS, RET_DK, RET_DV), jnp.float32)
    ret, state = retention_branch(qr, kr, vr, gr, decay_logit, s0)
    na = attend_blocked(qn, kn, vn)
    out = jnp.concatenate([ret, na.astype(ret.dtype)], axis=-1) @ w_out
    return out, state, kn, vn


def even_mixer_lat(h, w_in, w_out, decay_logit, rpb, state, kc, vc):
    qr, kr, vr, gr, qn, kn, vn = even_project(h, w_in)
    ret, _ = retention_branch(qr, kr, vr, gr, decay_logit, state)
    na = na_latent(qn, kn, vn, kc, vc, rpb)
    return jnp.concatenate([ret, na.astype(ret.dtype)], axis=-1) @ w_out


def short_conv(u, w):
    up = jnp.pad(u, ((0, 0), (1, 1), (0, 0)))
    return up[:, :-2] * w[0] + up[:, 1:-1] * w[1] + up[:, 2:] * w[2]


def odd_project(h, w_in, conv_w, qn_g, kn_g):
    B, L, _ = h.shape
    b_g, c_g, u, q, k, v = _split(h @ w_in, ODD_SIZES)
    conv = b_g * short_conv(c_g * u, conv_w)
    q = rms_norm(q.reshape(B, L, ATT_HEADS, ATT_DH), qn_g)
    k = rms_norm(k.reshape(B, L, ATT_KV_HEADS, ATT_DH), kn_g)
    v = v.reshape(B, L, ATT_KV_HEADS, ATT_DH)
    return conv, q, k, v


def odd_mixer_ctx(h, w_in, w_out, conv_w, qn_g, kn_g):
    conv, q, k, v = odd_project(h, w_in, conv_w, qn_g, kn_g)
    att = attend_blocked(q, k, v)
    out = jnp.concatenate([conv, att.astype(conv.dtype)], axis=-1) @ w_out
    return out, k, v


def odd_mixer_lat(h, w_in, w_out, conv_w, qn_g, kn_g, kc, vc):
    conv, q, k, v = odd_project(h, w_in, conv_w, qn_g, kn_g)
    q = axial_rope(q)
    k = axial_rope(k)
    keys = jnp.concatenate([k, kc.astype(k.dtype)], axis=1)
    vals = jnp.concatenate([v, vc.astype(v.dtype)], axis=1)
    att = attend_blocked(q, keys, vals)
    return jnp.concatenate([conv, att.astype(conv.dtype)], axis=-1) @ w_out


def moe_ffn(h, router_w, router_b, wg, wu, wd, sg, su, sd):
    shp = h.shape
    x = h.reshape(-1, shp[-1])
    T = x.shape[0]
    scores = jax.nn.sigmoid((x @ router_w).astype(jnp.float32))
    sel = scores + router_b.astype(jnp.float32)
    grp_score = lax.top_k(sel.reshape(T, N_GROUPS, -1), 2)[0].sum(-1)
    _, gidx = lax.top_k(grp_score, TOPK_GROUPS)
    gmask = jax.nn.one_hot(gidx, N_GROUPS).sum(1) > 0
    emask = jnp.repeat(gmask, N_EXPERTS // N_GROUPS, axis=1)
    _, eidx = lax.top_k(jnp.where(emask, sel, -jnp.inf), TOP_K)
    w = jnp.take_along_axis(scores, eidx, axis=-1)
    w = w / w.sum(-1, keepdims=True) * ROUTED_SCALE
    gate = jnp.einsum('tk,tke->te', w, jax.nn.one_hot(eidx, N_EXPERTS, dtype=jnp.float32)).astype(x.dtype)
    hg = jnp.einsum('td,edf->tef', x, wg)
    hu = jnp.einsum('td,edf->tef', x, wu)
    y = jnp.einsum('tef,efd->td', jax.nn.silu(hg) * hu * gate[:, :, None], wd)
    y = y + (jax.nn.silu(x @ sg) * (x @ su)) @ sd
    return y.reshape(shp)


def setup_inputs(seed: int = 0) -> dict:
    key = jax.random.key(seed)
    ks = jax.random.split(key, 30)

    def nrm(k, shape, scale):
        return jax.random.normal(k, shape, jnp.float32) * scale

    gam = 1.0 - 2.0 ** (-5.0 - jnp.arange(RET_HEADS, dtype=jnp.float32))
    base_logit = jnp.log(gam) - jnp.log1p(-gam)
    return {
        'x_prompt': nrm(ks[0], (BATCH, SEQ, D_MODEL), 1.0),
        'x_sample': nrm(ks[1], (DEC_BATCH, DEC_SEQ, D_MODEL), 1.0),
        'c': nrm(ks[2], (DEC_BATCH, D_MODEL), 1.0),
        'state_ret': nrm(ks[3], (DEC_BATCH, N_EVEN, 2, RET_HEADS, RET_DK, RET_DV), 1.0),
        'cache_na_k': nrm(ks[4], (DEC_BATCH, N_EVEN, PAST_LEN, NA_HEADS, NA_DH), 1.0),
        'cache_na_v': nrm(ks[5], (DEC_BATCH, N_EVEN, PAST_LEN, NA_HEADS, NA_DH), 1.0),
        'cache_att_k': nrm(ks[6], (DEC_BATCH, N_ODD, PAST_LEN, ATT_KV_HEADS, ATT_DH), 1.0),
        'cache_att_v': nrm(ks[7], (DEC_BATCH, N_ODD, PAST_LEN, ATT_KV_HEADS, ATT_DH), 1.0),
        'c_ctx': nrm(ks[8], (D_MODEL,), 1.0),
        'w_mod': nrm(ks[9], (DEPTH, D_MODEL, 6 * D_MODEL), 0.2 * D_MODEL ** -0.5),
        'b_mod': nrm(ks[10], (DEPTH, 6 * D_MODEL), 0.01),
        'ln_g': 1.0 + nrm(ks[11], (DEPTH, 2, D_MODEL), 0.02),
        'ln_b': nrm(ks[12], (DEPTH, 2, D_MODEL), 0.02),
        'w_in_even': nrm(ks[13], (N_EVEN, D_MODEL, IN_EVEN), D_MODEL ** -0.5),
        'w_out_even': nrm(ks[14], (N_EVEN, MIX_EVEN, D_MODEL), DEEPNORM_BETA * MIX_EVEN ** -0.5),
        'ret_decay_logit': jnp.broadcast_to(base_logit, (N_EVEN, 2, RET_HEADS)) + nrm(ks[15], (N_EVEN, 2, RET_HEADS), 0.05),
        'na_rpb': nrm(ks[16], (N_EVEN, NA_HEADS, 2 * NA_WIN_R - 1, 2 * NA_WIN_C - 1), 0.1),
        'w_in_odd': nrm(ks[17], (N_ODD, D_MODEL, IN_ODD), D_MODEL ** -0.5),
        'w_out_odd': nrm(ks[18], (N_ODD, MIX_ODD, D_MODEL), DEEPNORM_BETA * MIX_ODD ** -0.5),
        'conv_w': nrm(ks[19], (N_ODD, 3, CONV_CH), 3.0 ** -0.5),
        'q_norm_g': 1.0 + nrm(ks[20], (N_ODD, ATT_DH), 0.02),
        'k_norm_g': 1.0 + nrm(ks[21], (N_ODD, ATT_DH), 0.02),
        'router_w': nrm(ks[22], (DEPTH, D_MODEL, N_EXPERTS), D_MODEL ** -0.5),
        'router_b': nrm(ks[23], (DEPTH, N_EXPERTS), 0.01),
        'exp_w_gate': nrm(ks[24], (DEPTH, N_EXPERTS, D_MODEL, EXPERT_HIDDEN), D_MODEL ** -0.5),
        'exp_w_up': nrm(ks[25], (DEPTH, N_EXPERTS, D_MODEL, EXPERT_HIDDEN), D_MODEL ** -0.5),
        'exp_w_down': nrm(ks[26], (DEPTH, N_EXPERTS, EXPERT_HIDDEN, D_MODEL), DEEPNORM_BETA * EXPERT_HIDDEN ** -0.5),
        'sh_w_gate': nrm(ks[27], (DEPTH, D_MODEL, SHARED_HIDDEN), D_MODEL ** -0.5),
        'sh_w_up': nrm(ks[28], (DEPTH, D_MODEL, SHARED_HIDDEN), D_MODEL ** -0.5),
        'sh_w_down': nrm(ks[29], (DEPTH, SHARED_HIDDEN, D_MODEL), DEEPNORM_BETA * SHARED_HIDDEN ** -0.5),
    }


def reference(x_prompt, x_sample, c, state_ret, cache_na_k, cache_na_v, cache_att_k, cache_att_v, c_ctx,
              w_mod, b_mod, ln_g, ln_b, w_in_even, w_out_even, ret_decay_logit, na_rpb,
              w_in_odd, w_out_odd, conv_w, q_norm_g, k_norm_g, router_w, router_b,
              exp_w_gate, exp_w_up, exp_w_down, sh_w_gate, sh_w_up, sh_w_down):
    ctx_cond = c_ctx[None, None, :]
    lat_cond = c[:, None, :]
    yp, ys = x_prompt, x_sample
    ret_states, na_k, na_v, att_k, att_v = [], [], [], [], []
    for l in range(DEPTH):
        i = l // 2
        mp = adaln(ctx_cond, w_mod[l], b_mod[l])
        ms = adaln(lat_cond, w_mod[l], b_mod[l])
        hp = modulate(yp, mp[0], mp[1])
        hs = modulate(ys, ms[0], ms[1])
        if l % 2 == 0:
            op, st, kp, vp = even_mixer_ctx(hp, w_in_even[i], w_out_even[i], ret_decay_logit[i])
            os_ = even_mixer_lat(hs, w_in_even[i], w_out_even[i], ret_decay_logit[i], na_rpb[i],
                                 state_ret[:, i], cache_na_k[:, i], cache_na_v[:, i])
            ret_states.append(st)
            na_k.append(kp)
            na_v.append(vp)
        else:
            op, kp, vp = odd_mixer_ctx(hp, w_in_odd[i], w_out_odd[i], conv_w[i], q_norm_g[i], k_norm_g[i])
            os_ = odd_mixer_lat(hs, w_in_odd[i], w_out_odd[i], conv_w[i], q_norm_g[i], k_norm_g[i],
                                cache_att_k[:, i], cache_att_v[:, i])
            att_k.append(kp)
            att_v.append(vp)
        yp = post_norm(yp, op, mp[2], ln_g[l, 0], ln_b[l, 0])
        ys = post_norm(ys, os_, ms[2], ln_g[l, 0], ln_b[l, 0])
        moe_w = (router_w[l], router_b[l], exp_w_gate[l], exp_w_up[l], exp_w_down[l], sh_w_gate[l], sh_w_up[l], sh_w_down[l])
        yp = post_norm(yp, moe_ffn(modulate(yp, mp[3], mp[4]), *moe_w), mp[5], ln_g[l, 1], ln_b[l, 1])
        ys = post_norm(ys, moe_ffn(modulate(ys, ms[3], ms[4]), *moe_w), ms[5], ln_g[l, 1], ln_b[l, 1])
    new_state_ret = jnp.stack(ret_states, axis=1)
    new_na_k = jnp.stack(na_k, axis=1)
    new_na_v = jnp.stack(na_v, axis=1)
    new_att_k = jnp.stack(att_k, axis=1)
    new_att_v = jnp.stack(att_v, axis=1)
    return (yp, ys, new_state_ret, new_na_k, new_na_v, new_att_k, new_att_v)
```

```python
import functools

import jax
import jax.numpy as jnp
from jax import lax
from jax.experimental import pallas as pl
from jax.experimental.pallas import tpu as pltpu

F32 = jnp.float32
BF16 = jnp.bfloat16

D_MODEL = 1024
BATCH = 32
SEQ = 256
DEPTH = 2
DEC_BATCH = 4
DEC_SEQ = 1024
PAST_LEN = 512
GRID_W = 64
GRID_H = DEC_SEQ // GRID_W
MIX_HALF = D_MODEL // 2
RET_HEADS = 4
RET_DV = MIX_HALF // RET_HEADS
RET_DK = RET_DV // 2
NA_HEADS = 4
NA_DH = MIX_HALF // NA_HEADS
NA_WIN_R = 8
NA_WIN_C = 16
CONV_CH = MIX_HALF
ATT_HEADS = 4
ATT_KV_HEADS = 2
ATT_DH = MIX_HALF // ATT_HEADS
ROPE_THETA = 10000.0
N_EXPERTS = 64
EXPERT_HIDDEN = D_MODEL // 8
TOP_K = 8
N_GROUPS = 8
GROUP_SIZE = N_EXPERTS // N_GROUPS
TOPK_GROUPS = 4
ROUTED_SCALE = 2.5
DEEPNORM_ALPHA = (2 * DEPTH) ** 0.25
LN_EPS = 1e-5
RMS_EPS = 1e-6
EVEN_SIZES = (RET_HEADS * RET_DK, RET_HEADS * RET_DK, RET_HEADS * RET_DV, RET_HEADS * RET_DV,
              NA_HEADS * NA_DH, NA_HEADS * NA_DH, NA_HEADS * NA_DH)
ODD_SIZES = (CONV_CH, CONV_CH, CONV_CH, ATT_HEADS * ATT_DH, ATT_KV_HEADS * ATT_DH, ATT_KV_HEADS * ATT_DH)
N_MOD = 6
RPB_ROWS = 2 * NA_WIN_R - 1
RPB_COLS = 2 * NA_WIN_C - 1

LANES = 128
COND_ROWS = 16
V7X_VMEM_LIMIT = 56 * 1024 * 1024

PROJ_TILE = 512
MOE_TILE = 1024
MOE_EPS = 4
ADALN_TN = 1536


def _params(n_axes):
    return pltpu.CompilerParams(dimension_semantics=("arbitrary",) * n_axes,
                                vmem_limit_bytes=V7X_VMEM_LIMIT)


def _dot(a, b):
    return jnp.dot(a, b, preferred_element_type=F32)


def _dot_nt(a, b):
    return lax.dot_general(a, b, (((1,), (1,)), ((), ())), preferred_element_type=F32)


def _dot_tn(a, b):
    return lax.dot_general(a, b, (((0,), (0,)), ((), ())), preferred_element_type=F32)


def _split(a):
    hi = a.astype(BF16)
    lo = (a - hi.astype(F32)).astype(BF16)
    return hi, lo


def _silu(x):
    return x / (1.0 + jnp.exp(-x))


def _log_sigmoid(x):
    return jnp.minimum(x, 0.0) - jnp.log(1.0 + jnp.exp(-jnp.abs(x)))


def _layer_norm(z, g, b):
    mu = jnp.mean(z, axis=-1, keepdims=True)
    zc = z - mu
    var = jnp.mean(zc * zc, axis=-1, keepdims=True)
    return zc * lax.rsqrt(var + LN_EPS) * g + b


def _post_norm(x, y, gate, g, b):
    return _layer_norm(DEEPNORM_ALPHA * x + (1.0 + gate) * y, g, b)


def _head_norm(x):
    mu = jnp.mean(x, axis=-1, keepdims=True)
    xc = x - mu
    var = jnp.mean(xc * xc, axis=-1, keepdims=True)
    return xc * lax.rsqrt(var + LN_EPS)


def _rms_norm(x, g):
    return x * lax.rsqrt(jnp.mean(x * x, axis=-1, keepdims=True) + RMS_EPS) * g


def _softmax_pv(parts):
    m = None
    for s, _ in parts:
        mi = jnp.max(s, axis=-1, keepdims=True)
        m = mi if m is None else jnp.maximum(m, mi)
    l = None
    o = None
    for s, v in parts:
        p = jnp.exp(s - m)
        li = jnp.sum(p, axis=-1, keepdims=True)
        oi = _dot(p.astype(BF16), v)
        l = li if l is None else l + li
        o = oi if o is None else o + oi
    return o / l


def _adaln_kernel(c_ref, w_ref, b_ref, o_ref):
    a_hi, a_lo = _split(_silu(c_ref[...]))
    w_hi, w_lo = _split(w_ref[...])
    o_ref[...] = _dot(a_hi, w_hi) + _dot(a_lo, w_hi) + _dot(a_hi, w_lo) + b_ref[...]


def _adaln(cond, w_mod, b_mod):
    n = w_mod.shape[-1]
    return pl.pallas_call(
        _adaln_kernel,
        grid=(DEPTH, n // ADALN_TN),
        in_specs=[pl.BlockSpec((COND_ROWS, D_MODEL), lambda l, j: (0, 0)),
                  pl.BlockSpec((None, D_MODEL, ADALN_TN), lambda l, j: (l, 0, j)),
                  pl.BlockSpec((None, 1, ADALN_TN), lambda l, j: (l, 0, j))],
        out_specs=pl.BlockSpec((None, COND_ROWS, ADALN_TN), lambda l, j: (l, 0, j)),
        out_shape=jax.ShapeDtypeStruct((DEPTH, COND_ROWS, n), F32),
        compiler_params=_params(2),
        name="adaln",
    )(cond, w_mod, b_mod.reshape(DEPTH, 1, n))


def _inproj_kernel(x_ref, mod_ref, w_ref, *out_refs, sizes):
    shift = mod_ref[0:1, :]
    scale = mod_ref[1:2, :]
    h = (x_ref[...] * (1.0 + scale) + shift).astype(BF16)
    y = _dot(h, w_ref[...])
    off = 0
    for o_ref, n in zip(out_refs, sizes):
        o_ref[...] = y[:, off:off + n].astype(o_ref.dtype)
        off += n


def _inproj(x, mods, w_bf16, sizes, dtypes, seq_len):
    t = x.shape[0]
    tile = PROJ_TILE
    mod_idx = (lambda i: (i * tile // seq_len, 0, 0)) if mods.shape[0] > 1 else (lambda i: (0, 0, 0))
    return pl.pallas_call(
        functools.partial(_inproj_kernel, sizes=sizes),
        grid=(t // tile,),
        in_specs=[pl.BlockSpec((tile, D_MODEL), lambda i: (i, 0)),
                  pl.BlockSpec((None, N_MOD, D_MODEL), mod_idx),
                  pl.BlockSpec(w_bf16.shape, lambda i: (0, 0))],
        out_specs=[pl.BlockSpec((tile, n), lambda i: (i, 0)) for n in sizes],
        out_shape=[jax.ShapeDtypeStruct((t, n), dt) for n, dt in zip(sizes, dtypes)],
        compiler_params=_params(1),
        name="inproj",
    )(x, mods, w_bf16)


def _outproj_kernel(mix_ref, x_ref, mod_ref, w_ref, g_ref, b_ref, o_ref):
    y = _dot(mix_ref[...], w_ref[...])
    o_ref[...] = _post_norm(x_ref[...], y, mod_ref[2:3, :], g_ref[...], b_ref[...])


def _outproj(mix, x, mods, w_bf16, ln_g, ln_b, seq_len):
    t = x.shape[0]
    tile = PROJ_TILE
    mod_idx = (lambda i: (i * tile // seq_len, 0, 0)) if mods.shape[0] > 1 else (lambda i: (0, 0, 0))
    return pl.pallas_call(
        _outproj_kernel,
        grid=(t // tile,),
        in_specs=[pl.BlockSpec((tile, D_MODEL), lambda i: (i, 0)),
                  pl.BlockSpec((tile, D_MODEL), lambda i: (i, 0)),
                  pl.BlockSpec((None, N_MOD, D_MODEL), mod_idx),
                  pl.BlockSpec((D_MODEL, D_MODEL), lambda i: (0, 0)),
                  pl.BlockSpec((1, D_MODEL), lambda i: (0, 0)),
                  pl.BlockSpec((1, D_MODEL), lambda i: (0, 0))],
        out_specs=pl.BlockSpec((tile, D_MODEL), lambda i: (i, 0)),
        out_shape=jax.ShapeDtypeStruct((t, D_MODEL), F32),
        compiler_params=_params(1),
        name="outproj",
    )(mix, x, mods, w_bf16, ln_g.reshape(1, D_MODEL), ln_b.reshape(1, D_MODEL))


def _decay_matrix(length, lgf, lgb):
    ii = lax.broadcasted_iota(jnp.int32, (length, length), 0)
    jj = lax.broadcasted_iota(jnp.int32, (length, length), 1)
    rel = (ii - jj).astype(F32)
    fwd = jnp.where(rel >= 0.0, jnp.exp(lgf * jnp.maximum(rel, 0.0)), 0.0)
    bwd = jnp.where(rel <= 0.0, jnp.exp(lgb * jnp.maximum(-rel, 0.0)), 0.0)
    return fwd + bwd


def _retention_head(q, k, v, lgf, lgb, s0f, s0b):
    length = q.shape[0]
    s = _dot_nt(q.astype(BF16), k.astype(BF16))
    o = _dot((s * _decay_matrix(length, lgf, lgb)).astype(BF16), v)
    if s0f is not None:
        pos = lax.broadcasted_iota(jnp.int32, (length, 1), 0).astype(F32)
        qf = q * jnp.exp(lgf * (pos + 1.0))
        qb = q * jnp.exp(lgb * (length - pos))
        o = o + _dot(qf.astype(BF16), s0f.astype(BF16)) + _dot(qb.astype(BF16), s0b.astype(BF16))
    return o


def _even_ctx_kernel(qr_ref, kr_ref, vr_ref, gr_ref, qn_ref, kn_ref, vn_ref, dl_ref, mix_ref, st_ref):
    length = qr_ref.shape[0]
    lg = _log_sigmoid(dl_ref[...])
    pos = lax.broadcasted_iota(jnp.int32, (length, 1), 0).astype(F32)
    for h in range(RET_HEADS):
        lgf = lg[h:h + 1, 0:1]
        lgb = lg[RET_HEADS + h:RET_HEADS + h + 1, 0:1]
        q = qr_ref[:, h * RET_DK:(h + 1) * RET_DK].astype(F32)
        k = kr_ref[:, h * RET_DK:(h + 1) * RET_DK].astype(F32) * (RET_DK ** -0.5)
        v = vr_ref[:, h * RET_DV:(h + 1) * RET_DV]
        o = _retention_head(q, k, v, lgf, lgb, None, None)
        kf = k * jnp.exp(lgf * (length - 1.0 - pos))
        kb = k * jnp.exp(lgb * pos)
        st_ref[h] = _dot_tn(kf.astype(BF16), v)
        st_ref[RET_HEADS + h] = _dot_tn(kb.astype(BF16), v)
        g = gr_ref[:, h * RET_DV:(h + 1) * RET_DV]
        mix_ref[:, h * RET_DV:(h + 1) * RET_DV] = (_head_norm(o) * _silu(g)).astype(mix_ref.dtype)
    base = RET_HEADS * RET_DV
    for h in range(NA_HEADS):
        q = qn_ref[:, h * NA_DH:(h + 1) * NA_DH]
        k = kn_ref[:, h * NA_DH:(h + 1) * NA_DH].astype(BF16)
        v = vn_ref[:, h * NA_DH:(h + 1) * NA_DH].astype(BF16)
        s = _dot_nt(q, k) * (NA_DH ** -0.5)
        mix_ref[:, base + h * NA_DH:base + (h + 1) * NA_DH] = _softmax_pv([(s, v)]).astype(mix_ref.dtype)


def _even_ctx(qr, kr, vr, gr, qn, kn, vn, decay_rows):
    def seq_spec(n):
        return pl.BlockSpec((SEQ, n), lambda i: (i, 0))
    return pl.pallas_call(
        _even_ctx_kernel,
        grid=(BATCH,),
        in_specs=[seq_spec(n) for n in EVEN_SIZES] + [pl.BlockSpec((2 * RET_HEADS, LANES), lambda i: (0, 0))],
        out_specs=[seq_spec(D_MODEL),
                   pl.BlockSpec((None, 2 * RET_HEADS, RET_DK, RET_DV), lambda i: (i, 0, 0, 0))],
        out_shape=[jax.ShapeDtypeStruct((BATCH * SEQ, D_MODEL), BF16),
                   jax.ShapeDtypeStruct((BATCH, 2 * RET_HEADS, RET_DK, RET_DV), F32)],
        compiler_params=_params(1),
        name="even_ctx",
    )(qr, kr, vr, gr, qn, kn, vn, decay_rows)


def _build_rpb_tiles(rpb_ref, tile_ref):
    qc = lax.broadcasted_iota(jnp.int32, (GRID_W, LANES), 0)
    kc = lax.broadcasted_iota(jnp.int32, (GRID_W, LANES), 1) % GRID_W
    diff = kc - qc + (NA_WIN_C - 1)
    start = jnp.clip(qc - NA_WIN_C // 2, 0, GRID_W - NA_WIN_C)
    win = (kc >= start) & (kc < start + NA_WIN_C)

    def body(idx, carry):
        t = jnp.zeros((GRID_W, LANES), F32)
        for d in range(RPB_COLS):
            t = jnp.where(diff == d, rpb_ref[idx * RPB_COLS + d], t)
        tile_ref[idx] = jnp.where(win, t, -jnp.inf)
        return carry

    lax.fori_loop(0, NA_HEADS * RPB_ROWS, body, 0)


def _build_bias(tile_ref, bias_ref, head):
    left = lax.broadcasted_iota(jnp.int32, (GRID_W, LANES), 1) < GRID_W
    neg = jnp.full((GRID_W, LANES), -jnp.inf, F32)
    rows_w = min(NA_WIN_R, GRID_H)
    for qr in range(GRID_H):
        rs = min(max(qr - rows_w // 2, 0), GRID_H - rows_w)

        def tile(kr):
            if rs <= kr < rs + rows_w:
                return tile_ref[head * RPB_ROWS + kr - qr + NA_WIN_R - 1]
            return neg

        pieces = [jnp.where(left, tile(2 * a), tile(2 * a + 1)) for a in range(GRID_H // 2)]
        bias_ref[qr * GRID_W:(qr + 1) * GRID_W, :] = jnp.concatenate(pieces, axis=1)


def _even_lat_kernel(qr_ref, kr_ref, vr_ref, gr_ref, qn_ref, kn_ref, vn_ref, s0_ref, ck_ref, cv_ref,
                     dl_ref, rpb_ref, mix_ref, tile_ref, bias_ref):
    @pl.when(pl.program_id(0) == 0)
    def _():
        _build_rpb_tiles(rpb_ref, tile_ref)

    lg = _log_sigmoid(dl_ref[...])
    for h in range(RET_HEADS):
        lgf = lg[h:h + 1, 0:1]
        lgb = lg[RET_HEADS + h:RET_HEADS + h + 1, 0:1]
        q = qr_ref[:, h * RET_DK:(h + 1) * RET_DK].astype(F32)
        k = kr_ref[:, h * RET_DK:(h + 1) * RET_DK].astype(F32) * (RET_DK ** -0.5)
        v = vr_ref[:, h * RET_DV:(h + 1) * RET_DV]
        o = _retention_head(q, k, v, lgf, lgb, s0_ref[h], s0_ref[RET_HEADS + h])
        g = gr_ref[:, h * RET_DV:(h + 1) * RET_DV]
        mix_ref[:, h * RET_DV:(h + 1) * RET_DV] = (_head_norm(o) * _silu(g)).astype(mix_ref.dtype)
    base = RET_HEADS * RET_DV
    scale = NA_DH ** -0.5
    for h in range(NA_HEADS):
        _build_bias(tile_ref, bias_ref, h)
        sl = slice(h * NA_DH, (h + 1) * NA_DH)
        q = qn_ref[:, sl]
        s_band = _dot_nt(q, kn_ref[:, sl]) * scale + bias_ref[...]
        s_ctx = _dot_nt(q, ck_ref[:, sl].astype(BF16)) * scale
        o = _softmax_pv([(s_band, vn_ref[:, sl]), (s_ctx, cv_ref[:, sl].astype(BF16))])
        mix_ref[:, base + h * NA_DH:base + (h + 1) * NA_DH] = o.astype(mix_ref.dtype)


def _even_lat(qr, kr, vr, gr, qn, kn, vn, state, cache_k, cache_v, decay_rows, rpb_flat):
    def seq_spec(n):
        return pl.BlockSpec((DEC_SEQ, n), lambda i: (i, 0))
    cache_spec = pl.BlockSpec((None, PAST_LEN, NA_HEADS * NA_DH), lambda i: (i, 0, 0))
    return pl.pallas_call(
        _even_lat_kernel,
        grid=(DEC_BATCH,),
        in_specs=[seq_spec(n) for n in EVEN_SIZES]
        + [pl.BlockSpec((None, 2 * RET_HEADS, RET_DK, RET_DV), lambda i: (i, 0, 0, 0)),
           cache_spec, cache_spec,
           pl.BlockSpec((2 * RET_HEADS, LANES), lambda i: (0, 0)),
           pl.BlockSpec(memory_space=pltpu.SMEM)],
        out_specs=seq_spec(D_MODEL),
        out_shape=jax.ShapeDtypeStruct((DEC_BATCH * DEC_SEQ, D_MODEL), BF16),
        scratch_shapes=[pltpu.VMEM((NA_HEADS * RPB_ROWS, GRID_W, LANES), F32),
                        pltpu.VMEM((DEC_SEQ, DEC_SEQ), F32)],
        compiler_params=_params(1),
        name="even_lat",
    )(qr, kr, vr, gr, qn, kn, vn, state, cache_k, cache_v, decay_rows, rpb_flat)


def _gated_conv(bg_ref, cg_ref, u_ref, w_ref):
    length, ch = u_ref.shape
    z = cg_ref[...] * u_ref[...]
    row = lax.broadcasted_iota(jnp.int32, (length, ch), 0)
    z_prev = jnp.where(row == 0, 0.0, pltpu.roll(z, 1, 0))
    z_next = jnp.where(row == length - 1, 0.0, pltpu.roll(z, length - 1, 0))
    return bg_ref[...] * (z_prev * w_ref[0:1, :] + z * w_ref[1:2, :] + z_next * w_ref[2:3, :])


def _odd_ctx_kernel(bg_ref, cg_ref, u_ref, q_ref, k_ref, v_ref, w_ref, qg_ref, kg_ref, mix_ref, ko_ref):
    mix_ref[:, 0:CONV_CH] = _gated_conv(bg_ref, cg_ref, u_ref, w_ref).astype(mix_ref.dtype)
    group = ATT_HEADS // ATT_KV_HEADS
    scale = ATT_DH ** -0.5
    for kv in range(ATT_KV_HEADS):
        sl = slice(kv * ATT_DH, (kv + 1) * ATT_DH)
        k = _rms_norm(k_ref[:, sl], kg_ref[...])
        ko_ref[:, sl] = k
        k = k.astype(BF16)
        v = v_ref[:, sl].astype(BF16)
        for g in range(group):
            h = kv * group + g
            q = _rms_norm(q_ref[:, h * ATT_DH:(h + 1) * ATT_DH], qg_ref[...]).astype(BF16)
            o = _softmax_pv([(_dot_nt(q, k) * scale, v)])
            mix_ref[:, CONV_CH + h * ATT_DH:CONV_CH + (h + 1) * ATT_DH] = o.astype(mix_ref.dtype)


def _odd_ctx(bg, cg, u, q, k, v, conv_w, qn_g, kn_g):
    def seq_spec(n):
        return pl.BlockSpec((SEQ, n), lambda i: (i, 0))

    def full(shape):
        return pl.BlockSpec(shape, lambda i: (0,) * len(shape))
    kv_w = ATT_KV_HEADS * ATT_DH
    return pl.pallas_call(
        _odd_ctx_kernel,
        grid=(BATCH,),
        in_specs=[seq_spec(n) for n in ODD_SIZES] + [full((3, CONV_CH)), full((1, ATT_DH)), full((1, ATT_DH))],
        out_specs=[seq_spec(D_MODEL), seq_spec(kv_w)],
        out_shape=[jax.ShapeDtypeStruct((BATCH * SEQ, D_MODEL), BF16),
                   jax.ShapeDtypeStruct((BATCH * SEQ, kv_w), F32)],
        compiler_params=_params(1),
        name="odd_ctx",
    )(bg, cg, u, q, k, v, conv_w, qn_g, kn_g)


def _rope(x, cos, sin, even_lane):
    swapped = jnp.where(even_lane, pltpu.roll(x, LANES - 1, 1), pltpu.roll(x, 1, 1))
    return x * cos + swapped * sin


def _odd_lat_kernel(bg_ref, cg_ref, u_ref, q_ref, k_ref, v_ref, ck_ref, cv_ref, w_ref, qg_ref, kg_ref,
                    cos_ref, sin_ref, mix_ref):
    mix_ref[:, 0:CONV_CH] = _gated_conv(bg_ref, cg_ref, u_ref, w_ref).astype(mix_ref.dtype)
    group = ATT_HEADS // ATT_KV_HEADS
    scale = ATT_DH ** -0.5
    length = q_ref.shape[0]
    even_lane = lax.broadcasted_iota(jnp.int32, (length, ATT_DH), 1) % 2 == 0
    cos = cos_ref[...]
    sin = sin_ref[...]
    for kv in range(ATT_KV_HEADS):
        sl = slice(kv * ATT_DH, (kv + 1) * ATT_DH)
        k = _rope(_rms_norm(k_ref[:, sl], kg_ref[...]), cos, sin, even_lane).astype(BF16)
        v = v_ref[:, sl]
        kc = ck_ref[:, sl].astype(BF16)
        vc = cv_ref[:, sl].astype(BF16)
        for g in range(group):
            h = kv * group + g
            q = _rms_norm(q_ref[:, h * ATT_DH:(h + 1) * ATT_DH], qg_ref[...])
            q = _rope(q, cos, sin, even_lane).astype(BF16)
            o = _softmax_pv([(_dot_nt(q, k) * scale, v), (_dot_nt(q, kc) * scale, vc)])
            mix_ref[:, CONV_CH + h * ATT_DH:CONV_CH + (h + 1) * ATT_DH] = o.astype(mix_ref.dtype)


def _odd_lat(bg, cg, u, q, k, v, cache_k, cache_v, conv_w, qn_g, kn_g, cos, sin):
    def seq_spec(n):
        return pl.BlockSpec((DEC_SEQ, n), lambda i: (i, 0))

    def full(shape):
        return pl.BlockSpec(shape, lambda i: (0,) * len(shape))
    kv_w = ATT_KV_HEADS * ATT_DH
    cache_spec = pl.BlockSpec((None, PAST_LEN, kv_w), lambda i: (i, 0, 0))
    return pl.pallas_call(
        _odd_lat_kernel,
        grid=(DEC_BATCH,),
        in_specs=[seq_spec(n) for n in ODD_SIZES]
        + [cache_spec, cache_spec, full((3, CONV_CH)), full((1, ATT_DH)), full((1, ATT_DH)),
           full((DEC_SEQ, ATT_DH)), full((DEC_SEQ, ATT_DH))],
        out_specs=seq_spec(D_MODEL),
        out_shape=jax.ShapeDtypeStruct((DEC_BATCH * DEC_SEQ, D_MODEL), BF16),
        compiler_params=_params(1),
        name="odd_lat",
    )(bg, cg, u, q, k, v, cache_k, cache_v, conv_w, qn_g, kn_g, cos, sin)


def _rope_tables():
    t = jnp.arange(DEC_SEQ)
    row = (t // GRID_W).astype(F32)
    col = (t % GRID_W).astype(F32)
    n_freq = ATT_DH // 4
    freqs = ROPE_THETA ** (-jnp.arange(n_freq, dtype=F32) / n_freq)
    ang = jnp.concatenate([row[:, None] * freqs, col[:, None] * freqs], axis=-1)
    cos = jnp.repeat(jnp.cos(ang), 2, axis=-1)
    sin = jnp.stack([-jnp.sin(ang), jnp.sin(ang)], axis=-1).reshape(DEC_SEQ, ATT_DH)
    return cos, sin


def _route(scores, sel):
    n_tok = sel.shape[1]
    neg = -jnp.inf
    sub = lax.broadcasted_iota(jnp.int32, (GROUP_SIZE, n_tok), 0).astype(F32)
    blocks = [sel[g * GROUP_SIZE:(g + 1) * GROUP_SIZE, :] for g in range(N_GROUPS)]
    grp = []
    for blk in blocks:
        m1 = jnp.max(blk, axis=0, keepdims=True)
        i1 = jnp.min(jnp.where(blk == m1, sub, float(GROUP_SIZE)), axis=0, keepdims=True)
        m2 = jnp.max(jnp.where(sub == i1, neg, blk), axis=0, keepdims=True)
        grp.append(m1 + m2)
    masked = []
    for g in range(N_GROUPS):
        ahead = jnp.zeros((1, n_tok), F32)
        for o in range(N_GROUPS):
            if o == g:
                continue
            wins = grp[o] >= grp[g] if o < g else grp[o] > grp[g]
            ahead = ahead + jnp.where(wins, 1.0, 0.0)
        masked.append(jnp.where(ahead < float(TOPK_GROUPS), blocks[g], neg))
    val = jnp.concatenate(masked, axis=0)
    row = lax.broadcasted_iota(jnp.int32, (N_EXPERTS, n_tok), 0).astype(F32)
    w = jnp.zeros((N_EXPERTS, n_tok), F32)
    for _ in range(TOP_K):
        m = jnp.max(val, axis=0, keepdims=True)
        idx = jnp.min(jnp.where(val == m, row, float(N_EXPERTS)), axis=0, keepdims=True)
        pick = row == idx
        w = jnp.where(pick, scores, w)
        val = jnp.where(pick, neg, val)
    return w / jnp.sum(w, axis=0, keepdims=True) * ROUTED_SCALE


def _moe_kernel(y_ref, mod_ref, rwt_ref, rb_ref, wg_ref, wu_ref, wd_ref, sg_ref, su_ref, sd_ref,
                lng_ref, lnb_ref, o_ref, xb_ref, gate_ref, acc_ref, *, n_steps):
    j = pl.program_id(1)
    n_tok = y_ref.shape[0]

    @pl.when(j == 0)
    def _():
        x = y_ref[...] * (1.0 + mod_ref[4:5, :]) + mod_ref[3:4, :]
        x_hi, x_lo = _split(x)
        xb_ref[...] = x_hi
        w_hi, w_lo = _split(rwt_ref[...])
        logits = _dot_nt(w_hi, x_hi) + _dot_nt(w_lo, x_hi) + _dot_nt(w_hi, x_lo)
        scores = 1.0 / (1.0 + jnp.exp(-logits))
        gate_t = _route(scores, scores + rb_ref[...])
        gate_ref[...] = jnp.concatenate([gate_t, jnp.zeros_like(gate_t)], axis=0).T
        sgu = jnp.concatenate([sg_ref[...].astype(BF16), su_ref[...].astype(BF16)], axis=1)
        hs = _dot(x_hi, sgu)
        h = (_silu(hs[:, :EXPERT_HIDDEN]) * hs[:, EXPERT_HIDDEN:]).astype(BF16)
        acc_ref[...] = _dot(h, sd_ref[...].astype(BF16))

    xb = xb_ref[...]
    g_rot = pltpu.roll(gate_ref[...], jnp.bitwise_and(LANES - j * MOE_EPS, LANES - 1), 1)
    acc = acc_ref[...]
    for p in range(MOE_EPS // 2):
        wg2 = jnp.concatenate([wg_ref[2 * p].astype(BF16), wg_ref[2 * p + 1].astype(BF16)], axis=1)
        wu2 = jnp.concatenate([wu_ref[2 * p].astype(BF16), wu_ref[2 * p + 1].astype(BF16)], axis=1)
        hg = _dot(xb, wg2)
        hu = _dot(xb, wu2)
        g2 = jnp.concatenate(
            [jnp.broadcast_to(g_rot[:, 2 * p:2 * p + 1], (n_tok, EXPERT_HIDDEN)),
             jnp.broadcast_to(g_rot[:, 2 * p + 1:2 * p + 2], (n_tok, EXPERT_HIDDEN))], axis=1)
        h = (_silu(hg) * hu * g2).astype(BF16)
        wd2 = wd_ref[2 * p:2 * p + 2].reshape(2 * EXPERT_HIDDEN, D_MODEL).astype(BF16)
        acc = acc + _dot(h, wd2)
    acc_ref[...] = acc

    @pl.when(j == n_steps - 1)
    def _():
        o_ref[...] = _post_norm(y_ref[...], acc_ref[...], mod_ref[5:6, :], lng_ref[...], lnb_ref[...])


def _moe(y, mods, layer, rwt, rb, exp_w_gate, exp_w_up, exp_w_down, sg, su, sd, ln_g, ln_b, seq_len):
    t = y.shape[0]
    tile = MOE_TILE
    n_steps = N_EXPERTS // MOE_EPS
    mod_idx = ((lambda i, j: (i * tile // seq_len, 0, 0)) if mods.shape[0] > 1
               else (lambda i, j: (0, 0, 0)))

    def full(shape):
        return pl.BlockSpec(shape, lambda i, j: (0,) * len(shape))
    return pl.pallas_call(
        functools.partial(_moe_kernel, n_steps=n_steps),
        grid=(t // tile, n_steps),
        in_specs=[pl.BlockSpec((tile, D_MODEL), lambda i, j: (i, 0)),
                  pl.BlockSpec((None, N_MOD, D_MODEL), mod_idx),
                  full((N_EXPERTS, D_MODEL)),
                  full((N_EXPERTS, 1)),
                  pl.BlockSpec((None, MOE_EPS, D_MODEL, EXPERT_HIDDEN), lambda i, j: (layer, j, 0, 0)),
                  pl.BlockSpec((None, MOE_EPS, D_MODEL, EXPERT_HIDDEN), lambda i, j: (layer, j, 0, 0)),
                  pl.BlockSpec((None, MOE_EPS, EXPERT_HIDDEN, D_MODEL), lambda i, j: (layer, j, 0, 0)),
                  full((D_MODEL, EXPERT_HIDDEN)),
                  full((D_MODEL, EXPERT_HIDDEN)),
                  full((EXPERT_HIDDEN, D_MODEL)),
                  full((1, D_MODEL)),
                  full((1, D_MODEL))],
        out_specs=pl.BlockSpec((tile, D_MODEL), lambda i, j: (i, 0)),
        out_shape=jax.ShapeDtypeStruct((t, D_MODEL), F32),
        scratch_shapes=[pltpu.VMEM((tile, D_MODEL), BF16),
                        pltpu.VMEM((tile, LANES), F32),
                        pltpu.VMEM((tile, D_MODEL), F32)],
        compiler_params=_params(2),
        name="moe",
    )(y, mods, rwt, rb, exp_w_gate, exp_w_up, exp_w_down, sg, su, sd,
      ln_g.reshape(1, D_MODEL), ln_b.reshape(1, D_MODEL))


def kernel(x_prompt, x_sample, c, state_ret, cache_na_k, cache_na_v, cache_att_k, cache_att_v, c_ctx,
           w_mod, b_mod, ln_g, ln_b, w_in_even, w_out_even, ret_decay_logit, na_rpb,
           w_in_odd, w_out_odd, conv_w, q_norm_g, k_norm_g, router_w, router_b,
           exp_w_gate, exp_w_up, exp_w_down, sh_w_gate, sh_w_up, sh_w_down):
    yp = x_prompt.reshape(BATCH * SEQ, D_MODEL)
    ys = x_sample.reshape(DEC_BATCH * DEC_SEQ, D_MODEL)

    cond = jnp.zeros((COND_ROWS, D_MODEL), F32).at[0].set(c_ctx).at[1:1 + DEC_BATCH].set(c)
    mods = _adaln(cond, w_mod, b_mod).reshape(DEPTH, COND_ROWS, N_MOD, D_MODEL)

    out = {}
    for l in range(DEPTH):
        i = l // 2
        mp = mods[l, 0:1]
        ms = mods[l, 1:1 + DEC_BATCH]
        if l % 2 == 0:
            w_in = w_in_even[i].astype(BF16)
            w_out = w_out_even[i].astype(BF16)
            decay_rows = jnp.broadcast_to(ret_decay_logit[i].reshape(2 * RET_HEADS, 1), (2 * RET_HEADS, LANES))
            p_dt = (BF16, BF16, BF16, F32, BF16, F32, F32)
            s_dt = (BF16, BF16, BF16, F32, BF16, BF16, BF16)
            pp = _inproj(yp, mp, w_in, EVEN_SIZES, p_dt, SEQ)
            sp = _inproj(ys, ms, w_in, EVEN_SIZES, s_dt, DEC_SEQ)
            mix_p, st = _even_ctx(*pp, decay_rows)
            mix_s = _even_lat(
                *sp,
                state_ret[:, i].reshape(DEC_BATCH, 2 * RET_HEADS, RET_DK, RET_DV),
                cache_na_k[:, i].reshape(DEC_BATCH, PAST_LEN, NA_HEADS * NA_DH),
                cache_na_v[:, i].reshape(DEC_BATCH, PAST_LEN, NA_HEADS * NA_DH),
                decay_rows, na_rpb[i].reshape(-1))
            out.setdefault("st", []).append(st.reshape(BATCH, 2, RET_HEADS, RET_DK, RET_DV))
            out.setdefault("na_k", []).append(pp[5].reshape(BATCH, SEQ, NA_HEADS, NA_DH))
            out.setdefault("na_v", []).append(pp[6].reshape(BATCH, SEQ, NA_HEADS, NA_DH))
        else:
            w_in = w_in_odd[i].astype(BF16)
            w_out = w_out_odd[i].astype(BF16)
            p_dt = (F32,) * 6
            s_dt = (F32, F32, F32, F32, F32, BF16)
            pp = _inproj(yp, mp, w_in, ODD_SIZES, p_dt, SEQ)
            sp = _inproj(ys, ms, w_in, ODD_SIZES, s_dt, DEC_SEQ)
            qg = q_norm_g[i].reshape(1, ATT_DH)
            kg = k_norm_g[i].reshape(1, ATT_DH)
            mix_p, k_new = _odd_ctx(*pp, conv_w[i], qg, kg)
            cos, sin = _rope_tables()
            kv_w = ATT_KV_HEADS * ATT_DH
            mix_s = _odd_lat(*sp,
                             cache_att_k[:, i].reshape(DEC_BATCH, PAST_LEN, kv_w),
                             cache_att_v[:, i].reshape(DEC_BATCH, PAST_LEN, kv_w),
                             conv_w[i], qg, kg, cos, sin)
            out.setdefault("att_k", []).append(k_new.reshape(BATCH, SEQ, ATT_KV_HEADS, ATT_DH))
            out.setdefault("att_v", []).append(pp[5].reshape(BATCH, SEQ, ATT_KV_HEADS, ATT_DH))
        yp = _outproj(mix_p, yp, mp, w_out, ln_g[l, 0], ln_b[l, 0], SEQ)
        ys = _outproj(mix_s, ys, ms, w_out, ln_g[l, 0], ln_b[l, 0], DEC_SEQ)
        rwt = router_w[l].T
        rb = router_b[l].reshape(N_EXPERTS, 1)
        moe_w = (rwt, rb, exp_w_gate, exp_w_up, exp_w_down, sh_w_gate[l], sh_w_up[l], sh_w_down[l],
                 ln_g[l, 1], ln_b[l, 1])
        yp = _moe(yp, mp, l, *moe_w, SEQ)
        ys = _moe(ys, ms, l, *moe_w, DEC_SEQ)

    return (yp.reshape(BATCH, SEQ, D_MODEL),
            ys.reshape(DEC_BATCH, DEC_SEQ, D_MODEL),
            jnp.stack(out["st"], axis=1),
            jnp.stack(out["na_k"], axis=1),
            jnp.stack(out["na_v"], axis=1),
            jnp.stack(out["att_k"], axis=1),
            jnp.stack(out["att_v"], axis=1))
```

```python
import functools

import jax
import jax.numpy as jnp
from jax import lax
from jax.experimental import pallas as pl
from jax.experimental.pallas import tpu as pltpu

F32 = jnp.float32
BF16 = jnp.bfloat16

D_MODEL = 1024
BATCH = 32
SEQ = 256
DEPTH = 2
DEC_BATCH = 4
DEC_SEQ = 1024
PAST_LEN = 512
GRID_W = 64
GRID_H = DEC_SEQ // GRID_W
MIX_HALF = D_MODEL // 2
RET_HEADS = 4
RET_DV = MIX_HALF // RET_HEADS
RET_DK = RET_DV // 2
NA_HEADS = 4
NA_DH = MIX_HALF // NA_HEADS
NA_WIN_R = 8
NA_WIN_C = 16
CONV_CH = MIX_HALF
ATT_HEADS = 4
ATT_KV_HEADS = 2
ATT_DH = MIX_HALF // ATT_HEADS
ROPE_THETA = 10000.0
N_EXPERTS = 64
EXPERT_HIDDEN = D_MODEL // 8
TOP_K = 8
N_GROUPS = 8
GROUP_SIZE = N_EXPERTS // N_GROUPS
TOPK_GROUPS = 4
ROUTED_SCALE = 2.5
DEEPNORM_ALPHA = (2 * DEPTH) ** 0.25
LN_EPS = 1e-5
RMS_EPS = 1e-6
EVEN_SIZES = (RET_HEADS * RET_DK, RET_HEADS * RET_DK, RET_HEADS * RET_DV, RET_HEADS * RET_DV,
              NA_HEADS * NA_DH, NA_HEADS * NA_DH, NA_HEADS * NA_DH)
ODD_SIZES = (CONV_CH, CONV_CH, CONV_CH, ATT_HEADS * ATT_DH, ATT_KV_HEADS * ATT_DH, ATT_KV_HEADS * ATT_DH)
N_MOD = 6
RPB_ROWS = 2 * NA_WIN_R - 1
RPB_COLS = 2 * NA_WIN_C - 1

LANES = 128
COND_ROWS = 16
V7X_VMEM_LIMIT = 56 * 1024 * 1024
MOE_VMEM_LIMIT = 60 * 1024 * 1024

PROJ_TILE = 512
MOE_TILE = 1024
MOE_EPS = 8
ADALN_TN = 1536


def _params(n_axes, vmem_limit=V7X_VMEM_LIMIT):
    return pltpu.CompilerParams(dimension_semantics=("arbitrary",) * n_axes,
                                vmem_limit_bytes=vmem_limit)


def _dot(a, b):
    return jnp.dot(a, b, preferred_element_type=F32)


def _dot_nt(a, b):
    return lax.dot_general(a, b, (((1,), (1,)), ((), ())), preferred_element_type=F32)


def _dot_tn(a, b):
    return lax.dot_general(a, b, (((0,), (0,)), ((), ())), preferred_element_type=F32)


def _split(a):
    hi = a.astype(BF16)
    lo = (a - hi.astype(F32)).astype(BF16)
    return hi, lo


def _silu(x):
    return x / (1.0 + jnp.exp(-x))


def _log_sigmoid(x):
    return jnp.minimum(x, 0.0) - jnp.log(1.0 + jnp.exp(-jnp.abs(x)))


def _layer_norm(z, g, b):
    mu = jnp.mean(z, axis=-1, keepdims=True)
    zc = z - mu
    var = jnp.mean(zc * zc, axis=-1, keepdims=True)
    return zc * lax.rsqrt(var + LN_EPS) * g + b


def _post_norm(x, y, gate, g, b):
    return _layer_norm(DEEPNORM_ALPHA * x + (1.0 + gate) * y, g, b)


def _head_norm(x):
    mu = jnp.mean(x, axis=-1, keepdims=True)
    xc = x - mu
    var = jnp.mean(xc * xc, axis=-1, keepdims=True)
    return xc * lax.rsqrt(var + LN_EPS)


def _rms_norm(x, g):
    return x * lax.rsqrt(jnp.mean(x * x, axis=-1, keepdims=True) + RMS_EPS) * g


def _softmax_pv(parts):
    m = None
    for s, _ in parts:
        mi = jnp.max(s, axis=-1, keepdims=True)
        m = mi if m is None else jnp.maximum(m, mi)
    l = None
    o = None
    for s, v in parts:
        p = jnp.exp(s - m)
        li = jnp.sum(p, axis=-1, keepdims=True)
        oi = _dot(p.astype(BF16), v)
        l = li if l is None else l + li
        o = oi if o is None else o + oi
    return o / l


def _adaln_kernel(c_ref, w_ref, b_ref, o_ref):
    a_hi, a_lo = _split(_silu(c_ref[...]))
    w_hi, w_lo = _split(w_ref[...])
    o_ref[...] = _dot(a_hi, w_hi) + _dot(a_lo, w_hi) + _dot(a_hi, w_lo) + b_ref[...]


def _adaln(cond, w_mod, b_mod):
    n = w_mod.shape[-1]
    return pl.pallas_call(
        _adaln_kernel,
        grid=(DEPTH, n // ADALN_TN),
        in_specs=[pl.BlockSpec((COND_ROWS, D_MODEL), lambda l, j: (0, 0)),
                  pl.BlockSpec((None, D_MODEL, ADALN_TN), lambda l, j: (l, 0, j)),
                  pl.BlockSpec((None, 1, ADALN_TN), lambda l, j: (l, 0, j))],
        out_specs=pl.BlockSpec((None, COND_ROWS, ADALN_TN), lambda l, j: (l, 0, j)),
        out_shape=jax.ShapeDtypeStruct((DEPTH, COND_ROWS, n), F32),
        compiler_params=_params(2),
        name="adaln",
    )(cond, w_mod, b_mod.reshape(DEPTH, 1, n))


def _inproj_kernel(x_ref, mod_ref, w_ref, *out_refs, sizes):
    shift = mod_ref[0:1, :]
    scale = mod_ref[1:2, :]
    h = (x_ref[...] * (1.0 + scale) + shift).astype(BF16)
    y = _dot(h, w_ref[...])
    off = 0
    for o_ref, n in zip(out_refs, sizes):
        if len(o_ref.shape) == 3:
            for h in range(o_ref.shape[1]):
                o_ref[:, h, :] = y[:, off + h * LANES:off + (h + 1) * LANES].astype(o_ref.dtype)
        else:
            o_ref[...] = y[:, off:off + n].astype(o_ref.dtype)
        off += n


def _inproj(x, mods, w_bf16, sizes, dtypes, seq_len, by_head=()):
    t = x.shape[0]
    tile = PROJ_TILE

    def out_spec(k, n):
        if k in by_head:
            return pl.BlockSpec((tile, n // LANES, LANES), lambda i: (i, 0, 0))
        return pl.BlockSpec((tile, n), lambda i: (i, 0))

    def out_shape(k, n, dt):
        return jax.ShapeDtypeStruct((t, n // LANES, LANES) if k in by_head else (t, n), dt)
    mod_idx = (lambda i: (i * tile // seq_len, 0, 0)) if mods.shape[0] > 1 else (lambda i: (0, 0, 0))
    return pl.pallas_call(
        functools.partial(_inproj_kernel, sizes=sizes),
        grid=(t // tile,),
        in_specs=[pl.BlockSpec((tile, D_MODEL), lambda i: (i, 0)),
                  pl.BlockSpec((None, N_MOD, D_MODEL), mod_idx),
                  pl.BlockSpec(w_bf16.shape, lambda i: (0, 0))],
        out_specs=[out_spec(k, n) for k, n in enumerate(sizes)],
        out_shape=[out_shape(k, n, dt) for k, (n, dt) in enumerate(zip(sizes, dtypes))],
        compiler_params=_params(1),
        name="inproj",
    )(x, mods, w_bf16)


def _outproj_kernel(mix_ref, x_ref, mod_ref, w_ref, g_ref, b_ref, o_ref):
    y = _dot(mix_ref[...], w_ref[...])
    o_ref[...] = _post_norm(x_ref[...], y, mod_ref[2:3, :], g_ref[...], b_ref[...])


def _outproj(mix, x, mods, w_bf16, ln_g, ln_b, seq_len):
    t = x.shape[0]
    tile = PROJ_TILE
    mod_idx = (lambda i: (i * tile // seq_len, 0, 0)) if mods.shape[0] > 1 else (lambda i: (0, 0, 0))
    return pl.pallas_call(
        _outproj_kernel,
        grid=(t // tile,),
        in_specs=[pl.BlockSpec((tile, D_MODEL), lambda i: (i, 0)),
                  pl.BlockSpec((tile, D_MODEL), lambda i: (i, 0)),
                  pl.BlockSpec((None, N_MOD, D_MODEL), mod_idx),
                  pl.BlockSpec((D_MODEL, D_MODEL), lambda i: (0, 0)),
                  pl.BlockSpec((1, D_MODEL), lambda i: (0, 0)),
                  pl.BlockSpec((1, D_MODEL), lambda i: (0, 0))],
        out_specs=pl.BlockSpec((tile, D_MODEL), lambda i: (i, 0)),
        out_shape=jax.ShapeDtypeStruct((t, D_MODEL), F32),
        compiler_params=_params(1),
        name="outproj",
    )(mix, x, mods, w_bf16, ln_g.reshape(1, D_MODEL), ln_b.reshape(1, D_MODEL))


def _decay_matrix(length, lgf, lgb):
    ii = lax.broadcasted_iota(jnp.int32, (length, length), 0)
    jj = lax.broadcasted_iota(jnp.int32, (length, length), 1)
    rel = (ii - jj).astype(F32)
    fwd = jnp.where(rel >= 0.0, jnp.exp(lgf * jnp.maximum(rel, 0.0)), 0.0)
    bwd = jnp.where(rel <= 0.0, jnp.exp(lgb * jnp.maximum(-rel, 0.0)), 0.0)
    return fwd + bwd


def _retention_head(q, k, v, lgf, lgb, s0f, s0b):
    length = q.shape[0]
    s = _dot_nt(q.astype(BF16), k.astype(BF16))
    o = _dot((s * _decay_matrix(length, lgf, lgb)).astype(BF16), v)
    if s0f is not None:
        pos = lax.broadcasted_iota(jnp.int32, (length, 1), 0).astype(F32)
        qf = q * jnp.exp(lgf * (pos + 1.0))
        qb = q * jnp.exp(lgb * (length - pos))
        o = o + _dot(qf.astype(BF16), s0f.astype(BF16)) + _dot(qb.astype(BF16), s0b.astype(BF16))
    return o


def _even_ctx_kernel(qr_ref, kr_ref, vr_ref, gr_ref, qn_ref, kn_ref, vn_ref, dl_ref, mix_ref, st_ref):
    length = qr_ref.shape[0]
    lg = _log_sigmoid(dl_ref[...])
    pos = lax.broadcasted_iota(jnp.int32, (length, 1), 0).astype(F32)
    for h in range(RET_HEADS):
        lgf = lg[h:h + 1, 0:1]
        lgb = lg[RET_HEADS + h:RET_HEADS + h + 1, 0:1]
        q = qr_ref[:, h * RET_DK:(h + 1) * RET_DK].astype(F32)
        k = kr_ref[:, h * RET_DK:(h + 1) * RET_DK].astype(F32) * (RET_DK ** -0.5)
        v = vr_ref[:, h * RET_DV:(h + 1) * RET_DV]
        o = _retention_head(q, k, v, lgf, lgb, None, None)
        kf = k * jnp.exp(lgf * (length - 1.0 - pos))
        kb = k * jnp.exp(lgb * pos)
        st_ref[h] = _dot_tn(kf.astype(BF16), v)
        st_ref[RET_HEADS + h] = _dot_tn(kb.astype(BF16), v)
        g = gr_ref[:, h * RET_DV:(h + 1) * RET_DV]
        mix_ref[:, h * RET_DV:(h + 1) * RET_DV] = (_head_norm(o) * _silu(g)).astype(mix_ref.dtype)
    base = RET_HEADS * RET_DV
    for h in range(NA_HEADS):
        q = qn_ref[:, h * NA_DH:(h + 1) * NA_DH]
        k = kn_ref[:, h, :].astype(BF16)
        v = vn_ref[:, h, :].astype(BF16)
        s = _dot_nt(q, k) * (NA_DH ** -0.5)
        mix_ref[:, base + h * NA_DH:base + (h + 1) * NA_DH] = _softmax_pv([(s, v)]).astype(mix_ref.dtype)


def _even_ctx(qr, kr, vr, gr, qn, kn, vn, decay_rows):
    def seq_spec(n):
        return pl.BlockSpec((SEQ, n), lambda i: (i, 0))
    head_spec = pl.BlockSpec((SEQ, NA_HEADS, NA_DH), lambda i: (i, 0, 0))
    return pl.pallas_call(
        _even_ctx_kernel,
        grid=(BATCH,),
        in_specs=[seq_spec(n) for n in EVEN_SIZES[:5]] + [head_spec, head_spec]
        + [pl.BlockSpec((2 * RET_HEADS, LANES), lambda i: (0, 0))],
        out_specs=[seq_spec(D_MODEL),
                   pl.BlockSpec((None, 2 * RET_HEADS, RET_DK, RET_DV), lambda i: (i, 0, 0, 0))],
        out_shape=[jax.ShapeDtypeStruct((BATCH * SEQ, D_MODEL), BF16),
                   jax.ShapeDtypeStruct((BATCH, 2 * RET_HEADS, RET_DK, RET_DV), F32)],
        compiler_params=_params(1),
        name="even_ctx",
    )(qr, kr, vr, gr, qn, kn, vn, decay_rows)


def _build_rpb_tiles(rpb_ref, tile_ref):
    qc = lax.broadcasted_iota(jnp.int32, (GRID_W, LANES), 0)
    kc = lax.broadcasted_iota(jnp.int32, (GRID_W, LANES), 1) % GRID_W
    diff = kc - qc + (NA_WIN_C - 1)
    start = jnp.clip(qc - NA_WIN_C // 2, 0, GRID_W - NA_WIN_C)
    win = (kc >= start) & (kc < start + NA_WIN_C)

    def body(idx, carry):
        t = jnp.zeros((GRID_W, LANES), F32)
        for d in range(RPB_COLS):
            t = jnp.where(diff == d, rpb_ref[idx * RPB_COLS + d], t)
        tile_ref[idx] = jnp.where(win, t, -jnp.inf)
        return carry

    lax.fori_loop(0, NA_HEADS * RPB_ROWS, body, 0)


def _build_bias(tile_ref, bias_ref, head):
    left = lax.broadcasted_iota(jnp.int32, (GRID_W, LANES), 1) < GRID_W
    neg = jnp.full((GRID_W, LANES), -jnp.inf, F32)
    rows_w = min(NA_WIN_R, GRID_H)
    for qr in range(GRID_H):
        rs = min(max(qr - rows_w // 2, 0), GRID_H - rows_w)

        def tile(kr):
            if rs <= kr < rs + rows_w:
                return tile_ref[head * RPB_ROWS + kr - qr + NA_WIN_R - 1]
            return neg

        pieces = [jnp.where(left, tile(2 * a), tile(2 * a + 1)) for a in range(GRID_H // 2)]
        bias_ref[qr * GRID_W:(qr + 1) * GRID_W, :] = jnp.concatenate(pieces, axis=1)


def _even_lat_kernel(qr_ref, kr_ref, vr_ref, gr_ref, qn_ref, kn_ref, vn_ref, s0_ref, ck_ref, cv_ref,
                     dl_ref, rpb_ref, mix_ref, tile_ref, bias_ref):
    @pl.when(pl.program_id(0) == 0)
    def _():
        _build_rpb_tiles(rpb_ref, tile_ref)

    lg = _log_sigmoid(dl_ref[...])
    for h in range(RET_HEADS):
        lgf = lg[h:h + 1, 0:1]
        lgb = lg[RET_HEADS + h:RET_HEADS + h + 1, 0:1]
        q = qr_ref[:, h * RET_DK:(h + 1) * RET_DK].astype(F32)
        k = kr_ref[:, h * RET_DK:(h + 1) * RET_DK].astype(F32) * (RET_DK ** -0.5)
        v = vr_ref[:, h * RET_DV:(h + 1) * RET_DV]
        o = _retention_head(q, k, v, lgf, lgb, s0_ref[h], s0_ref[RET_HEADS + h])
        g = gr_ref[:, h * RET_DV:(h + 1) * RET_DV]
        mix_ref[:, h * RET_DV:(h + 1) * RET_DV] = (_head_norm(o) * _silu(g)).astype(mix_ref.dtype)
    base = RET_HEADS * RET_DV
    scale = NA_DH ** -0.5
    for h in range(NA_HEADS):
        _build_bias(tile_ref, bias_ref, h)
        sl = slice(h * NA_DH, (h + 1) * NA_DH)
        q = qn_ref[:, sl]
        s_band = _dot_nt(q, kn_ref[:, sl]) * scale + bias_ref[...]
        s_ctx = _dot_nt(q, ck_ref[:, sl].astype(BF16)) * scale
        o = _softmax_pv([(s_band, vn_ref[:, sl]), (s_ctx, cv_ref[:, sl].astype(BF16))])
        mix_ref[:, base + h * NA_DH:base + (h + 1) * NA_DH] = o.astype(mix_ref.dtype)


def _even_lat(qr, kr, vr, gr, qn, kn, vn, state, cache_k, cache_v, decay_rows, rpb_flat):
    def seq_spec(n):
        return pl.BlockSpec((DEC_SEQ, n), lambda i: (i, 0))
    cache_spec = pl.BlockSpec((None, PAST_LEN, NA_HEADS * NA_DH), lambda i: (i, 0, 0))
    return pl.pallas_call(
        _even_lat_kernel,
        grid=(DEC_BATCH,),
        in_specs=[seq_spec(n) for n in EVEN_SIZES]
        + [pl.BlockSpec((None, 2 * RET_HEADS, RET_DK, RET_DV), lambda i: (i, 0, 0, 0)),
           cache_spec, cache_spec,
           pl.BlockSpec((2 * RET_HEADS, LANES), lambda i: (0, 0)),
           pl.BlockSpec(memory_space=pltpu.SMEM)],
        out_specs=seq_spec(D_MODEL),
        out_shape=jax.ShapeDtypeStruct((DEC_BATCH * DEC_SEQ, D_MODEL), BF16),
        scratch_shapes=[pltpu.VMEM((NA_HEADS * RPB_ROWS, GRID_W, LANES), F32),
                        pltpu.VMEM((DEC_SEQ, DEC_SEQ), F32)],
        compiler_params=_params(1),
        name="even_lat",
    )(qr, kr, vr, gr, qn, kn, vn, state, cache_k, cache_v, decay_rows, rpb_flat)


def _gated_conv(bg_ref, cg_ref, u_ref, w_ref):
    length, ch = u_ref.shape
    z = cg_ref[...] * u_ref[...]
    row = lax.broadcasted_iota(jnp.int32, (length, ch), 0)
    z_prev = jnp.where(row == 0, 0.0, pltpu.roll(z, 1, 0))
    z_next = jnp.where(row == length - 1, 0.0, pltpu.roll(z, length - 1, 0))
    return bg_ref[...] * (z_prev * w_ref[0:1, :] + z * w_ref[1:2, :] + z_next * w_ref[2:3, :])


def _odd_ctx_kernel(bg_ref, cg_ref, u_ref, q_ref, k_ref, v_ref, w_ref, qg_ref, kg_ref, mix_ref, ko_ref):
    mix_ref[:, 0:CONV_CH] = _gated_conv(bg_ref, cg_ref, u_ref, w_ref).astype(mix_ref.dtype)
    group = ATT_HEADS // ATT_KV_HEADS
    scale = ATT_DH ** -0.5
    for kv in range(ATT_KV_HEADS):
        sl = slice(kv * ATT_DH, (kv + 1) * ATT_DH)
        k = _rms_norm(k_ref[:, sl], kg_ref[...])
        ko_ref[:, kv, :] = k
        k = k.astype(BF16)
        v = v_ref[:, kv, :].astype(BF16)
        for g in range(group):
            h = kv * group + g
            q = _rms_norm(q_ref[:, h * ATT_DH:(h + 1) * ATT_DH], qg_ref[...]).astype(BF16)
            o = _softmax_pv([(_dot_nt(q, k) * scale, v)])
            mix_ref[:, CONV_CH + h * ATT_DH:CONV_CH + (h + 1) * ATT_DH] = o.astype(mix_ref.dtype)


def _odd_ctx(bg, cg, u, q, k, v, conv_w, qn_g, kn_g):
    def seq_spec(n):
        return pl.BlockSpec((SEQ, n), lambda i: (i, 0))

    def full(shape):
        return pl.BlockSpec(shape, lambda i: (0,) * len(shape))
    head_spec = pl.BlockSpec((SEQ, ATT_KV_HEADS, ATT_DH), lambda i: (i, 0, 0))
    return pl.pallas_call(
        _odd_ctx_kernel,
        grid=(BATCH,),
        in_specs=[seq_spec(n) for n in ODD_SIZES[:5]] + [head_spec]
        + [full((3, CONV_CH)), full((1, ATT_DH)), full((1, ATT_DH))],
        out_specs=[seq_spec(D_MODEL), head_spec],
        out_shape=[jax.ShapeDtypeStruct((BATCH * SEQ, D_MODEL), BF16),
                   jax.ShapeDtypeStruct((BATCH * SEQ, ATT_KV_HEADS, ATT_DH), F32)],
        compiler_params=_params(1),
        name="odd_ctx",
    )(bg, cg, u, q, k, v, conv_w, qn_g, kn_g)


def _rope(x, cos, sin, even_lane):
    swapped = jnp.where(even_lane, pltpu.roll(x, LANES - 1, 1), pltpu.roll(x, 1, 1))
    return x * cos + swapped * sin


def _odd_lat_kernel(bg_ref, cg_ref, u_ref, q_ref, k_ref, v_ref, ck_ref, cv_ref, w_ref, qg_ref, kg_ref,
                    cos_ref, sin_ref, mix_ref):
    mix_ref[:, 0:CONV_CH] = _gated_conv(bg_ref, cg_ref, u_ref, w_ref).astype(mix_ref.dtype)
    group = ATT_HEADS // ATT_KV_HEADS
    scale = ATT_DH ** -0.5
    length = q_ref.shape[0]
    even_lane = lax.broadcasted_iota(jnp.int32, (length, ATT_DH), 1) % 2 == 0
    cos = cos_ref[...]
    sin = sin_ref[...]
    for kv in range(ATT_KV_HEADS):
        sl = slice(kv * ATT_DH, (kv + 1) * ATT_DH)
        k = _rope(_rms_norm(k_ref[:, sl], kg_ref[...]), cos, sin, even_lane).astype(BF16)
        v = v_ref[:, sl]
        kc = ck_ref[:, sl].astype(BF16)
        vc = cv_ref[:, sl].astype(BF16)
        for g in range(group):
            h = kv * group + g
            q = _rms_norm(q_ref[:, h * ATT_DH:(h + 1) * ATT_DH], qg_ref[...])
            q = _rope(q, cos, sin, even_lane).astype(BF16)
            o = _softmax_pv([(_dot_nt(q, k) * scale, v), (_dot_nt(q, kc) * scale, vc)])
            mix_ref[:, CONV_CH + h * ATT_DH:CONV_CH + (h + 1) * ATT_DH] = o.astype(mix_ref.dtype)


def _odd_lat(bg, cg, u, q, k, v, cache_k, cache_v, conv_w, qn_g, kn_g, cos, sin):
    def seq_spec(n):
        return pl.BlockSpec((DEC_SEQ, n), lambda i: (i, 0))

    def full(shape):
        return pl.BlockSpec(shape, lambda i: (0,) * len(shape))
    kv_w = ATT_KV_HEADS * ATT_DH
    cache_spec = pl.BlockSpec((None, PAST_LEN, kv_w), lambda i: (i, 0, 0))
    return pl.pallas_call(
        _odd_lat_kernel,
        grid=(DEC_BATCH,),
        in_specs=[seq_spec(n) for n in ODD_SIZES]
        + [cache_spec, cache_spec, full((3, CONV_CH)), full((1, ATT_DH)), full((1, ATT_DH)),
           full((DEC_SEQ, ATT_DH)), full((DEC_SEQ, ATT_DH))],
        out_specs=seq_spec(D_MODEL),
        out_shape=jax.ShapeDtypeStruct((DEC_BATCH * DEC_SEQ, D_MODEL), BF16),
        compiler_params=_params(1),
        name="odd_lat",
    )(bg, cg, u, q, k, v, cache_k, cache_v, conv_w, qn_g, kn_g, cos, sin)


def _rope_tables():
    t = jnp.arange(DEC_SEQ)
    row = (t // GRID_W).astype(F32)
    col = (t % GRID_W).astype(F32)
    n_freq = ATT_DH // 4
    freqs = ROPE_THETA ** (-jnp.arange(n_freq, dtype=F32) / n_freq)
    ang = jnp.concatenate([row[:, None] * freqs, col[:, None] * freqs], axis=-1)
    cos = jnp.repeat(jnp.cos(ang), 2, axis=-1)
    sin = jnp.stack([-jnp.sin(ang), jnp.sin(ang)], axis=-1).reshape(DEC_SEQ, ATT_DH)
    return cos, sin


def _route(scores, sel):
    n_tok = sel.shape[1]
    neg = -jnp.inf
    sub = lax.broadcasted_iota(jnp.int32, (GROUP_SIZE, n_tok), 0).astype(F32)
    blocks = [sel[g * GROUP_SIZE:(g + 1) * GROUP_SIZE, :] for g in range(N_GROUPS)]
    grp = []
    for blk in blocks:
        m1 = jnp.max(blk, axis=0, keepdims=True)
        i1 = jnp.min(jnp.where(blk == m1, sub, float(GROUP_SIZE)), axis=0, keepdims=True)
        m2 = jnp.max(jnp.where(sub == i1, neg, blk), axis=0, keepdims=True)
        grp.append(m1 + m2)
    masked = []
    for g in range(N_GROUPS):
        ahead = jnp.zeros((1, n_tok), F32)
        for o in range(N_GROUPS):
            if o == g:
                continue
            wins = grp[o] >= grp[g] if o < g else grp[o] > grp[g]
            ahead = ahead + jnp.where(wins, 1.0, 0.0)
        masked.append(jnp.where(ahead < float(TOPK_GROUPS), blocks[g], neg))
    val = jnp.concatenate(masked, axis=0)
    row = lax.broadcasted_iota(jnp.int32, (N_EXPERTS, n_tok), 0).astype(F32)
    w = jnp.zeros((N_EXPERTS, n_tok), F32)
    for _ in range(TOP_K):
        m = jnp.max(val, axis=0, keepdims=True)
        idx = jnp.min(jnp.where(val == m, row, float(N_EXPERTS)), axis=0, keepdims=True)
        pick = row == idx
        w = jnp.where(pick, scores, w)
        val = jnp.where(pick, neg, val)
    return w / jnp.sum(w, axis=0, keepdims=True) * ROUTED_SCALE


def _moe_kernel(y_ref, mod_ref, rwt_ref, rb_ref, wg_ref, wu_ref, wd_ref, sg_ref, su_ref, sd_ref,
                lng_ref, lnb_ref, o_ref, xb_ref, gate_ref, *, n_steps):
    j = pl.program_id(1)
    n_tok = y_ref.shape[0]
    acc_ref = o_ref

    @pl.when(j == 0)
    def _():
        x = y_ref[...] * (1.0 + mod_ref[4:5, :]) + mod_ref[3:4, :]
        x_hi, x_lo = _split(x)
        xb_ref[...] = x_hi
        w_hi, w_lo = _split(rwt_ref[...])
        logits = _dot_nt(w_hi, x_hi) + _dot_nt(w_lo, x_hi) + _dot_nt(w_hi, x_lo)
        scores = 1.0 / (1.0 + jnp.exp(-logits))
        gate_t = _route(scores, scores + rb_ref[...])
        gate_ref[...] = jnp.concatenate([gate_t, jnp.zeros_like(gate_t)], axis=0).T
        sgu = jnp.concatenate([sg_ref[...].astype(BF16), su_ref[...].astype(BF16)], axis=1)
        hs = _dot(x_hi, sgu)
        h = (_silu(hs[:, :EXPERT_HIDDEN]) * hs[:, EXPERT_HIDDEN:]).astype(BF16)
        acc_ref[...] = _dot(h, sd_ref[...].astype(BF16))

    xb = xb_ref[...]
    g_rot = pltpu.roll(gate_ref[...], jnp.bitwise_and(LANES - j * MOE_EPS, LANES - 1), 1)
    acc = acc_ref[...]
    for p in range(MOE_EPS // 2):
        wg2 = jnp.concatenate([wg_ref[2 * p].astype(BF16), wg_ref[2 * p + 1].astype(BF16)], axis=1)
        wu2 = jnp.concatenate([wu_ref[2 * p].astype(BF16), wu_ref[2 * p + 1].astype(BF16)], axis=1)
        hg = _dot(xb, wg2)
        hu = _dot(xb, wu2)
        g2 = jnp.concatenate(
            [jnp.broadcast_to(g_rot[:, 2 * p:2 * p + 1], (n_tok, EXPERT_HIDDEN)),
             jnp.broadcast_to(g_rot[:, 2 * p + 1:2 * p + 2], (n_tok, EXPERT_HIDDEN))], axis=1)
        h = (_silu(hg) * hu * g2).astype(BF16)
        wd2 = wd_ref[2 * p:2 * p + 2].reshape(2 * EXPERT_HIDDEN, D_MODEL).astype(BF16)
        acc = acc + _dot(h, wd2)
    acc_ref[...] = acc

    @pl.when(j == n_steps - 1)
    def _():
        o_ref[...] = _post_norm(y_ref[...], acc_ref[...], mod_ref[5:6, :], lng_ref[...], lnb_ref[...])


def _moe(y, mods, layer, rwt, rb, exp_w_gate, exp_w_up, exp_w_down, sg, su, sd, ln_g, ln_b, seq_len):
    t = y.shape[0]
    tile = MOE_TILE
    n_steps = N_EXPERTS // MOE_EPS
    mod_idx = ((lambda i, j: (i * tile // seq_len, 0, 0)) if mods.shape[0] > 1
               else (lambda i, j: (0, 0, 0)))

    def full(shape):
        return pl.BlockSpec(shape, lambda i, j: (0,) * len(shape), pipeline_mode=pl.Buffered(1))
    return pl.pallas_call(
        functools.partial(_moe_kernel, n_steps=n_steps),
        grid=(t // tile, n_steps),
        in_specs=[pl.BlockSpec((tile, D_MODEL), lambda i, j: (i, 0), pipeline_mode=pl.Buffered(1)),
                  pl.BlockSpec((None, N_MOD, D_MODEL), mod_idx),
                  full((N_EXPERTS, D_MODEL)),
                  full((N_EXPERTS, 1)),
                  pl.BlockSpec((None, MOE_EPS, D_MODEL, EXPERT_HIDDEN), lambda i, j: (layer, j, 0, 0)),
                  pl.BlockSpec((None, MOE_EPS, D_MODEL, EXPERT_HIDDEN), lambda i, j: (layer, j, 0, 0)),
                  pl.BlockSpec((None, MOE_EPS, EXPERT_HIDDEN, D_MODEL), lambda i, j: (layer, j, 0, 0)),
                  full((D_MODEL, EXPERT_HIDDEN)),
                  full((D_MODEL, EXPERT_HIDDEN)),
                  full((EXPERT_HIDDEN, D_MODEL)),
                  full((1, D_MODEL)),
                  full((1, D_MODEL))],
        out_specs=pl.BlockSpec((tile, D_MODEL), lambda i, j: (i, 0)),
        out_shape=jax.ShapeDtypeStruct((t, D_MODEL), F32),
        scratch_shapes=[pltpu.VMEM((tile, D_MODEL), BF16),
                        pltpu.VMEM((tile, LANES), F32)],
        compiler_params=_params(2, MOE_VMEM_LIMIT),
        name="moe",
    )(y, mods, rwt, rb, exp_w_gate, exp_w_up, exp_w_down, sg, su, sd,
      ln_g.reshape(1, D_MODEL), ln_b.reshape(1, D_MODEL))


def kernel(x_prompt, x_sample, c, state_ret, cache_na_k, cache_na_v, cache_att_k, cache_att_v, c_ctx,
           w_mod, b_mod, ln_g, ln_b, w_in_even, w_out_even, ret_decay_logit, na_rpb,
           w_in_odd, w_out_odd, conv_w, q_norm_g, k_norm_g, router_w, router_b,
           exp_w_gate, exp_w_up, exp_w_down, sh_w_gate, sh_w_up, sh_w_down):
    yp = x_prompt.reshape(BATCH * SEQ, D_MODEL)
    ys = x_sample.reshape(DEC_BATCH * DEC_SEQ, D_MODEL)

    cond = jnp.zeros((COND_ROWS, D_MODEL), F32).at[0].set(c_ctx).at[1:1 + DEC_BATCH].set(c)
    mods = _adaln(cond, w_mod, b_mod).reshape(DEPTH, COND_ROWS, N_MOD, D_MODEL)

    out = {}
    for l in range(DEPTH):
        i = l // 2
        mp = mods[l, 0:1]
        ms = mods[l, 1:1 + DEC_BATCH]
        if l % 2 == 0:
            w_in = w_in_even[i].astype(BF16)
            w_out = w_out_even[i].astype(BF16)
            decay_rows = jnp.broadcast_to(ret_decay_logit[i].reshape(2 * RET_HEADS, 1), (2 * RET_HEADS, LANES))
            p_dt = (BF16, BF16, BF16, F32, BF16, F32, F32)
            s_dt = (BF16, BF16, BF16, F32, BF16, BF16, BF16)
            pp = _inproj(yp, mp, w_in, EVEN_SIZES, p_dt, SEQ, by_head=(5, 6))
            sp = _inproj(ys, ms, w_in, EVEN_SIZES, s_dt, DEC_SEQ)
            mix_p, st = _even_ctx(*pp, decay_rows)
            mix_s = _even_lat(
                *sp,
                state_ret[:, i].reshape(DEC_BATCH, 2 * RET_HEADS, RET_DK, RET_DV),
                cache_na_k[:, i].reshape(DEC_BATCH, PAST_LEN, NA_HEADS * NA_DH),
                cache_na_v[:, i].reshape(DEC_BATCH, PAST_LEN, NA_HEADS * NA_DH),
                decay_rows, na_rpb[i].reshape(-1))
            out.setdefault("st", []).append(st.reshape(BATCH, 2, RET_HEADS, RET_DK, RET_DV))
            out.setdefault("na_k", []).append(pp[5].reshape(BATCH, SEQ, NA_HEADS, NA_DH))
            out.setdefault("na_v", []).append(pp[6].reshape(BATCH, SEQ, NA_HEADS, NA_DH))
        else:
            w_in = w_in_odd[i].astype(BF16)
            w_out = w_out_odd[i].astype(BF16)
            p_dt = (F32,) * 6
            s_dt = (F32, F32, F32, F32, F32, BF16)
            pp = _inproj(yp, mp, w_in, ODD_SIZES, p_dt, SEQ, by_head=(5,))
            sp = _inproj(ys, ms, w_in, ODD_SIZES, s_dt, DEC_SEQ)
            qg = q_norm_g[i].reshape(1, ATT_DH)
            kg = k_norm_g[i].reshape(1, ATT_DH)
            mix_p, k_new = _odd_ctx(*pp, conv_w[i], qg, kg)
            cos, sin = _rope_tables()
            kv_w = ATT_KV_HEADS * ATT_DH
            mix_s = _odd_lat(*sp,
                             cache_att_k[:, i].reshape(DEC_BATCH, PAST_LEN, kv_w),
                             cache_att_v[:, i].reshape(DEC_BATCH, PAST_LEN, kv_w),
                             conv_w[i], qg, kg, cos, sin)
            out.setdefault("att_k", []).append(k_new.reshape(BATCH, SEQ, ATT_KV_HEADS, ATT_DH))
            out.setdefault("att_v", []).append(pp[5].reshape(BATCH, SEQ, ATT_KV_HEADS, ATT_DH))
        yp = _outproj(mix_p, yp, mp, w_out, ln_g[l, 0], ln_b[l, 0], SEQ)
        ys = _outproj(mix_s, ys, ms, w_out, ln_g[l, 0], ln_b[l, 0], DEC_SEQ)
        rwt = router_w[l].T
        rb = router_b[l].reshape(N_EXPERTS, 1)
        moe_w = (rwt, rb, exp_w_gate, exp_w_up, exp_w_down, sh_w_gate[l], sh_w_up[l], sh_w_down[l],
                 ln_g[l, 1], ln_b[l, 1])
        yp = _moe(yp, mp, l, *moe_w, SEQ)
        ys = _moe(ys, ms, l, *moe_w, DEC_SEQ)

    return (yp.reshape(BATCH, SEQ, D_MODEL),
            ys.reshape(DEC_BATCH, DEC_SEQ, D_MODEL),
            jnp.stack(out["st"], axis=1),
            jnp.stack(out["na_k"], axis=1),
            jnp.stack(out["na_v"], axis=1),
            jnp.stack(out["att_k"], axis=1),
            jnp.stack(out["att_v"], axis=1))
```

```python
import functools

import jax
import jax.numpy as jnp
from jax import lax
from jax.experimental import pallas as pl
from jax.experimental.pallas import tpu as pltpu

F32 = jnp.float32
BF16 = jnp.bfloat16

D_MODEL = 1024
BATCH = 32
SEQ = 256
DEPTH = 2
DEC_BATCH = 4
DEC_SEQ = 1024
PAST_LEN = 512
GRID_W = 64
GRID_H = DEC_SEQ // GRID_W
MIX_HALF = D_MODEL // 2
RET_HEADS = 4
RET_DV = MIX_HALF // RET_HEADS
RET_DK = RET_DV // 2
NA_HEADS = 4
NA_DH = MIX_HALF // NA_HEADS
NA_WIN_R = 8
NA_WIN_C = 16
CONV_CH = MIX_HALF
ATT_HEADS = 4
ATT_KV_HEADS = 2
ATT_DH = MIX_HALF // ATT_HEADS
ROPE_THETA = 10000.0
N_EXPERTS = 64
EXPERT_HIDDEN = D_MODEL // 8
TOP_K = 8
N_GROUPS = 8
GROUP_SIZE = N_EXPERTS // N_GROUPS
TOPK_GROUPS = 4
ROUTED_SCALE = 2.5
DEEPNORM_ALPHA = (2 * DEPTH) ** 0.25
LN_EPS = 1e-5
RMS_EPS = 1e-6
EVEN_SIZES = (RET_HEADS * RET_DK, RET_HEADS * RET_DK, RET_HEADS * RET_DV, RET_HEADS * RET_DV,
              NA_HEADS * NA_DH, NA_HEADS * NA_DH, NA_HEADS * NA_DH)
ODD_SIZES = (CONV_CH, CONV_CH, CONV_CH, ATT_HEADS * ATT_DH, ATT_KV_HEADS * ATT_DH, ATT_KV_HEADS * ATT_DH)
N_MOD = 6
RPB_ROWS = 2 * NA_WIN_R - 1
RPB_COLS = 2 * NA_WIN_C - 1

LANES = 128
COND_ROWS = 16
V7X_VMEM_LIMIT = 56 * 1024 * 1024

PROJ_TILE = 512
MOE_TILE = 1024
MOE_EPS = 4
ADALN_TN = 1536


def _params(n_axes, vmem_limit=V7X_VMEM_LIMIT):
    return pltpu.CompilerParams(dimension_semantics=("arbitrary",) * n_axes,
                                vmem_limit_bytes=vmem_limit)


def _dot(a, b):
    return jnp.dot(a, b, preferred_element_type=F32)


def _dot_nt(a, b):
    return lax.dot_general(a, b, (((1,), (1,)), ((), ())), preferred_element_type=F32)


def _dot_tn(a, b):
    return lax.dot_general(a, b, (((0,), (0,)), ((), ())), preferred_element_type=F32)


def _split(a):
    hi = a.astype(BF16)
    lo = (a - hi.astype(F32)).astype(BF16)
    return hi, lo


def _silu(x):
    return x / (1.0 + jnp.exp(-x))


def _log_sigmoid(x):
    return jnp.minimum(x, 0.0) - jnp.log(1.0 + jnp.exp(-jnp.abs(x)))


def _layer_norm(z, g, b):
    mu = jnp.mean(z, axis=-1, keepdims=True)
    zc = z - mu
    var = jnp.mean(zc * zc, axis=-1, keepdims=True)
    return zc * lax.rsqrt(var + LN_EPS) * g + b


def _post_norm(x, y, gate, g, b):
    return _layer_norm(DEEPNORM_ALPHA * x + (1.0 + gate) * y, g, b)


def _head_norm(x):
    mu = jnp.mean(x, axis=-1, keepdims=True)
    xc = x - mu
    var = jnp.mean(xc * xc, axis=-1, keepdims=True)
    return xc * lax.rsqrt(var + LN_EPS)


def _rms_norm(x, g):
    return x * lax.rsqrt(jnp.mean(x * x, axis=-1, keepdims=True) + RMS_EPS) * g


def _softmax_pv(parts):
    m = None
    for s, _ in parts:
        mi = jnp.max(s, axis=-1, keepdims=True)
        m = mi if m is None else jnp.maximum(m, mi)
    l = None
    o = None
    for s, v in parts:
        p = jnp.exp(s - m)
        li = jnp.sum(p, axis=-1, keepdims=True)
        oi = _dot(p.astype(BF16), v)
        l = li if l is None else l + li
        o = oi if o is None else o + oi
    return o / l


def _adaln_kernel(c_ref, w_ref, b_ref, o_ref):
    a_hi, a_lo = _split(_silu(c_ref[...]))
    w_hi, w_lo = _split(w_ref[...])
    o_ref[...] = _dot(a_hi, w_hi) + _dot(a_lo, w_hi) + _dot(a_hi, w_lo) + b_ref[...]


def _adaln(cond, w_mod, b_mod):
    n = w_mod.shape[-1]
    return pl.pallas_call(
        _adaln_kernel,
        grid=(DEPTH, n // ADALN_TN),
        in_specs=[pl.BlockSpec((COND_ROWS, D_MODEL), lambda l, j: (0, 0)),
                  pl.BlockSpec((None, D_MODEL, ADALN_TN), lambda l, j: (l, 0, j)),
                  pl.BlockSpec((None, 1, ADALN_TN), lambda l, j: (l, 0, j))],
        out_specs=pl.BlockSpec((None, COND_ROWS, ADALN_TN), lambda l, j: (l, 0, j)),
        out_shape=jax.ShapeDtypeStruct((DEPTH, COND_ROWS, n), F32),
        compiler_params=_params(2),
        name="adaln",
    )(cond, w_mod, b_mod.reshape(DEPTH, 1, n))


def _inproj_kernel(x_ref, mod_ref, w_ref, *out_refs, sizes, head_copies):
    shift = mod_ref[0:1, :]
    scale = mod_ref[1:2, :]
    h = (x_ref[...] * (1.0 + scale) + shift).astype(BF16)
    y = _dot(h, w_ref[...])
    offsets = [sum(sizes[:k]) for k in range(len(sizes))]
    for o_ref, off, n in zip(out_refs, offsets, sizes):
        o_ref[...] = y[:, off:off + n].astype(o_ref.dtype)
    for c_ref, k in zip(out_refs[len(sizes):], head_copies):
        for hd in range(c_ref.shape[1]):
            c_ref[:, hd, :] = y[:, offsets[k] + hd * LANES:offsets[k] + (hd + 1) * LANES]


def _inproj(x, mods, w_bf16, sizes, dtypes, seq_len, head_copies=()):
    t = x.shape[0]
    tile = PROJ_TILE
    mod_idx = (lambda i: (i * tile // seq_len, 0, 0)) if mods.shape[0] > 1 else (lambda i: (0, 0, 0))
    out_specs = [pl.BlockSpec((tile, n), lambda i: (i, 0)) for n in sizes]
    out_shape = [jax.ShapeDtypeStruct((t, n), dt) for n, dt in zip(sizes, dtypes)]
    for k in head_copies:
        out_specs.append(pl.BlockSpec((tile, sizes[k] // LANES, LANES), lambda i: (i, 0, 0)))
        out_shape.append(jax.ShapeDtypeStruct((t, sizes[k] // LANES, LANES), F32))
    outs = pl.pallas_call(
        functools.partial(_inproj_kernel, sizes=sizes, head_copies=head_copies),
        grid=(t // tile,),
        in_specs=[pl.BlockSpec((tile, D_MODEL), lambda i: (i, 0)),
                  pl.BlockSpec((None, N_MOD, D_MODEL), mod_idx),
                  pl.BlockSpec(w_bf16.shape, lambda i: (0, 0))],
        out_specs=out_specs,
        out_shape=out_shape,
        compiler_params=_params(1),
        name="inproj",
    )(x, mods, w_bf16)
    return outs[:len(sizes)], outs[len(sizes):]


def _outproj_kernel(mix_ref, x_ref, mod_ref, w_ref, g_ref, b_ref, o_ref):
    y = _dot(mix_ref[...], w_ref[...])
    o_ref[...] = _post_norm(x_ref[...], y, mod_ref[2:3, :], g_ref[...], b_ref[...])


def _outproj(mix, x, mods, w_bf16, ln_g, ln_b, seq_len):
    t = x.shape[0]
    tile = PROJ_TILE
    mod_idx = (lambda i: (i * tile // seq_len, 0, 0)) if mods.shape[0] > 1 else (lambda i: (0, 0, 0))
    return pl.pallas_call(
        _outproj_kernel,
        grid=(t // tile,),
        in_specs=[pl.BlockSpec((tile, D_MODEL), lambda i: (i, 0)),
                  pl.BlockSpec((tile, D_MODEL), lambda i: (i, 0)),
                  pl.BlockSpec((None, N_MOD, D_MODEL), mod_idx),
                  pl.BlockSpec((D_MODEL, D_MODEL), lambda i: (0, 0)),
                  pl.BlockSpec((1, D_MODEL), lambda i: (0, 0)),
                  pl.BlockSpec((1, D_MODEL), lambda i: (0, 0))],
        out_specs=pl.BlockSpec((tile, D_MODEL), lambda i: (i, 0)),
        out_shape=jax.ShapeDtypeStruct((t, D_MODEL), F32),
        compiler_params=_params(1),
        name="outproj",
    )(mix, x, mods, w_bf16, ln_g.reshape(1, D_MODEL), ln_b.reshape(1, D_MODEL))


def _decay_matrix(length, lgf, lgb, row0=0, rows=None):
    rows = length if rows is None else rows
    ii = lax.broadcasted_iota(jnp.int32, (rows, length), 0) + row0
    jj = lax.broadcasted_iota(jnp.int32, (rows, length), 1)
    rel = (ii - jj).astype(F32)
    fwd = jnp.where(rel >= 0.0, jnp.exp(lgf * jnp.maximum(rel, 0.0)), 0.0)
    bwd = jnp.where(rel <= 0.0, jnp.exp(lgb * jnp.maximum(-rel, 0.0)), 0.0)
    return fwd + bwd


def _retention_head(q, k, v, lgf, lgb, s0f, s0b, dmat=None):
    length = q.shape[0]
    s = _dot_nt(q.astype(BF16), k.astype(BF16))
    if dmat is None:
        dmat = _decay_matrix(length, lgf, lgb)
    if dmat.dtype == BF16:
        o = _dot(s.astype(BF16) * dmat, v)
    else:
        o = _dot((s * dmat).astype(BF16), v)
    if s0f is not None:
        pos = lax.broadcasted_iota(jnp.int32, (length, 1), 0).astype(F32)
        qf = q * jnp.exp(lgf * (pos + 1.0))
        qb = q * jnp.exp(lgb * (length - pos))
        o = o + _dot(qf.astype(BF16), s0f.astype(BF16)) + _dot(qb.astype(BF16), s0b.astype(BF16))
    return o


def _even_ctx_kernel(qr_ref, kr_ref, vr_ref, gr_ref, qn_ref, kn_ref, vn_ref, dl_ref,
                     x_ref, mod_ref, wo_ref, lng_ref, lnb_ref, y_ref, st_ref, mix_ref, dmat_ref, kdec_ref):
    length = qr_ref.shape[0]

    @pl.when(pl.program_id(0) == 0)
    def _():
        lg = _log_sigmoid(dl_ref[...])
        pos = lax.broadcasted_iota(jnp.int32, (length, 1), 0).astype(F32)
        for h in range(RET_HEADS):
            lgf = lg[h:h + 1, 0:1]
            lgb = lg[RET_HEADS + h:RET_HEADS + h + 1, 0:1]
            dmat_ref[h] = _decay_matrix(length, lgf, lgb)
            kdec_ref[h] = jnp.broadcast_to(jnp.exp(lgf * (length - 1.0 - pos)), (length, RET_DK))
            kdec_ref[RET_HEADS + h] = jnp.broadcast_to(jnp.exp(lgb * pos), (length, RET_DK))

    for h in range(RET_HEADS):
        q = qr_ref[:, h * RET_DK:(h + 1) * RET_DK].astype(F32)
        k = kr_ref[:, h * RET_DK:(h + 1) * RET_DK].astype(F32) * (RET_DK ** -0.5)
        v = vr_ref[:, h * RET_DV:(h + 1) * RET_DV]
        o = _retention_head(q, k, v, None, None, None, None, dmat=dmat_ref[h])
        st_ref[h] = _dot_tn((k * kdec_ref[h]).astype(BF16), v)
        st_ref[RET_HEADS + h] = _dot_tn((k * kdec_ref[RET_HEADS + h]).astype(BF16), v)
        g = gr_ref[:, h * RET_DV:(h + 1) * RET_DV]
        mix_ref[:, h * RET_DV:(h + 1) * RET_DV] = (_head_norm(o) * _silu(g)).astype(mix_ref.dtype)
    base = RET_HEADS * RET_DV
    for h in range(NA_HEADS):
        sl = slice(h * NA_DH, (h + 1) * NA_DH)
        s = _dot_nt(qn_ref[:, sl], kn_ref[:, sl]) * (NA_DH ** -0.5)
        mix_ref[:, base + h * NA_DH:base + (h + 1) * NA_DH] = _softmax_pv([(s, vn_ref[:, sl])]).astype(mix_ref.dtype)
    y = _dot(mix_ref[...], wo_ref[...])
    y_ref[...] = _post_norm(x_ref[...], y, mod_ref[2:3, :], lng_ref[...], lnb_ref[...])


def _even_ctx(qr, kr, vr, gr, qn, kn, vn, decay_rows, x, mods, w_out, ln_g, ln_b):
    def seq_spec(n):
        return pl.BlockSpec((SEQ, n), lambda i: (i, 0))

    def full(shape):
        return pl.BlockSpec(shape, lambda i: (0,) * len(shape))
    return pl.pallas_call(
        _even_ctx_kernel,
        grid=(BATCH,),
        in_specs=[seq_spec(n) for n in EVEN_SIZES]
        + [full((2 * RET_HEADS, LANES)), seq_spec(D_MODEL), full((None, N_MOD, D_MODEL)),
           full((D_MODEL, D_MODEL)), full((1, D_MODEL)), full((1, D_MODEL))],
        out_specs=[seq_spec(D_MODEL),
                   pl.BlockSpec((None, 2 * RET_HEADS, RET_DK, RET_DV), lambda i: (i, 0, 0, 0))],
        out_shape=[jax.ShapeDtypeStruct((BATCH * SEQ, D_MODEL), F32),
                   jax.ShapeDtypeStruct((BATCH, 2 * RET_HEADS, RET_DK, RET_DV), F32)],
        scratch_shapes=[pltpu.VMEM((SEQ, D_MODEL), BF16),
                        pltpu.VMEM((RET_HEADS, SEQ, SEQ), F32),
                        pltpu.VMEM((2 * RET_HEADS, SEQ, RET_DK), F32)],
        compiler_params=_params(1),
        name="even_ctx",
    )(qr, kr, vr, gr, qn, kn, vn, decay_rows, x, mods, w_out,
      ln_g.reshape(1, D_MODEL), ln_b.reshape(1, D_MODEL))


def _build_rpb_tiles(rpb_ref, tile_ref):
    qc = lax.broadcasted_iota(jnp.int32, (GRID_W, LANES), 0)
    kc = lax.broadcasted_iota(jnp.int32, (GRID_W, LANES), 1) % GRID_W
    diff = kc - qc + (NA_WIN_C - 1)
    start = jnp.clip(qc - NA_WIN_C // 2, 0, GRID_W - NA_WIN_C)
    win = (kc >= start) & (kc < start + NA_WIN_C)

    def body(idx, carry):
        t = jnp.zeros((GRID_W, LANES), F32)
        for d in range(RPB_COLS):
            t = jnp.where(diff == d, rpb_ref[idx * RPB_COLS + d], t)
        tile_ref[idx] = jnp.where(win, t, -jnp.inf)
        return carry

    lax.fori_loop(0, NA_HEADS * RPB_ROWS, body, 0)


def _build_bias(tile_ref, bias_ref, head):
    left = lax.broadcasted_iota(jnp.int32, (GRID_W, LANES), 1) < GRID_W
    neg = jnp.full((GRID_W, LANES), -jnp.inf, F32)
    rows_w = min(NA_WIN_R, GRID_H)
    for qr in range(GRID_H):
        rs = min(max(qr - rows_w // 2, 0), GRID_H - rows_w)

        def tile(kr):
            if rs <= kr < rs + rows_w:
                return tile_ref[head * RPB_ROWS + kr - qr + NA_WIN_R - 1]
            return neg

        pieces = [jnp.where(left, tile(2 * a), tile(2 * a + 1)) for a in range(GRID_H // 2)]
        bias_ref[qr * GRID_W:(qr + 1) * GRID_W, :] = jnp.concatenate(pieces, axis=1)


def _even_lat_kernel(qr_ref, kr_ref, vr_ref, gr_ref, qn_ref, kn_ref, vn_ref, s0_ref, ck_ref, cv_ref,
                     dl_ref, rpb_ref, mix_ref, tile_ref, bias_ref, dmat_ref):
    length = qr_ref.shape[0]
    lg = _log_sigmoid(dl_ref[...])

    @pl.when(pl.program_id(0) == 0)
    def _():
        _build_rpb_tiles(rpb_ref, tile_ref)
        for h in range(RET_HEADS):
            lgf = lg[h:h + 1, 0:1]
            lgb = lg[RET_HEADS + h:RET_HEADS + h + 1, 0:1]

            def rows_body(r, carry, h=h, lgf=lgf, lgb=lgb):
                r0 = pl.multiple_of(r * LANES, LANES)
                dmat_ref[h, pl.ds(r0, LANES), :] = _decay_matrix(length, lgf, lgb, r0, LANES).astype(BF16)
                return carry

            lax.fori_loop(0, length // LANES, rows_body, 0)

    for h in range(RET_HEADS):
        lgf = lg[h:h + 1, 0:1]
        lgb = lg[RET_HEADS + h:RET_HEADS + h + 1, 0:1]
        q = qr_ref[:, h * RET_DK:(h + 1) * RET_DK].astype(F32)
        k = kr_ref[:, h * RET_DK:(h + 1) * RET_DK].astype(F32) * (RET_DK ** -0.5)
        v = vr_ref[:, h * RET_DV:(h + 1) * RET_DV]
        o = _retention_head(q, k, v, lgf, lgb, s0_ref[h], s0_ref[RET_HEADS + h], dmat=dmat_ref[h])
        g = gr_ref[:, h * RET_DV:(h + 1) * RET_DV]
        mix_ref[:, h * RET_DV:(h + 1) * RET_DV] = (_head_norm(o) * _silu(g)).astype(mix_ref.dtype)
    base = RET_HEADS * RET_DV
    scale = NA_DH ** -0.5
    for h in range(NA_HEADS):
        _build_bias(tile_ref, bias_ref, h)
        sl = slice(h * NA_DH, (h + 1) * NA_DH)
        q = qn_ref[:, sl]
        s_band = _dot_nt(q, kn_ref[:, sl]) * scale + bias_ref[...]
        s_ctx = _dot_nt(q, ck_ref[:, sl].astype(BF16)) * scale
        o = _softmax_pv([(s_band, vn_ref[:, sl]), (s_ctx, cv_ref[:, sl].astype(BF16))])
        mix_ref[:, base + h * NA_DH:base + (h + 1) * NA_DH] = o.astype(mix_ref.dtype)


def _even_lat(qr, kr, vr, gr, qn, kn, vn, state, cache_k, cache_v, decay_rows, rpb_flat):
    def seq_spec(n):
        return pl.BlockSpec((DEC_SEQ, n), lambda i: (i, 0))
    cache_spec = pl.BlockSpec((None, PAST_LEN, NA_HEADS * NA_DH), lambda i: (i, 0, 0))
    return pl.pallas_call(
        _even_lat_kernel,
        grid=(DEC_BATCH,),
        in_specs=[seq_spec(n) for n in EVEN_SIZES]
        + [pl.BlockSpec((None, 2 * RET_HEADS, RET_DK, RET_DV), lambda i: (i, 0, 0, 0)),
           cache_spec, cache_spec,
           pl.BlockSpec((2 * RET_HEADS, LANES), lambda i: (0, 0)),
           pl.BlockSpec(memory_space=pltpu.SMEM)],
        out_specs=seq_spec(D_MODEL),
        out_shape=jax.ShapeDtypeStruct((DEC_BATCH * DEC_SEQ, D_MODEL), BF16),
        scratch_shapes=[pltpu.VMEM((NA_HEADS * RPB_ROWS, GRID_W, LANES), F32),
                        pltpu.VMEM((DEC_SEQ, DEC_SEQ), F32),
                        pltpu.VMEM((RET_HEADS, DEC_SEQ, DEC_SEQ), BF16)],
        compiler_params=_params(1),
        name="even_lat",
    )(qr, kr, vr, gr, qn, kn, vn, state, cache_k, cache_v, decay_rows, rpb_flat)


def _gated_conv(bg_ref, cg_ref, u_ref, w_ref):
    length, ch = u_ref.shape
    z = cg_ref[...] * u_ref[...]
    row = lax.broadcasted_iota(jnp.int32, (length, ch), 0)
    z_prev = jnp.where(row == 0, 0.0, pltpu.roll(z, 1, 0))
    z_next = jnp.where(row == length - 1, 0.0, pltpu.roll(z, length - 1, 0))
    return bg_ref[...] * (z_prev * w_ref[0:1, :] + z * w_ref[1:2, :] + z_next * w_ref[2:3, :])


def _odd_ctx_kernel(bg_ref, cg_ref, u_ref, q_ref, k_ref, v_ref, w_ref, qg_ref, kg_ref,
                    x_ref, mod_ref, wo_ref, lng_ref, lnb_ref, y_ref, ko_ref, mix_ref):
    mix_ref[:, 0:CONV_CH] = _gated_conv(bg_ref, cg_ref, u_ref, w_ref).astype(mix_ref.dtype)
    group = ATT_HEADS // ATT_KV_HEADS
    scale = ATT_DH ** -0.5
    for kv in range(ATT_KV_HEADS):
        sl = slice(kv * ATT_DH, (kv + 1) * ATT_DH)
        k = _rms_norm(k_ref[:, sl], kg_ref[...])
        ko_ref[:, kv, :] = k
        k = k.astype(BF16)
        v = v_ref[:, sl]
        for g in range(group):
            h = kv * group + g
            q = _rms_norm(q_ref[:, h * ATT_DH:(h + 1) * ATT_DH], qg_ref[...]).astype(BF16)
            o = _softmax_pv([(_dot_nt(q, k) * scale, v)])
            mix_ref[:, CONV_CH + h * ATT_DH:CONV_CH + (h + 1) * ATT_DH] = o.astype(mix_ref.dtype)
    y = _dot(mix_ref[...], wo_ref[...])
    y_ref[...] = _post_norm(x_ref[...], y, mod_ref[2:3, :], lng_ref[...], lnb_ref[...])


def _odd_ctx(bg, cg, u, q, k, v, conv_w, qn_g, kn_g, x, mods, w_out, ln_g, ln_b):
    def seq_spec(n):
        return pl.BlockSpec((SEQ, n), lambda i: (i, 0))

    def full(shape):
        return pl.BlockSpec(shape, lambda i: (0,) * len(shape))
    head_spec = pl.BlockSpec((SEQ, ATT_KV_HEADS, ATT_DH), lambda i: (i, 0, 0))
    return pl.pallas_call(
        _odd_ctx_kernel,
        grid=(BATCH,),
        in_specs=[seq_spec(n) for n in ODD_SIZES]
        + [full((3, CONV_CH)), full((1, ATT_DH)), full((1, ATT_DH)),
           seq_spec(D_MODEL), full((None, N_MOD, D_MODEL)), full((D_MODEL, D_MODEL)),
           full((1, D_MODEL)), full((1, D_MODEL))],
        out_specs=[seq_spec(D_MODEL), head_spec],
        out_shape=[jax.ShapeDtypeStruct((BATCH * SEQ, D_MODEL), F32),
                   jax.ShapeDtypeStruct((BATCH * SEQ, ATT_KV_HEADS, ATT_DH), F32)],
        scratch_shapes=[pltpu.VMEM((SEQ, D_MODEL), BF16)],
        compiler_params=_params(1),
        name="odd_ctx",
    )(bg, cg, u, q, k, v, conv_w, qn_g, kn_g, x, mods, w_out,
      ln_g.reshape(1, D_MODEL), ln_b.reshape(1, D_MODEL))


def _rope(x, cos, sin, even_lane):
    swapped = jnp.where(even_lane, pltpu.roll(x, LANES - 1, 1), pltpu.roll(x, 1, 1))
    return x * cos + swapped * sin


def _odd_lat_kernel(bg_ref, cg_ref, u_ref, q_ref, k_ref, v_ref, ck_ref, cv_ref, w_ref, qg_ref, kg_ref,
                    cos_ref, sin_ref, mix_ref):
    mix_ref[:, 0:CONV_CH] = _gated_conv(bg_ref, cg_ref, u_ref, w_ref).astype(mix_ref.dtype)
    group = ATT_HEADS // ATT_KV_HEADS
    scale = ATT_DH ** -0.5
    length = q_ref.shape[0]
    even_lane = lax.broadcasted_iota(jnp.int32, (length, ATT_DH), 1) % 2 == 0
    cos = cos_ref[...]
    sin = sin_ref[...]
    for kv in range(ATT_KV_HEADS):
        sl = slice(kv * ATT_DH, (kv + 1) * ATT_DH)
        k = _rope(_rms_norm(k_ref[:, sl], kg_ref[...]), cos, sin, even_lane).astype(BF16)
        v = v_ref[:, sl]
        kc = ck_ref[:, sl].astype(BF16)
        vc = cv_ref[:, sl].astype(BF16)
        for g in range(group):
            h = kv * group + g
            q = _rms_norm(q_ref[:, h * ATT_DH:(h + 1) * ATT_DH], qg_ref[...])
            q = _rope(q, cos, sin, even_lane).astype(BF16)
            o = _softmax_pv([(_dot_nt(q, k) * scale, v), (_dot_nt(q, kc) * scale, vc)])
            mix_ref[:, CONV_CH + h * ATT_DH:CONV_CH + (h + 1) * ATT_DH] = o.astype(mix_ref.dtype)


def _odd_lat(bg, cg, u, q, k, v, cache_k, cache_v, conv_w, qn_g, kn_g, cos, sin):
    def seq_spec(n):
        return pl.BlockSpec((DEC_SEQ, n), lambda i: (i, 0))

    def full(shape):
        return pl.BlockSpec(shape, lambda i: (0,) * len(shape))
    kv_w = ATT_KV_HEADS * ATT_DH
    cache_spec = pl.BlockSpec((None, PAST_LEN, kv_w), lambda i: (i, 0, 0))
    return pl.pallas_call(
        _odd_lat_kernel,
        grid=(DEC_BATCH,),
        in_specs=[seq_spec(n) for n in ODD_SIZES]
        + [cache_spec, cache_spec, full((3, CONV_CH)), full((1, ATT_DH)), full((1, ATT_DH)),
           full((DEC_SEQ, ATT_DH)), full((DEC_SEQ, ATT_DH))],
        out_specs=seq_spec(D_MODEL),
        out_shape=jax.ShapeDtypeStruct((DEC_BATCH * DEC_SEQ, D_MODEL), BF16),
        compiler_params=_params(1),
        name="odd_lat",
    )(bg, cg, u, q, k, v, cache_k, cache_v, conv_w, qn_g, kn_g, cos, sin)


def _rope_tables():
    t = jnp.arange(DEC_SEQ)
    row = (t // GRID_W).astype(F32)
    col = (t % GRID_W).astype(F32)
    n_freq = ATT_DH // 4
    freqs = ROPE_THETA ** (-jnp.arange(n_freq, dtype=F32) / n_freq)
    ang = jnp.concatenate([row[:, None] * freqs, col[:, None] * freqs], axis=-1)
    cos = jnp.repeat(jnp.cos(ang), 2, axis=-1)
    sin = jnp.stack([-jnp.sin(ang), jnp.sin(ang)], axis=-1).reshape(DEC_SEQ, ATT_DH)
    return cos, sin


def _route(scores, sel):
    n_tok = sel.shape[1]
    neg = -jnp.inf
    sub = lax.broadcasted_iota(jnp.int32, (GROUP_SIZE, n_tok), 0).astype(F32)
    blocks = [sel[g * GROUP_SIZE:(g + 1) * GROUP_SIZE, :] for g in range(N_GROUPS)]
    grp = []
    for blk in blocks:
        m1 = jnp.max(blk, axis=0, keepdims=True)
        i1 = jnp.min(jnp.where(blk == m1, sub, float(GROUP_SIZE)), axis=0, keepdims=True)
        m2 = jnp.max(jnp.where(sub == i1, neg, blk), axis=0, keepdims=True)
        grp.append(m1 + m2)
    masked = []
    for g in range(N_GROUPS):
        ahead = jnp.zeros((1, n_tok), F32)
        for o in range(N_GROUPS):
            if o == g:
                continue
            wins = grp[o] >= grp[g] if o < g else grp[o] > grp[g]
            ahead = ahead + jnp.where(wins, 1.0, 0.0)
        masked.append(jnp.where(ahead < float(TOPK_GROUPS), blocks[g], neg))
    val = jnp.concatenate(masked, axis=0)
    row = lax.broadcasted_iota(jnp.int32, (N_EXPERTS, n_tok), 0).astype(F32)
    w = jnp.zeros((N_EXPERTS, n_tok), F32)
    for _ in range(TOP_K):
        m = jnp.max(val, axis=0, keepdims=True)
        idx = jnp.min(jnp.where(val == m, row, float(N_EXPERTS)), axis=0, keepdims=True)
        pick = row == idx
        w = jnp.where(pick, scores, w)
        val = jnp.where(pick, neg, val)
    return w / jnp.sum(w, axis=0, keepdims=True) * ROUTED_SCALE


def _moe_kernel(y_ref, mod_ref, rwt_ref, rb_ref, wg_ref, wu_ref, wd_ref, sg_ref, su_ref, sd_ref,
                lng_ref, lnb_ref, o_ref, xb_ref, gate_ref, acc_ref, *, n_steps):
    j = pl.program_id(1)
    n_tok = y_ref.shape[0]

    @pl.when(j == 0)
    def _():
        x = y_ref[...] * (1.0 + mod_ref[4:5, :]) + mod_ref[3:4, :]
        x_hi, x_lo = _split(x)
        xb_ref[...] = x_hi
        w_hi, w_lo = _split(rwt_ref[...])
        logits = _dot_nt(w_hi, x_hi) + _dot_nt(w_lo, x_hi) + _dot_nt(w_hi, x_lo)
        scores = 1.0 / (1.0 + jnp.exp(-logits))
        gate_t = _route(scores, scores + rb_ref[...])
        gate_ref[...] = jnp.concatenate([gate_t, jnp.zeros_like(gate_t)], axis=0).T
        sgu = jnp.concatenate([sg_ref[...].astype(BF16), su_ref[...].astype(BF16)], axis=1)
        hs = _dot(x_hi, sgu)
        h = (_silu(hs[:, :EXPERT_HIDDEN]) * hs[:, EXPERT_HIDDEN:]).astype(BF16)
        acc_ref[...] = _dot(h, sd_ref[...].astype(BF16))

    xb = xb_ref[...]
    g_rot = pltpu.roll(gate_ref[...], jnp.bitwise_and(LANES - j * MOE_EPS, LANES - 1), 1)
    acc = acc_ref[...]
    for p in range(MOE_EPS // 2):
        wg2 = jnp.concatenate([wg_ref[2 * p].astype(BF16), wg_ref[2 * p + 1].astype(BF16)], axis=1)
        wu2 = jnp.concatenate([wu_ref[2 * p].astype(BF16), wu_ref[2 * p + 1].astype(BF16)], axis=1)
        hg = _dot(xb, wg2)
        hu = _dot(xb, wu2)
        g2 = jnp.concatenate(
            [jnp.broadcast_to(g_rot[:, 2 * p:2 * p + 1], (n_tok, EXPERT_HIDDEN)),
             jnp.broadcast_to(g_rot[:, 2 * p + 1:2 * p + 2], (n_tok, EXPERT_HIDDEN))], axis=1)
        h = (_silu(hg) * hu * g2).astype(BF16)
        wd2 = wd_ref[2 * p:2 * p + 2].reshape(2 * EXPERT_HIDDEN, D_MODEL).astype(BF16)
        acc = acc + _dot(h, wd2)
    acc_ref[...] = acc

    @pl.when(j == n_steps - 1)
    def _():
        o_ref[...] = _post_norm(y_ref[...], acc_ref[...], mod_ref[5:6, :], lng_ref[...], lnb_ref[...])


def _moe(y, mods, layer, rwt, rb, exp_w_gate, exp_w_up, exp_w_down, sg, su, sd, ln_g, ln_b, seq_len):
    t = y.shape[0]
    tile = MOE_TILE
    n_steps = N_EXPERTS // MOE_EPS
    mod_idx = ((lambda i, j: (i * tile // seq_len, 0, 0)) if mods.shape[0] > 1
               else (lambda i, j: (0, 0, 0)))

    def full(shape):
        return pl.BlockSpec(shape, lambda i, j: (0,) * len(shape))
    return pl.pallas_call(
        functools.partial(_moe_kernel, n_steps=n_steps),
        grid=(t // tile, n_steps),
        in_specs=[pl.BlockSpec((tile, D_MODEL), lambda i, j: (i, 0)),
                  pl.BlockSpec((None, N_MOD, D_MODEL), mod_idx),
                  full((N_EXPERTS, D_MODEL)),
                  full((N_EXPERTS, 1)),
                  pl.BlockSpec((None, MOE_EPS, D_MODEL, EXPERT_HIDDEN), lambda i, j: (layer, j, 0, 0)),
                  pl.BlockSpec((None, MOE_EPS, D_MODEL, EXPERT_HIDDEN), lambda i, j: (layer, j, 0, 0)),
                  pl.BlockSpec((None, MOE_EPS, EXPERT_HIDDEN, D_MODEL), lambda i, j: (layer, j, 0, 0)),
                  full((D_MODEL, EXPERT_HIDDEN)),
                  full((D_MODEL, EXPERT_HIDDEN)),
                  full((EXPERT_HIDDEN, D_MODEL)),
                  full((1, D_MODEL)),
                  full((1, D_MODEL))],
        out_specs=pl.BlockSpec((tile, D_MODEL), lambda i, j: (i, 0)),
        out_shape=jax.ShapeDtypeStruct((t, D_MODEL), F32),
        scratch_shapes=[pltpu.VMEM((tile, D_MODEL), BF16),
                        pltpu.VMEM((tile, LANES), F32),
                        pltpu.VMEM((tile, D_MODEL), F32)],
        compiler_params=_params(2),
        name="moe",
    )(y, mods, rwt, rb, exp_w_gate, exp_w_up, exp_w_down, sg, su, sd,
      ln_g.reshape(1, D_MODEL), ln_b.reshape(1, D_MODEL))


def kernel(x_prompt, x_sample, c, state_ret, cache_na_k, cache_na_v, cache_att_k, cache_att_v, c_ctx,
           w_mod, b_mod, ln_g, ln_b, w_in_even, w_out_even, ret_decay_logit, na_rpb,
           w_in_odd, w_out_odd, conv_w, q_norm_g, k_norm_g, router_w, router_b,
           exp_w_gate, exp_w_up, exp_w_down, sh_w_gate, sh_w_up, sh_w_down):
    yp = x_prompt.reshape(BATCH * SEQ, D_MODEL)
    ys = x_sample.reshape(DEC_BATCH * DEC_SEQ, D_MODEL)

    cond = jnp.zeros((COND_ROWS, D_MODEL), F32).at[0].set(c_ctx).at[1:1 + DEC_BATCH].set(c)
    mods = _adaln(cond, w_mod, b_mod).reshape(DEPTH, COND_ROWS, N_MOD, D_MODEL)

    out = {}
    for l in range(DEPTH):
        i = l // 2
        mp = mods[l, 0:1]
        ms = mods[l, 1:1 + DEC_BATCH]
        if l % 2 == 0:
            w_in = w_in_even[i].astype(BF16)
            w_out = w_out_even[i].astype(BF16)
            decay_rows = jnp.broadcast_to(ret_decay_logit[i].reshape(2 * RET_HEADS, 1), (2 * RET_HEADS, LANES))
            dts = (BF16, BF16, BF16, F32, BF16, BF16, BF16)
            pp, (na_k, na_v) = _inproj(yp, mp, w_in, EVEN_SIZES, dts, SEQ, head_copies=(5, 6))
            sp, _ = _inproj(ys, ms, w_in, EVEN_SIZES, dts, DEC_SEQ)
            yp, st = _even_ctx(*pp, decay_rows, yp, mp, w_out, ln_g[l, 0], ln_b[l, 0])
            mix_s = _even_lat(
                *sp,
                state_ret[:, i].reshape(DEC_BATCH, 2 * RET_HEADS, RET_DK, RET_DV),
                cache_na_k[:, i].reshape(DEC_BATCH, PAST_LEN, NA_HEADS * NA_DH),
                cache_na_v[:, i].reshape(DEC_BATCH, PAST_LEN, NA_HEADS * NA_DH),
                decay_rows, na_rpb[i].reshape(-1))
            out.setdefault("st", []).append(st.reshape(BATCH, 2, RET_HEADS, RET_DK, RET_DV))
            out.setdefault("na_k", []).append(na_k.reshape(BATCH, SEQ, NA_HEADS, NA_DH))
            out.setdefault("na_v", []).append(na_v.reshape(BATCH, SEQ, NA_HEADS, NA_DH))
        else:
            w_in = w_in_odd[i].astype(BF16)
            w_out = w_out_odd[i].astype(BF16)
            dts = (F32, F32, F32, F32, F32, BF16)
            pp, (att_v,) = _inproj(yp, mp, w_in, ODD_SIZES, dts, SEQ, head_copies=(5,))
            sp, _ = _inproj(ys, ms, w_in, ODD_SIZES, dts, DEC_SEQ)
            qg = q_norm_g[i].reshape(1, ATT_DH)
            kg = k_norm_g[i].reshape(1, ATT_DH)
            yp, k_new = _odd_ctx(*pp, conv_w[i], qg, kg, yp, mp, w_out, ln_g[l, 0], ln_b[l, 0])
            cos, sin = _rope_tables()
            kv_w = ATT_KV_HEADS * ATT_DH
            mix_s = _odd_lat(*sp,
                             cache_att_k[:, i].reshape(DEC_BATCH, PAST_LEN, kv_w),
                             cache_att_v[:, i].reshape(DEC_BATCH, PAST_LEN, kv_w),
                             conv_w[i], qg, kg, cos, sin)
            out.setdefault("att_k", []).append(k_new.reshape(BATCH, SEQ, ATT_KV_HEADS, ATT_DH))
            out.setdefault("att_v", []).append(att_v.reshape(BATCH, SEQ, ATT_KV_HEADS, ATT_DH))
        ys = _outproj(mix_s, ys, ms, w_out, ln_g[l, 0], ln_b[l, 0], DEC_SEQ)
        rwt = router_w[l].T
        rb = router_b[l].reshape(N_EXPERTS, 1)
        moe_w = (rwt, rb, exp_w_gate, exp_w_up, exp_w_down, sh_w_gate[l], sh_w_up[l], sh_w_down[l],
                 ln_g[l, 1], ln_b[l, 1])
        yp = _moe(yp, mp, l, *moe_w, SEQ)
        ys = _moe(ys, ms, l, *moe_w, DEC_SEQ)

    return (yp.reshape(BATCH, SEQ, D_MODEL),
            ys.reshape(DEC_BATCH, DEC_SEQ, D_MODEL),
            jnp.stack(out["st"], axis=1),
            jnp.stack(out["na_k"], axis=1),
            jnp.stack(out["na_v"], axis=1),
            jnp.stack(out["att_k"], axis=1),
            jnp.stack(out["att_v"], axis=1))
```

```python
import functools

import jax
import jax.numpy as jnp
from jax import lax
from jax.experimental import pallas as pl
from jax.experimental.pallas import tpu as pltpu

F32 = jnp.float32
BF16 = jnp.bfloat16

D_MODEL = 1024
BATCH = 32
SEQ = 256
DEPTH = 2
DEC_BATCH = 4
DEC_SEQ = 1024
PAST_LEN = 512
GRID_W = 64
GRID_H = DEC_SEQ // GRID_W
MIX_HALF = D_MODEL // 2
RET_HEADS = 4
RET_DV = MIX_HALF // RET_HEADS
RET_DK = RET_DV // 2
NA_HEADS = 4
NA_DH = MIX_HALF // NA_HEADS
NA_WIN_R = 8
NA_WIN_C = 16
CONV_CH = MIX_HALF
ATT_HEADS = 4
ATT_KV_HEADS = 2
ATT_DH = MIX_HALF // ATT_HEADS
ROPE_THETA = 10000.0
N_EXPERTS = 64
EXPERT_HIDDEN = D_MODEL // 8
TOP_K = 8
N_GROUPS = 8
GROUP_SIZE = N_EXPERTS // N_GROUPS
TOPK_GROUPS = 4
ROUTED_SCALE = 2.5
DEEPNORM_ALPHA = (2 * DEPTH) ** 0.25
LN_EPS = 1e-5
RMS_EPS = 1e-6
EVEN_SIZES = (RET_HEADS * RET_DK, RET_HEADS * RET_DK, RET_HEADS * RET_DV, RET_HEADS * RET_DV,
              NA_HEADS * NA_DH, NA_HEADS * NA_DH, NA_HEADS * NA_DH)
ODD_SIZES = (CONV_CH, CONV_CH, CONV_CH, ATT_HEADS * ATT_DH, ATT_KV_HEADS * ATT_DH, ATT_KV_HEADS * ATT_DH)
N_MOD = 6
RPB_ROWS = 2 * NA_WIN_R - 1
RPB_COLS = 2 * NA_WIN_C - 1

LANES = 128
COND_ROWS = 16
V7X_VMEM_LIMIT = 56 * 1024 * 1024

PROJ_TILE = 512
MOE_TILE = 1024
MOE_EPS = 4
ADALN_TN = 1536


def _params(n_axes, vmem_limit=V7X_VMEM_LIMIT):
    return pltpu.CompilerParams(dimension_semantics=("arbitrary",) * n_axes,
                                vmem_limit_bytes=vmem_limit)


def _dot(a, b):
    return jnp.dot(a, b, preferred_element_type=F32)


def _dot_nt(a, b):
    return lax.dot_general(a, b, (((1,), (1,)), ((), ())), preferred_element_type=F32)


def _dot_tn(a, b):
    return lax.dot_general(a, b, (((0,), (0,)), ((), ())), preferred_element_type=F32)


def _split(a):
    hi = a.astype(BF16)
    lo = (a - hi.astype(F32)).astype(BF16)
    return hi, lo


def _silu(x):
    return x / (1.0 + jnp.exp(-x))


def _log_sigmoid(x):
    return jnp.minimum(x, 0.0) - jnp.log(1.0 + jnp.exp(-jnp.abs(x)))


def _layer_norm(z, g, b):
    mu = jnp.mean(z, axis=-1, keepdims=True)
    zc = z - mu
    var = jnp.mean(zc * zc, axis=-1, keepdims=True)
    return zc * lax.rsqrt(var + LN_EPS) * g + b


def _post_norm(x, y, gate, g, b):
    return _layer_norm(DEEPNORM_ALPHA * x + (1.0 + gate) * y, g, b)


def _head_norm(x):
    mu = jnp.mean(x, axis=-1, keepdims=True)
    xc = x - mu
    var = jnp.mean(xc * xc, axis=-1, keepdims=True)
    return xc * lax.rsqrt(var + LN_EPS)


def _rms_norm(x, g):
    return x * lax.rsqrt(jnp.mean(x * x, axis=-1, keepdims=True) + RMS_EPS) * g


def _softmax_pv(parts):
    m = None
    for s, _ in parts:
        mi = jnp.max(s, axis=-1, keepdims=True)
        m = mi if m is None else jnp.maximum(m, mi)
    l = None
    o = None
    for s, v in parts:
        p = jnp.exp(s - m)
        li = jnp.sum(p, axis=-1, keepdims=True)
        oi = _dot(p.astype(BF16), v)
        l = li if l is None else l + li
        o = oi if o is None else o + oi
    return o / l


def _adaln_kernel(c_ref, w_ref, b_ref, o_ref):
    a_hi, a_lo = _split(_silu(c_ref[...]))
    w_hi, w_lo = _split(w_ref[...])
    o_ref[...] = _dot(a_hi, w_hi) + _dot(a_lo, w_hi) + _dot(a_hi, w_lo) + b_ref[...]


def _adaln(cond, w_mod, b_mod):
    n = w_mod.shape[-1]
    return pl.pallas_call(
        _adaln_kernel,
        grid=(DEPTH, n // ADALN_TN),
        in_specs=[pl.BlockSpec((COND_ROWS, D_MODEL), lambda l, j: (0, 0)),
                  pl.BlockSpec((None, D_MODEL, ADALN_TN), lambda l, j: (l, 0, j)),
                  pl.BlockSpec((None, 1, ADALN_TN), lambda l, j: (l, 0, j))],
        out_specs=pl.BlockSpec((None, COND_ROWS, ADALN_TN), lambda l, j: (l, 0, j)),
        out_shape=jax.ShapeDtypeStruct((DEPTH, COND_ROWS, n), F32),
        compiler_params=_params(2),
        name="adaln",
    )(cond, w_mod, b_mod.reshape(DEPTH, 1, n))


def _inproj_kernel(x_ref, mod_ref, w_ref, *out_refs, sizes, head_copies):
    shift = mod_ref[0:1, :]
    scale = mod_ref[1:2, :]
    h = (x_ref[...] * (1.0 + scale) + shift).astype(BF16)
    y = _dot(h, w_ref[...])
    offsets = [sum(sizes[:k]) for k in range(len(sizes))]
    for o_ref, off, n in zip(out_refs, offsets, sizes):
        o_ref[...] = y[:, off:off + n].astype(o_ref.dtype)
    for c_ref, k in zip(out_refs[len(sizes):], head_copies):
        for hd in range(c_ref.shape[1]):
            c_ref[:, hd, :] = y[:, offsets[k] + hd * LANES:offsets[k] + (hd + 1) * LANES]


def _inproj(x, mods, w_bf16, sizes, dtypes, seq_len, head_copies=()):
    t = x.shape[0]
    tile = PROJ_TILE
    mod_idx = (lambda i: (i * tile // seq_len, 0, 0)) if mods.shape[0] > 1 else (lambda i: (0, 0, 0))
    out_specs = [pl.BlockSpec((tile, n), lambda i: (i, 0)) for n in sizes]
    out_shape = [jax.ShapeDtypeStruct((t, n), dt) for n, dt in zip(sizes, dtypes)]
    for k in head_copies:
        out_specs.append(pl.BlockSpec((tile, sizes[k] // LANES, LANES), lambda i: (i, 0, 0)))
        out_shape.append(jax.ShapeDtypeStruct((t, sizes[k] // LANES, LANES), F32))
    outs = pl.pallas_call(
        functools.partial(_inproj_kernel, sizes=sizes, head_copies=head_copies),
        grid=(t // tile,),
        in_specs=[pl.BlockSpec((tile, D_MODEL), lambda i: (i, 0)),
                  pl.BlockSpec((None, N_MOD, D_MODEL), mod_idx),
                  pl.BlockSpec(w_bf16.shape, lambda i: (0, 0))],
        out_specs=out_specs,
        out_shape=out_shape,
        compiler_params=_params(1),
        name="inproj",
    )(x, mods, w_bf16)
    return outs[:len(sizes)], outs[len(sizes):]


def _outproj_kernel(mix_ref, x_ref, mod_ref, w_ref, g_ref, b_ref, o_ref):
    y = _dot(mix_ref[...], w_ref[...])
    o_ref[...] = _post_norm(x_ref[...], y, mod_ref[2:3, :], g_ref[...], b_ref[...])


def _outproj(mix, x, mods, w_bf16, ln_g, ln_b, seq_len):
    t = x.shape[0]
    tile = PROJ_TILE
    mod_idx = (lambda i: (i * tile // seq_len, 0, 0)) if mods.shape[0] > 1 else (lambda i: (0, 0, 0))
    return pl.pallas_call(
        _outproj_kernel,
        grid=(t // tile,),
        in_specs=[pl.BlockSpec((tile, D_MODEL), lambda i: (i, 0)),
                  pl.BlockSpec((tile, D_MODEL), lambda i: (i, 0)),
                  pl.BlockSpec((None, N_MOD, D_MODEL), mod_idx),
                  pl.BlockSpec((D_MODEL, D_MODEL), lambda i: (0, 0)),
                  pl.BlockSpec((1, D_MODEL), lambda i: (0, 0)),
                  pl.BlockSpec((1, D_MODEL), lambda i: (0, 0))],
        out_specs=pl.BlockSpec((tile, D_MODEL), lambda i: (i, 0)),
        out_shape=jax.ShapeDtypeStruct((t, D_MODEL), F32),
        compiler_params=_params(1),
        name="outproj",
    )(mix, x, mods, w_bf16, ln_g.reshape(1, D_MODEL), ln_b.reshape(1, D_MODEL))


def _decay_matrix(length, lgf, lgb, row0=0, rows=None):
    rows = length if rows is None else rows
    ii = lax.broadcasted_iota(jnp.int32, (rows, length), 0) + row0
    jj = lax.broadcasted_iota(jnp.int32, (rows, length), 1)
    rel = (ii - jj).astype(F32)
    fwd = jnp.where(rel >= 0.0, jnp.exp(lgf * jnp.maximum(rel, 0.0)), 0.0)
    bwd = jnp.where(rel <= 0.0, jnp.exp(lgb * jnp.maximum(-rel, 0.0)), 0.0)
    return fwd + bwd


def _retention_head(q, k, v, lgf, lgb, s0f, s0b, dmat=None):
    length = q.shape[0]
    s = _dot_nt(q.astype(BF16), k.astype(BF16))
    if dmat is None:
        dmat = _decay_matrix(length, lgf, lgb)
    if dmat.dtype == BF16:
        o = _dot(s.astype(BF16) * dmat, v)
    else:
        o = _dot((s * dmat).astype(BF16), v)
    if s0f is not None:
        pos = lax.broadcasted_iota(jnp.int32, (length, 1), 0).astype(F32)
        qf = q * jnp.exp(lgf * (pos + 1.0))
        qb = q * jnp.exp(lgb * (length - pos))
        o = o + _dot(qf.astype(BF16), s0f.astype(BF16)) + _dot(qb.astype(BF16), s0b.astype(BF16))
    return o


def _even_ctx_kernel(qr_ref, kr_ref, vr_ref, gr_ref, qn_ref, kn_ref, vn_ref, dl_ref,
                     x_ref, mod_ref, wo_ref, lng_ref, lnb_ref, y_ref, st_ref, mix_ref, dmat_ref, kdec_ref):
    length = qr_ref.shape[0]

    @pl.when(pl.program_id(0) == 0)
    def _():
        lg = _log_sigmoid(dl_ref[...])
        pos = lax.broadcasted_iota(jnp.int32, (length, 1), 0).astype(F32)
        for h in range(RET_HEADS):
            lgf = lg[h:h + 1, 0:1]
            lgb = lg[RET_HEADS + h:RET_HEADS + h + 1, 0:1]
            dmat_ref[h] = _decay_matrix(length, lgf, lgb)
            kdec_ref[h] = jnp.broadcast_to(jnp.exp(lgf * (length - 1.0 - pos)), (length, RET_DK))
            kdec_ref[RET_HEADS + h] = jnp.broadcast_to(jnp.exp(lgb * pos), (length, RET_DK))

    for h in range(RET_HEADS):
        q = qr_ref[:, h * RET_DK:(h + 1) * RET_DK].astype(F32)
        k = kr_ref[:, h * RET_DK:(h + 1) * RET_DK].astype(F32) * (RET_DK ** -0.5)
        v = vr_ref[:, h * RET_DV:(h + 1) * RET_DV]
        o = _retention_head(q, k, v, None, None, None, None, dmat=dmat_ref[h])
        st_ref[h] = _dot_tn((k * kdec_ref[h]).astype(BF16), v)
        st_ref[RET_HEADS + h] = _dot_tn((k * kdec_ref[RET_HEADS + h]).astype(BF16), v)
        g = gr_ref[:, h * RET_DV:(h + 1) * RET_DV]
        mix_ref[:, h * RET_DV:(h + 1) * RET_DV] = (_head_norm(o) * _silu(g)).astype(mix_ref.dtype)
    base = RET_HEADS * RET_DV
    for h in range(NA_HEADS):
        sl = slice(h * NA_DH, (h + 1) * NA_DH)
        s = _dot_nt(qn_ref[:, sl], kn_ref[:, sl]) * (NA_DH ** -0.5)
        mix_ref[:, base + h * NA_DH:base + (h + 1) * NA_DH] = _softmax_pv([(s, vn_ref[:, sl])]).astype(mix_ref.dtype)
    y = _dot(mix_ref[...], wo_ref[...])
    y_ref[...] = _post_norm(x_ref[...], y, mod_ref[2:3, :], lng_ref[...], lnb_ref[...])


def _even_ctx(qr, kr, vr, gr, qn, kn, vn, decay_rows, x, mods, w_out, ln_g, ln_b):
    def seq_spec(n):
        return pl.BlockSpec((SEQ, n), lambda i: (i, 0))

    def full(shape):
        return pl.BlockSpec(shape, lambda i: (0,) * len(shape))
    return pl.pallas_call(
        _even_ctx_kernel,
        grid=(BATCH,),
        in_specs=[seq_spec(n) for n in EVEN_SIZES]
        + [full((2 * RET_HEADS, LANES)), seq_spec(D_MODEL), full((None, N_MOD, D_MODEL)),
           full((D_MODEL, D_MODEL)), full((1, D_MODEL)), full((1, D_MODEL))],
        out_specs=[seq_spec(D_MODEL),
                   pl.BlockSpec((None, 2 * RET_HEADS, RET_DK, RET_DV), lambda i: (i, 0, 0, 0))],
        out_shape=[jax.ShapeDtypeStruct((BATCH * SEQ, D_MODEL), F32),
                   jax.ShapeDtypeStruct((BATCH, 2 * RET_HEADS, RET_DK, RET_DV), F32)],
        scratch_shapes=[pltpu.VMEM((SEQ, D_MODEL), BF16),
                        pltpu.VMEM((RET_HEADS, SEQ, SEQ), F32),
                        pltpu.VMEM((2 * RET_HEADS, SEQ, RET_DK), F32)],
        compiler_params=_params(1),
        name="even_ctx",
    )(qr, kr, vr, gr, qn, kn, vn, decay_rows, x, mods, w_out,
      ln_g.reshape(1, D_MODEL), ln_b.reshape(1, D_MODEL))


def _build_rpb_tiles(rpb_ref, tile_ref):
    qc = lax.broadcasted_iota(jnp.int32, (GRID_W, LANES), 0)
    kc = lax.broadcasted_iota(jnp.int32, (GRID_W, LANES), 1) % GRID_W
    diff = kc - qc + (NA_WIN_C - 1)
    start = jnp.clip(qc - NA_WIN_C // 2, 0, GRID_W - NA_WIN_C)
    win = (kc >= start) & (kc < start + NA_WIN_C)

    def body(idx, carry):
        t = jnp.zeros((GRID_W, LANES), F32)
        for d in range(RPB_COLS):
            t = jnp.where(diff == d, rpb_ref[idx * RPB_COLS + d], t)
        tile_ref[idx] = jnp.where(win, t, -jnp.inf)
        return carry

    lax.fori_loop(0, NA_HEADS * RPB_ROWS, body, 0)


def _build_bias(tile_ref, bias_ref, head):
    left = lax.broadcasted_iota(jnp.int32, (GRID_W, LANES), 1) < GRID_W
    neg = jnp.full((GRID_W, LANES), -jnp.inf, F32)
    rows_w = min(NA_WIN_R, GRID_H)
    for qr in range(GRID_H):
        rs = min(max(qr - rows_w // 2, 0), GRID_H - rows_w)

        def tile(kr):
            if rs <= kr < rs + rows_w:
                return tile_ref[head * RPB_ROWS + kr - qr + NA_WIN_R - 1]
            return neg

        pieces = [jnp.where(left, tile(2 * a), tile(2 * a + 1)) for a in range(GRID_H // 2)]
        bias_ref[qr * GRID_W:(qr + 1) * GRID_W, :] = jnp.concatenate(pieces, axis=1)


def _even_lat_kernel(qr_ref, kr_ref, vr_ref, gr_ref, qn_ref, kn_ref, vn_ref, s0_ref, ck_ref, cv_ref,
                     dl_ref, rpb_ref, mix_ref, tile_ref, bias_ref, dmat_ref):
    length = qr_ref.shape[0]
    lg = _log_sigmoid(dl_ref[...])

    @pl.when(pl.program_id(0) == 0)
    def _():
        _build_rpb_tiles(rpb_ref, tile_ref)
        for h in range(RET_HEADS):
            lgf = lg[h:h + 1, 0:1]
            lgb = lg[RET_HEADS + h:RET_HEADS + h + 1, 0:1]

            def rows_body(r, carry, h=h, lgf=lgf, lgb=lgb):
                r0 = pl.multiple_of(r * LANES, LANES)
                dmat_ref[h, pl.ds(r0, LANES), :] = _decay_matrix(length, lgf, lgb, r0, LANES).astype(BF16)
                return carry

            lax.fori_loop(0, length // LANES, rows_body, 0)

    for h in range(RET_HEADS):
        lgf = lg[h:h + 1, 0:1]
        lgb = lg[RET_HEADS + h:RET_HEADS + h + 1, 0:1]
        q = qr_ref[:, h * RET_DK:(h + 1) * RET_DK].astype(F32)
        k = kr_ref[:, h * RET_DK:(h + 1) * RET_DK].astype(F32) * (RET_DK ** -0.5)
        v = vr_ref[:, h * RET_DV:(h + 1) * RET_DV]
        o = _retention_head(q, k, v, lgf, lgb, s0_ref[h], s0_ref[RET_HEADS + h], dmat=dmat_ref[h])
        g = gr_ref[:, h * RET_DV:(h + 1) * RET_DV]
        mix_ref[:, h * RET_DV:(h + 1) * RET_DV] = (_head_norm(o) * _silu(g)).astype(mix_ref.dtype)
    base = RET_HEADS * RET_DV
    scale = NA_DH ** -0.5
    for h in range(NA_HEADS):
        _build_bias(tile_ref, bias_ref, h)
        sl = slice(h * NA_DH, (h + 1) * NA_DH)
        q = qn_ref[:, sl]
        s_band = _dot_nt(q, kn_ref[:, sl]) * scale + bias_ref[...]
        s_ctx = _dot_nt(q, ck_ref[:, sl].astype(BF16)) * scale
        o = _softmax_pv([(s_band, vn_ref[:, sl]), (s_ctx, cv_ref[:, sl].astype(BF16))])
        mix_ref[:, base + h * NA_DH:base + (h + 1) * NA_DH] = o.astype(mix_ref.dtype)


def _even_lat(qr, kr, vr, gr, qn, kn, vn, state, cache_k, cache_v, decay_rows, rpb_flat):
    def seq_spec(n):
        return pl.BlockSpec((DEC_SEQ, n), lambda i: (i, 0))
    cache_spec = pl.BlockSpec((None, PAST_LEN, NA_HEADS * NA_DH), lambda i: (i, 0, 0))
    return pl.pallas_call(
        _even_lat_kernel,
        grid=(DEC_BATCH,),
        in_specs=[seq_spec(n) for n in EVEN_SIZES]
        + [pl.BlockSpec((None, 2 * RET_HEADS, RET_DK, RET_DV), lambda i: (i, 0, 0, 0)),
           cache_spec, cache_spec,
           pl.BlockSpec((2 * RET_HEADS, LANES), lambda i: (0, 0)),
           pl.BlockSpec(memory_space=pltpu.SMEM)],
        out_specs=seq_spec(D_MODEL),
        out_shape=jax.ShapeDtypeStruct((DEC_BATCH * DEC_SEQ, D_MODEL), BF16),
        scratch_shapes=[pltpu.VMEM((NA_HEADS * RPB_ROWS, GRID_W, LANES), F32),
                        pltpu.VMEM((DEC_SEQ, DEC_SEQ), F32),
                        pltpu.VMEM((RET_HEADS, DEC_SEQ, DEC_SEQ), BF16)],
        compiler_params=_params(1),
        name="even_lat",
    )(qr, kr, vr, gr, qn, kn, vn, state, cache_k, cache_v, decay_rows, rpb_flat)


def _gated_conv(bg_ref, cg_ref, u_ref, w_ref):
    length, ch = u_ref.shape
    z = cg_ref[...] * u_ref[...]
    row = lax.broadcasted_iota(jnp.int32, (length, ch), 0)
    z_prev = jnp.where(row == 0, 0.0, pltpu.roll(z, 1, 0))
    z_next = jnp.where(row == length - 1, 0.0, pltpu.roll(z, length - 1, 0))
    return bg_ref[...] * (z_prev * w_ref[0:1, :] + z * w_ref[1:2, :] + z_next * w_ref[2:3, :])


def _odd_ctx_kernel(bg_ref, cg_ref, u_ref, q_ref, k_ref, v_ref, w_ref, qg_ref, kg_ref,
                    x_ref, mod_ref, wo_ref, lng_ref, lnb_ref, y_ref, ko_ref, mix_ref):
    mix_ref[:, 0:CONV_CH] = _gated_conv(bg_ref, cg_ref, u_ref, w_ref).astype(mix_ref.dtype)
    group = ATT_HEADS // ATT_KV_HEADS
    scale = ATT_DH ** -0.5
    for kv in range(ATT_KV_HEADS):
        sl = slice(kv * ATT_DH, (kv + 1) * ATT_DH)
        k = _rms_norm(k_ref[:, sl], kg_ref[...])
        ko_ref[:, kv, :] = k
        k = k.astype(BF16)
        v = v_ref[:, sl]
        for g in range(group):
            h = kv * group + g
            q = _rms_norm(q_ref[:, h * ATT_DH:(h + 1) * ATT_DH], qg_ref[...]).astype(BF16)
            o = _softmax_pv([(_dot_nt(q, k) * scale, v)])
            mix_ref[:, CONV_CH + h * ATT_DH:CONV_CH + (h + 1) * ATT_DH] = o.astype(mix_ref.dtype)
    y = _dot(mix_ref[...], wo_ref[...])
    y_ref[...] = _post_norm(x_ref[...], y, mod_ref[2:3, :], lng_ref[...], lnb_ref[...])


def _odd_ctx(bg, cg, u, q, k, v, conv_w, qn_g, kn_g, x, mods, w_out, ln_g, ln_b):
    def seq_spec(n):
        return pl.BlockSpec((SEQ, n), lambda i: (i, 0))

    def full(shape):
        return pl.BlockSpec(shape, lambda i: (0,) * len(shape))
    head_spec = pl.BlockSpec((SEQ, ATT_KV_HEADS, ATT_DH), lambda i: (i, 0, 0))
    return pl.pallas_call(
        _odd_ctx_kernel,
        grid=(BATCH,),
        in_specs=[seq_spec(n) for n in ODD_SIZES]
        + [full((3, CONV_CH)), full((1, ATT_DH)), full((1, ATT_DH)),
           seq_spec(D_MODEL), full((None, N_MOD, D_MODEL)), full((D_MODEL, D_MODEL)),
           full((1, D_MODEL)), full((1, D_MODEL))],
        out_specs=[seq_spec(D_MODEL), head_spec],
        out_shape=[jax.ShapeDtypeStruct((BATCH * SEQ, D_MODEL), F32),
                   jax.ShapeDtypeStruct((BATCH * SEQ, ATT_KV_HEADS, ATT_DH), F32)],
        scratch_shapes=[pltpu.VMEM((SEQ, D_MODEL), BF16)],
        compiler_params=_params(1),
        name="odd_ctx",
    )(bg, cg, u, q, k, v, conv_w, qn_g, kn_g, x, mods, w_out,
      ln_g.reshape(1, D_MODEL), ln_b.reshape(1, D_MODEL))


def _rope(x, cos, sin, even_lane):
    swapped = jnp.where(even_lane, pltpu.roll(x, LANES - 1, 1), pltpu.roll(x, 1, 1))
    return x * cos + swapped * sin


def _odd_lat_kernel(bg_ref, cg_ref, u_ref, q_ref, k_ref, v_ref, ck_ref, cv_ref, w_ref, qg_ref, kg_ref,
                    cos_ref, sin_ref, mix_ref):
    mix_ref[:, 0:CONV_CH] = _gated_conv(bg_ref, cg_ref, u_ref, w_ref).astype(mix_ref.dtype)
    group = ATT_HEADS // ATT_KV_HEADS
    scale = ATT_DH ** -0.5
    length = q_ref.shape[0]
    even_lane = lax.broadcasted_iota(jnp.int32, (length, ATT_DH), 1) % 2 == 0
    cos = cos_ref[...]
    sin = sin_ref[...]
    for kv in range(ATT_KV_HEADS):
        sl = slice(kv * ATT_DH, (kv + 1) * ATT_DH)
        k = _rope(_rms_norm(k_ref[:, sl], kg_ref[...]), cos, sin, even_lane).astype(BF16)
        v = v_ref[:, sl]
        kc = ck_ref[:, sl].astype(BF16)
        vc = cv_ref[:, sl].astype(BF16)
        for g in range(group):
            h = kv * group + g
            q = _rms_norm(q_ref[:, h * ATT_DH:(h + 1) * ATT_DH], qg_ref[...])
            q = _rope(q, cos, sin, even_lane).astype(BF16)
            o = _softmax_pv([(_dot_nt(q, k) * scale, v), (_dot_nt(q, kc) * scale, vc)])
            mix_ref[:, CONV_CH + h * ATT_DH:CONV_CH + (h + 1) * ATT_DH] = o.astype(mix_ref.dtype)


def _odd_lat(bg, cg, u, q, k, v, cache_k, cache_v, conv_w, qn_g, kn_g, cos, sin):
    def seq_spec(n):
        return pl.BlockSpec((DEC_SEQ, n), lambda i: (i, 0))

    def full(shape):
        return pl.BlockSpec(shape, lambda i: (0,) * len(shape))
    kv_w = ATT_KV_HEADS * ATT_DH
    cache_spec = pl.BlockSpec((None, PAST_LEN, kv_w), lambda i: (i, 0, 0))
    return pl.pallas_call(
        _odd_lat_kernel,
        grid=(DEC_BATCH,),
        in_specs=[seq_spec(n) for n in ODD_SIZES]
        + [cache_spec, cache_spec, full((3, CONV_CH)), full((1, ATT_DH)), full((1, ATT_DH)),
           full((DEC_SEQ, ATT_DH)), full((DEC_SEQ, ATT_DH))],
        out_specs=seq_spec(D_MODEL),
        out_shape=jax.ShapeDtypeStruct((DEC_BATCH * DEC_SEQ, D_MODEL), BF16),
        compiler_params=_params(1),
        name="odd_lat",
    )(bg, cg, u, q, k, v, cache_k, cache_v, conv_w, qn_g, kn_g, cos, sin)


def _rope_tables():
    t = jnp.arange(DEC_SEQ)
    row = (t // GRID_W).astype(F32)
    col = (t % GRID_W).astype(F32)
    n_freq = ATT_DH // 4
    freqs = ROPE_THETA ** (-jnp.arange(n_freq, dtype=F32) / n_freq)
    ang = jnp.concatenate([row[:, None] * freqs, col[:, None] * freqs], axis=-1)
    cos = jnp.repeat(jnp.cos(ang), 2, axis=-1)
    sin = jnp.stack([-jnp.sin(ang), jnp.sin(ang)], axis=-1).reshape(DEC_SEQ, ATT_DH)
    return cos, sin


def _route(scores, sel):
    n_tok = sel.shape[1]
    neg = -jnp.inf
    sub = lax.broadcasted_iota(jnp.int32, (GROUP_SIZE, n_tok), 0).astype(F32)
    blocks = [sel[g * GROUP_SIZE:(g + 1) * GROUP_SIZE, :] for g in range(N_GROUPS)]
    grp = []
    for blk in blocks:
        m1 = jnp.max(blk, axis=0, keepdims=True)
        i1 = jnp.min(jnp.where(blk == m1, sub, float(GROUP_SIZE)), axis=0, keepdims=True)
        m2 = jnp.max(jnp.where(sub == i1, neg, blk), axis=0, keepdims=True)
        grp.append(m1 + m2)
    masked = []
    for g in range(N_GROUPS):
        ahead = jnp.zeros((1, n_tok), F32)
        for o in range(N_GROUPS):
            if o == g:
                continue
            wins = grp[o] >= grp[g] if o < g else grp[o] > grp[g]
            ahead = ahead + jnp.where(wins, 1.0, 0.0)
        masked.append(jnp.where(ahead < float(TOPK_GROUPS), blocks[g], neg))
    val = jnp.concatenate(masked, axis=0)
    row = lax.broadcasted_iota(jnp.int32, (N_EXPERTS, n_tok), 0).astype(F32)
    w = jnp.zeros((N_EXPERTS, n_tok), F32)
    for _ in range(TOP_K):
        m = jnp.max(val, axis=0, keepdims=True)
        idx = jnp.min(jnp.where(val == m, row, float(N_EXPERTS)), axis=0, keepdims=True)
        pick = row == idx
        w = jnp.where(pick, scores, w)
        val = jnp.where(pick, neg, val)
    return w / jnp.sum(w, axis=0, keepdims=True) * ROUTED_SCALE


def _moe_kernel(y_ref, mod_ref, rwt_ref, rb_ref, wg_hbm, wu_hbm, wd_hbm, sg_ref, su_ref, sd_ref,
                lng_ref, lnb_ref, o_ref, xb_ref, gate_ref, acc_ref, wg_buf, wu_buf, wd_buf, sem,
                *, layer, n_steps):
    n_tok = y_ref.shape[0]

    def weight_copies(step, slot):
        experts = pl.ds(step * MOE_EPS, MOE_EPS)
        return (pltpu.make_async_copy(wg_hbm.at[layer, experts], wg_buf.at[slot], sem.at[0, slot]),
                pltpu.make_async_copy(wu_hbm.at[layer, experts], wu_buf.at[slot], sem.at[1, slot]),
                pltpu.make_async_copy(wd_hbm.at[layer, experts], wd_buf.at[slot], sem.at[2, slot]))

    for cp in weight_copies(0, 0):
        cp.start()

    x = y_ref[...] * (1.0 + mod_ref[4:5, :]) + mod_ref[3:4, :]
    x_hi, x_lo = _split(x)
    xb_ref[...] = x_hi
    w_hi, w_lo = _split(rwt_ref[...])
    logits = _dot_nt(w_hi, x_hi) + _dot_nt(w_lo, x_hi) + _dot_nt(w_hi, x_lo)
    scores = 1.0 / (1.0 + jnp.exp(-logits))
    gate_t = _route(scores, scores + rb_ref[...])
    gate_ref[...] = jnp.concatenate([gate_t, jnp.zeros_like(gate_t)], axis=0).T
    sgu = jnp.concatenate([sg_ref[...].astype(BF16), su_ref[...].astype(BF16)], axis=1)
    hs = _dot(x_hi, sgu)
    h_sh = (_silu(hs[:, :EXPERT_HIDDEN]) * hs[:, EXPERT_HIDDEN:]).astype(BF16)
    acc_ref[...] = _dot(h_sh, sd_ref[...].astype(BF16))

    def expert_group(step, slot):
        xb = xb_ref[...]
        g_rot = pltpu.roll(gate_ref[...], jnp.bitwise_and(LANES - step * MOE_EPS, LANES - 1), 1)
        acc = acc_ref[...]
        for p in range(MOE_EPS // 2):
            wg2 = jnp.concatenate([wg_buf[slot, 2 * p].astype(BF16), wg_buf[slot, 2 * p + 1].astype(BF16)], axis=1)
            wu2 = jnp.concatenate([wu_buf[slot, 2 * p].astype(BF16), wu_buf[slot, 2 * p + 1].astype(BF16)], axis=1)
            hg = _dot(xb, wg2)
            hu = _dot(xb, wu2)
            g2 = jnp.concatenate(
                [jnp.broadcast_to(g_rot[:, 2 * p:2 * p + 1], (n_tok, EXPERT_HIDDEN)),
                 jnp.broadcast_to(g_rot[:, 2 * p + 1:2 * p + 2], (n_tok, EXPERT_HIDDEN))], axis=1)
            h = (_silu(hg) * hu * g2).astype(BF16)
            wd2 = wd_buf[slot, 2 * p:2 * p + 2].reshape(2 * EXPERT_HIDDEN, D_MODEL).astype(BF16)
            acc = acc + _dot(h, wd2)
        acc_ref[...] = acc

    def two_groups(k, carry):
        step = 2 * k
        for cp in weight_copies(step, 0):
            cp.wait()
        for cp in weight_copies(step + 1, 1):
            cp.start()
        expert_group(step, 0)

        for cp in weight_copies(step + 1, 1):
            cp.wait()

        @pl.when(step + 2 < n_steps)
        def _():
            for cp in weight_copies(step + 2, 0):
                cp.start()

        expert_group(step + 1, 1)
        return carry

    lax.fori_loop(0, n_steps // 2, two_groups, 0)
    o_ref[...] = _post_norm(y_ref[...], acc_ref[...], mod_ref[5:6, :], lng_ref[...], lnb_ref[...])


def _moe(y, mods, layer, rwt, rb, exp_w_gate, exp_w_up, exp_w_down, sg, su, sd, ln_g, ln_b, seq_len):
    t = y.shape[0]
    tile = MOE_TILE
    n_steps = N_EXPERTS // MOE_EPS
    assert n_steps % 2 == 0
    mod_idx = (lambda i: (i * tile // seq_len, 0, 0)) if mods.shape[0] > 1 else (lambda i: (0, 0, 0))

    def full(shape):
        return pl.BlockSpec(shape, lambda i: (0,) * len(shape))
    hbm = pl.BlockSpec(memory_space=pl.ANY)
    return pl.pallas_call(
        functools.partial(_moe_kernel, layer=layer, n_steps=n_steps),
        grid=(t // tile,),
        in_specs=[pl.BlockSpec((tile, D_MODEL), lambda i: (i, 0)),
                  pl.BlockSpec((None, N_MOD, D_MODEL), mod_idx),
                  full((N_EXPERTS, D_MODEL)),
                  full((N_EXPERTS, 1)),
                  hbm, hbm, hbm,
                  full((D_MODEL, EXPERT_HIDDEN)),
                  full((D_MODEL, EXPERT_HIDDEN)),
                  full((EXPERT_HIDDEN, D_MODEL)),
                  full((1, D_MODEL)),
                  full((1, D_MODEL))],
        out_specs=pl.BlockSpec((tile, D_MODEL), lambda i: (i, 0)),
        out_shape=jax.ShapeDtypeStruct((t, D_MODEL), F32),
        scratch_shapes=[pltpu.VMEM((tile, D_MODEL), BF16),
                        pltpu.VMEM((tile, LANES), F32),
                        pltpu.VMEM((tile, D_MODEL), F32),
                        pltpu.VMEM((2, MOE_EPS, D_MODEL, EXPERT_HIDDEN), F32),
                        pltpu.VMEM((2, MOE_EPS, D_MODEL, EXPERT_HIDDEN), F32),
                        pltpu.VMEM((2, MOE_EPS, EXPERT_HIDDEN, D_MODEL), F32),
                        pltpu.SemaphoreType.DMA((3, 2))],
        compiler_params=_params(1),
        name="moe",
    )(y, mods, rwt, rb, exp_w_gate, exp_w_up, exp_w_down, sg, su, sd,
      ln_g.reshape(1, D_MODEL), ln_b.reshape(1, D_MODEL))


def kernel(x_prompt, x_sample, c, state_ret, cache_na_k, cache_na_v, cache_att_k, cache_att_v, c_ctx,
           w_mod, b_mod, ln_g, ln_b, w_in_even, w_out_even, ret_decay_logit, na_rpb,
           w_in_odd, w_out_odd, conv_w, q_norm_g, k_norm_g, router_w, router_b,
           exp_w_gate, exp_w_up, exp_w_down, sh_w_gate, sh_w_up, sh_w_down):
    yp = x_prompt.reshape(BATCH * SEQ, D_MODEL)
    ys = x_sample.reshape(DEC_BATCH * DEC_SEQ, D_MODEL)

    cond = jnp.zeros((COND_ROWS, D_MODEL), F32).at[0].set(c_ctx).at[1:1 + DEC_BATCH].set(c)
    mods = _adaln(cond, w_mod, b_mod).reshape(DEPTH, COND_ROWS, N_MOD, D_MODEL)

    out = {}
    for l in range(DEPTH):
        i = l // 2
        mp = mods[l, 0:1]
        ms = mods[l, 1:1 + DEC_BATCH]
        if l % 2 == 0:
            w_in = w_in_even[i].astype(BF16)
            w_out = w_out_even[i].astype(BF16)
            decay_rows = jnp.broadcast_to(ret_decay_logit[i].reshape(2 * RET_HEADS, 1), (2 * RET_HEADS, LANES))
            dts = (BF16, BF16, BF16, F32, BF16, BF16, BF16)
            pp, (na_k, na_v) = _inproj(yp, mp, w_in, EVEN_SIZES, dts, SEQ, head_copies=(5, 6))
            sp, _ = _inproj(ys, ms, w_in, EVEN_SIZES, dts, DEC_SEQ)
            yp, st = _even_ctx(*pp, decay_rows, yp, mp, w_out, ln_g[l, 0], ln_b[l, 0])
            mix_s = _even_lat(
                *sp,
                state_ret[:, i].reshape(DEC_BATCH, 2 * RET_HEADS, RET_DK, RET_DV),
                cache_na_k[:, i].reshape(DEC_BATCH, PAST_LEN, NA_HEADS * NA_DH),
                cache_na_v[:, i].reshape(DEC_BATCH, PAST_LEN, NA_HEADS * NA_DH),
                decay_rows, na_rpb[i].reshape(-1))
            out.setdefault("st", []).append(st.reshape(BATCH, 2, RET_HEADS, RET_DK, RET_DV))
            out.setdefault("na_k", []).append(na_k.reshape(BATCH, SEQ, NA_HEADS, NA_DH))
            out.setdefault("na_v", []).append(na_v.reshape(BATCH, SEQ, NA_HEADS, NA_DH))
        else:
            w_in = w_in_odd[i].astype(BF16)
            w_out = w_out_odd[i].astype(BF16)
            dts = (F32, F32, F32, F32, F32, BF16)
            pp, (att_v,) = _inproj(yp, mp, w_in, ODD_SIZES, dts, SEQ, head_copies=(5,))
            sp, _ = _inproj(ys, ms, w_in, ODD_SIZES, dts, DEC_SEQ)
            qg = q_norm_g[i].reshape(1, ATT_DH)
            kg = k_norm_g[i].reshape(1, ATT_DH)
            yp, k_new = _odd_ctx(*pp, conv_w[i], qg, kg, yp, mp, w_out, ln_g[l, 0], ln_b[l, 0])
            cos, sin = _rope_tables()
            kv_w = ATT_KV_HEADS * ATT_DH
            mix_s = _odd_lat(*sp,
                             cache_att_k[:, i].reshape(DEC_BATCH, PAST_LEN, kv_w),
                             cache_att_v[:, i].reshape(DEC_BATCH, PAST_LEN, kv_w),
                             conv_w[i], qg, kg, cos, sin)
            out.setdefault("att_k", []).append(k_new.reshape(BATCH, SEQ, ATT_KV_HEADS, ATT_DH))
            out.setdefault("att_v", []).append(att_v.reshape(BATCH, SEQ, ATT_KV_HEADS, ATT_DH))
        ys = _outproj(mix_s, ys, ms, w_out, ln_g[l, 0], ln_b[l, 0], DEC_SEQ)
        rwt = router_w[l].T
        rb = router_b[l].reshape(N_EXPERTS, 1)
        moe_w = (rwt, rb, exp_w_gate, exp_w_up, exp_w_down, sh_w_gate[l], sh_w_up[l], sh_w_down[l],
                 ln_g[l, 1], ln_b[l, 1])
        yp = _moe(yp, mp, l, *moe_w, SEQ)
        ys = _moe(ys, ms, l, *moe_w, DEC_SEQ)

    return (yp.reshape(BATCH, SEQ, D_MODEL),
            ys.reshape(DEC_BATCH, DEC_SEQ, D_MODEL),
            jnp.stack(out["st"], axis=1),
            jnp.stack(out["na_k"], axis=1),
            jnp.stack(out["na_v"], axis=1),
            jnp.stack(out["att_k"], axis=1),
            jnp.stack(out["att_v"], axis=1))
```

```python
import functools

import jax
import jax.numpy as jnp
from jax import lax
from jax.experimental import pallas as pl
from jax.experimental.pallas import tpu as pltpu

F32 = jnp.float32
BF16 = jnp.bfloat16

D_MODEL = 1024
BATCH = 32
SEQ = 256
DEPTH = 2
DEC_BATCH = 4
DEC_SEQ = 1024
PAST_LEN = 512
GRID_W = 64
GRID_H = DEC_SEQ // GRID_W
MIX_HALF = D_MODEL // 2
RET_HEADS = 4
RET_DV = MIX_HALF // RET_HEADS
RET_DK = RET_DV // 2
NA_HEADS = 4
NA_DH = MIX_HALF // NA_HEADS
NA_WIN_R = 8
NA_WIN_C = 16
CONV_CH = MIX_HALF
ATT_HEADS = 4
ATT_KV_HEADS = 2
ATT_DH = MIX_HALF // ATT_HEADS
ROPE_THETA = 10000.0
N_EXPERTS = 64
EXPERT_HIDDEN = D_MODEL // 8
TOP_K = 8
N_GROUPS = 8
GROUP_SIZE = N_EXPERTS // N_GROUPS
TOPK_GROUPS = 4
ROUTED_SCALE = 2.5
DEEPNORM_ALPHA = (2 * DEPTH) ** 0.25
LN_EPS = 1e-5
RMS_EPS = 1e-6
EVEN_SIZES = (RET_HEADS * RET_DK, RET_HEADS * RET_DK, RET_HEADS * RET_DV, RET_HEADS * RET_DV,
              NA_HEADS * NA_DH, NA_HEADS * NA_DH, NA_HEADS * NA_DH)
ODD_SIZES = (CONV_CH, CONV_CH, CONV_CH, ATT_HEADS * ATT_DH, ATT_KV_HEADS * ATT_DH, ATT_KV_HEADS * ATT_DH)
N_MOD = 6
RPB_ROWS = 2 * NA_WIN_R - 1
RPB_COLS = 2 * NA_WIN_C - 1

LANES = 128
COND_ROWS = 16
V7X_VMEM_LIMIT = 56 * 1024 * 1024

PROJ_TILE = 512
MOE_TILE = 1024
MOE_EPS = 4
ADALN_TN = 1536


def _params(n_axes, vmem_limit=V7X_VMEM_LIMIT):
    return pltpu.CompilerParams(dimension_semantics=("arbitrary",) * n_axes,
                                vmem_limit_bytes=vmem_limit)


def _dot(a, b):
    return jnp.dot(a, b, preferred_element_type=F32)


def _dot_nt(a, b):
    return lax.dot_general(a, b, (((1,), (1,)), ((), ())), preferred_element_type=F32)


def _dot_tn(a, b):
    return lax.dot_general(a, b, (((0,), (0,)), ((), ())), preferred_element_type=F32)


def _split(a):
    hi = a.astype(BF16)
    lo = (a - hi.astype(F32)).astype(BF16)
    return hi, lo


def _silu(x):
    return x / (1.0 + jnp.exp(-x))


def _log_sigmoid(x):
    return jnp.minimum(x, 0.0) - jnp.log(1.0 + jnp.exp(-jnp.abs(x)))


def _layer_norm(z, g, b):
    mu = jnp.mean(z, axis=-1, keepdims=True)
    zc = z - mu
    var = jnp.mean(zc * zc, axis=-1, keepdims=True)
    return zc * lax.rsqrt(var + LN_EPS) * g + b


def _post_norm(x, y, gate, g, b):
    return _layer_norm(DEEPNORM_ALPHA * x + (1.0 + gate) * y, g, b)


def _head_norm(x):
    mu = jnp.mean(x, axis=-1, keepdims=True)
    xc = x - mu
    var = jnp.mean(xc * xc, axis=-1, keepdims=True)
    return xc * lax.rsqrt(var + LN_EPS)


def _rms_norm(x, g):
    return x * lax.rsqrt(jnp.mean(x * x, axis=-1, keepdims=True) + RMS_EPS) * g


def _softmax_pv(parts):
    m = None
    for s, _ in parts:
        mi = jnp.max(s, axis=-1, keepdims=True)
        m = mi if m is None else jnp.maximum(m, mi)
    l = None
    o = None
    for s, v in parts:
        p = jnp.exp(s - m)
        li = jnp.sum(p, axis=-1, keepdims=True)
        oi = _dot(p.astype(BF16), v)
        l = li if l is None else l + li
        o = oi if o is None else o + oi
    return o / l


def _adaln_kernel(c_ref, w_ref, b_ref, o_ref):
    a_hi, a_lo = _split(_silu(c_ref[...]))
    w_hi, w_lo = _split(w_ref[...])
    o_ref[...] = _dot(a_hi, w_hi) + _dot(a_lo, w_hi) + _dot(a_hi, w_lo) + b_ref[...]


def _adaln(cond, w_mod, b_mod):
    n = w_mod.shape[-1]
    return pl.pallas_call(
        _adaln_kernel,
        grid=(DEPTH, n // ADALN_TN),
        in_specs=[pl.BlockSpec((COND_ROWS, D_MODEL), lambda l, j: (0, 0)),
                  pl.BlockSpec((None, D_MODEL, ADALN_TN), lambda l, j: (l, 0, j)),
                  pl.BlockSpec((None, 1, ADALN_TN), lambda l, j: (l, 0, j))],
        out_specs=pl.BlockSpec((None, COND_ROWS, ADALN_TN), lambda l, j: (l, 0, j)),
        out_shape=jax.ShapeDtypeStruct((DEPTH, COND_ROWS, n), F32),
        compiler_params=_params(2),
        name="adaln",
    )(cond, w_mod, b_mod.reshape(DEPTH, 1, n))


def _inproj_kernel(x_ref, mod_ref, w_ref, *out_refs, sizes):
    shift = mod_ref[0:1, :]
    scale = mod_ref[1:2, :]
    h = (x_ref[...] * (1.0 + scale) + shift).astype(BF16)
    y = _dot(h, w_ref[...])
    for o_ref, off, n in zip(out_refs, _offsets(sizes), sizes):
        o_ref[...] = y[:, off:off + n].astype(o_ref.dtype)


def _inproj(x, mods, w_bf16, sizes, dtypes, seq_len):
    t = x.shape[0]
    tile = PROJ_TILE
    mod_idx = (lambda i: (i * tile // seq_len, 0, 0)) if mods.shape[0] > 1 else (lambda i: (0, 0, 0))
    return pl.pallas_call(
        functools.partial(_inproj_kernel, sizes=sizes),
        grid=(t // tile,),
        in_specs=[pl.BlockSpec((tile, D_MODEL), lambda i: (i, 0)),
                  pl.BlockSpec((None, N_MOD, D_MODEL), mod_idx),
                  pl.BlockSpec(w_bf16.shape, lambda i: (0, 0))],
        out_specs=[pl.BlockSpec((tile, n), lambda i: (i, 0)) for n in sizes],
        out_shape=[jax.ShapeDtypeStruct((t, n), dt) for n, dt in zip(sizes, dtypes)],
        compiler_params=_params(1),
        name="inproj",
    )(x, mods, w_bf16)


def _outproj_kernel(mix_ref, x_ref, mod_ref, w_ref, g_ref, b_ref, o_ref):
    y = _dot(mix_ref[...], w_ref[...])
    o_ref[...] = _post_norm(x_ref[...], y, mod_ref[2:3, :], g_ref[...], b_ref[...])


def _outproj(mix, x, mods, w_bf16, ln_g, ln_b, seq_len):
    t = x.shape[0]
    tile = PROJ_TILE
    mod_idx = (lambda i: (i * tile // seq_len, 0, 0)) if mods.shape[0] > 1 else (lambda i: (0, 0, 0))
    return pl.pallas_call(
        _outproj_kernel,
        grid=(t // tile,),
        in_specs=[pl.BlockSpec((tile, D_MODEL), lambda i: (i, 0)),
                  pl.BlockSpec((tile, D_MODEL), lambda i: (i, 0)),
                  pl.BlockSpec((None, N_MOD, D_MODEL), mod_idx),
                  pl.BlockSpec((D_MODEL, D_MODEL), lambda i: (0, 0)),
                  pl.BlockSpec((1, D_MODEL), lambda i: (0, 0)),
                  pl.BlockSpec((1, D_MODEL), lambda i: (0, 0))],
        out_specs=pl.BlockSpec((tile, D_MODEL), lambda i: (i, 0)),
        out_shape=jax.ShapeDtypeStruct((t, D_MODEL), F32),
        compiler_params=_params(1),
        name="outproj",
    )(mix, x, mods, w_bf16, ln_g.reshape(1, D_MODEL), ln_b.reshape(1, D_MODEL))


def _decay_matrix(length, lgf, lgb, row0=0, rows=None):
    rows = length if rows is None else rows
    ii = lax.broadcasted_iota(jnp.int32, (rows, length), 0) + row0
    jj = lax.broadcasted_iota(jnp.int32, (rows, length), 1)
    rel = (ii - jj).astype(F32)
    fwd = jnp.where(rel >= 0.0, jnp.exp(lgf * jnp.maximum(rel, 0.0)), 0.0)
    bwd = jnp.where(rel <= 0.0, jnp.exp(lgb * jnp.maximum(-rel, 0.0)), 0.0)
    return fwd + bwd


def _retention_head(q, k, v, lgf, lgb, s0f, s0b, dmat=None):
    length = q.shape[0]
    s = _dot_nt(q.astype(BF16), k.astype(BF16))
    if dmat is None:
        dmat = _decay_matrix(length, lgf, lgb)
    if dmat.dtype == BF16:
        o = _dot(s.astype(BF16) * dmat, v)
    else:
        o = _dot((s * dmat).astype(BF16), v)
    if s0f is not None:
        pos = lax.broadcasted_iota(jnp.int32, (length, 1), 0).astype(F32)
        qf = q * jnp.exp(lgf * (pos + 1.0))
        qb = q * jnp.exp(lgb * (length - pos))
        o = o + _dot(qf.astype(BF16), s0f.astype(BF16)) + _dot(qb.astype(BF16), s0b.astype(BF16))
    return o


def _offsets(sizes):
    return [sum(sizes[:k]) for k in range(len(sizes))]


def _project_next(xn_ref, mod_ref, wi_ref, nxt_ref, cur_ref):
    @pl.when(pl.program_id(0) == 0)
    def _():
        cur_ref[...] = jnp.zeros(cur_ref.shape, cur_ref.dtype)

    h = (xn_ref[...] * (1.0 + mod_ref[1:2, :]) + mod_ref[0:1, :]).astype(BF16)
    nxt_ref[...] = _dot(h, wi_ref[...])
    return cur_ref


def _even_ctx_kernel(xn_ref, xc_ref, mod_ref, wi_ref, dl_ref, wo_ref, lng_ref, lnb_ref,
                     y_ref, st_ref, nak_ref, nav_ref, nxt_ref, cur_ref, mix_ref, dmat_ref, kdec_ref):
    length = xc_ref.shape[0]

    @pl.when(pl.program_id(0) == 0)
    def _():
        lg = _log_sigmoid(dl_ref[...])
        pos = lax.broadcasted_iota(jnp.int32, (length, 1), 0).astype(F32)
        for h in range(RET_HEADS):
            lgf = lg[h:h + 1, 0:1]
            lgb = lg[RET_HEADS + h:RET_HEADS + h + 1, 0:1]
            dmat_ref[h] = _decay_matrix(length, lgf, lgb)
            kdec_ref[h] = jnp.broadcast_to(jnp.exp(lgf * (length - 1.0 - pos)), (length, RET_DK))
            kdec_ref[RET_HEADS + h] = jnp.broadcast_to(jnp.exp(lgb * pos), (length, RET_DK))

    cur = _project_next(xn_ref, mod_ref, wi_ref, nxt_ref, cur_ref)
    o_qr, o_kr, o_vr, o_gr, o_qn, o_kn, o_vn = _offsets(EVEN_SIZES)
    eye = jnp.where(lax.broadcasted_iota(jnp.int32, (2 * RET_DK, 2 * RET_DK), 0)
                    == lax.broadcasted_iota(jnp.int32, (2 * RET_DK, 2 * RET_DK), 1), 1.0, 0.0).astype(BF16)
    for h in range(RET_HEADS):
        q = cur[:, o_qr + h * RET_DK:o_qr + (h + 1) * RET_DK]
        k = cur[:, o_kr + h * RET_DK:o_kr + (h + 1) * RET_DK] * (RET_DK ** -0.5)
        v = cur[:, o_vr + h * RET_DV:o_vr + (h + 1) * RET_DV].astype(BF16)
        o = _retention_head(q, k, v, None, None, None, None, dmat=dmat_ref[h])
        k_dec = jnp.concatenate([k * kdec_ref[h], k * kdec_ref[RET_HEADS + h]], axis=1).astype(BF16)
        states = _dot(_dot_nt(eye, k_dec).astype(BF16), v)
        st_ref[h] = states[:RET_DK]
        st_ref[RET_HEADS + h] = states[RET_DK:]
        g = cur[:, o_gr + h * RET_DV:o_gr + (h + 1) * RET_DV]
        mix_ref[:, h * RET_DV:(h + 1) * RET_DV] = (_head_norm(o) * _silu(g)).astype(mix_ref.dtype)
    base = RET_HEADS * RET_DV
    for h in range(NA_HEADS):
        q = cur[:, o_qn + h * NA_DH:o_qn + (h + 1) * NA_DH].astype(BF16)
        k = cur[:, o_kn + h * NA_DH:o_kn + (h + 1) * NA_DH]
        v = cur[:, o_vn + h * NA_DH:o_vn + (h + 1) * NA_DH]
        nak_ref[:, h, :] = k
        nav_ref[:, h, :] = v
        s = _dot_nt(q, k.astype(BF16)) * (NA_DH ** -0.5)
        mix_ref[:, base + h * NA_DH:base + (h + 1) * NA_DH] = _softmax_pv([(s, v.astype(BF16))]).astype(mix_ref.dtype)
    y = _dot(mix_ref[...], wo_ref[...])
    y_ref[...] = _post_norm(xc_ref[...], y, mod_ref[2:3, :], lng_ref[...], lnb_ref[...])
    cur_ref[...] = nxt_ref[...]


def _ctx_layer_specs(n_proj):
    def nxt(n):
        return pl.BlockSpec((SEQ, n), lambda i: (jnp.minimum(i, BATCH - 1), 0))

    def cur(shape):
        return pl.BlockSpec(shape, lambda i: (jnp.maximum(i - 1, 0),) + (0,) * (len(shape) - 1))

    def full(shape):
        return pl.BlockSpec(shape, lambda i: (0,) * len(shape))

    def const(shape):
        return pl.BlockSpec(shape, lambda i: (0,) * len(shape), pipeline_mode=pl.Buffered(1))
    head = [nxt(D_MODEL), cur((SEQ, D_MODEL)), full((None, N_MOD, D_MODEL)), const((D_MODEL, n_proj))]
    tail = [const((D_MODEL, D_MODEL)), full((1, D_MODEL)), full((1, D_MODEL))]
    return head, tail, cur, full


def _even_ctx(x, mods, w_in, decay_rows, w_out, ln_g, ln_b):
    head, tail, cur, full = _ctx_layer_specs(sum(EVEN_SIZES))
    return pl.pallas_call(
        _even_ctx_kernel,
        grid=(BATCH + 1,),
        in_specs=head + [full((2 * RET_HEADS, LANES))] + tail,
        out_specs=[cur((SEQ, D_MODEL)),
                   cur((None, 2 * RET_HEADS, RET_DK, RET_DV)),
                   cur((SEQ, NA_HEADS, NA_DH)), cur((SEQ, NA_HEADS, NA_DH))],
        out_shape=[jax.ShapeDtypeStruct((BATCH * SEQ, D_MODEL), F32),
                   jax.ShapeDtypeStruct((BATCH, 2 * RET_HEADS, RET_DK, RET_DV), F32),
                   jax.ShapeDtypeStruct((BATCH * SEQ, NA_HEADS, NA_DH), F32),
                   jax.ShapeDtypeStruct((BATCH * SEQ, NA_HEADS, NA_DH), F32)],
        scratch_shapes=[pltpu.VMEM((SEQ, sum(EVEN_SIZES)), F32),
                        pltpu.VMEM((SEQ, sum(EVEN_SIZES)), F32),
                        pltpu.VMEM((SEQ, D_MODEL), BF16),
                        pltpu.VMEM((RET_HEADS, SEQ, SEQ), F32),
                        pltpu.VMEM((2 * RET_HEADS, SEQ, RET_DK), F32)],
        compiler_params=_params(1),
        name="even_ctx",
    )(x, x, mods, w_in, decay_rows, w_out, ln_g.reshape(1, D_MODEL), ln_b.reshape(1, D_MODEL))


def _build_rpb_tiles(rpb_ref, tile_ref):
    qc = lax.broadcasted_iota(jnp.int32, (GRID_W, LANES), 0)
    kc = lax.broadcasted_iota(jnp.int32, (GRID_W, LANES), 1) % GRID_W
    diff = kc - qc + (NA_WIN_C - 1)
    start = jnp.clip(qc - NA_WIN_C // 2, 0, GRID_W - NA_WIN_C)
    win = (kc >= start) & (kc < start + NA_WIN_C)

    def body(idx, carry):
        t = jnp.zeros((GRID_W, LANES), F32)
        for d in range(RPB_COLS):
            t = jnp.where(diff == d, rpb_ref[idx * RPB_COLS + d], t)
        tile_ref[idx] = jnp.where(win, t, -jnp.inf)
        return carry

    lax.fori_loop(0, NA_HEADS * RPB_ROWS, body, 0)


def _build_bias(tile_ref, bias_ref, head):
    left = lax.broadcasted_iota(jnp.int32, (GRID_W, LANES), 1) < GRID_W
    neg = jnp.full((GRID_W, LANES), -jnp.inf, F32)
    rows_w = min(NA_WIN_R, GRID_H)
    for qr in range(GRID_H):
        rs = min(max(qr - rows_w // 2, 0), GRID_H - rows_w)

        def tile(kr):
            if rs <= kr < rs + rows_w:
                return tile_ref[head * RPB_ROWS + kr - qr + NA_WIN_R - 1]
            return neg

        pieces = [jnp.where(left, tile(2 * a), tile(2 * a + 1)) for a in range(GRID_H // 2)]
        bias_ref[qr * GRID_W:(qr + 1) * GRID_W, :] = jnp.concatenate(pieces, axis=1)


def _even_lat_kernel(qr_ref, kr_ref, vr_ref, gr_ref, qn_ref, kn_ref, vn_ref, s0_ref, ck_ref, cv_ref,
                     dl_ref, rpb_ref, mix_ref, tile_ref, bias_ref, dmat_ref):
    length = qr_ref.shape[0]
    lg = _log_sigmoid(dl_ref[...])

    @pl.when(pl.program_id(0) == 0)
    def _():
        _build_rpb_tiles(rpb_ref, tile_ref)
        for h in range(RET_HEADS):
            lgf = lg[h:h + 1, 0:1]
            lgb = lg[RET_HEADS + h:RET_HEADS + h + 1, 0:1]

            def rows_body(r, carry, h=h, lgf=lgf, lgb=lgb):
                r0 = pl.multiple_of(r * LANES, LANES)
                dmat_ref[h, pl.ds(r0, LANES), :] = _decay_matrix(length, lgf, lgb, r0, LANES).astype(BF16)
                return carry

            lax.fori_loop(0, length // LANES, rows_body, 0)

    for h in range(RET_HEADS):
        lgf = lg[h:h + 1, 0:1]
        lgb = lg[RET_HEADS + h:RET_HEADS + h + 1, 0:1]
        q = qr_ref[:, h * RET_DK:(h + 1) * RET_DK].astype(F32)
        k = kr_ref[:, h * RET_DK:(h + 1) * RET_DK].astype(F32) * (RET_DK ** -0.5)
        v = vr_ref[:, h * RET_DV:(h + 1) * RET_DV]
        o = _retention_head(q, k, v, lgf, lgb, s0_ref[h], s0_ref[RET_HEADS + h], dmat=dmat_ref[h])
        g = gr_ref[:, h * RET_DV:(h + 1) * RET_DV]
        mix_ref[:, h * RET_DV:(h + 1) * RET_DV] = (_head_norm(o) * _silu(g)).astype(mix_ref.dtype)
    base = RET_HEADS * RET_DV
    scale = NA_DH ** -0.5
    for h in range(NA_HEADS):
        _build_bias(tile_ref, bias_ref, h)
        sl = slice(h * NA_DH, (h + 1) * NA_DH)
        q = qn_ref[:, sl]
        s_band = _dot_nt(q, kn_ref[:, sl]) * scale + bias_ref[...]
        s_ctx = _dot_nt(q, ck_ref[:, sl].astype(BF16)) * scale
        o = _softmax_pv([(s_band, vn_ref[:, sl]), (s_ctx, cv_ref[:, sl].astype(BF16))])
        mix_ref[:, base + h * NA_DH:base + (h + 1) * NA_DH] = o.astype(mix_ref.dtype)


def _even_lat(qr, kr, vr, gr, qn, kn, vn, state, cache_k, cache_v, decay_rows, rpb_flat):
    def seq_spec(n):
        return pl.BlockSpec((DEC_SEQ, n), lambda i: (i, 0))
    cache_spec = pl.BlockSpec((None, PAST_LEN, NA_HEADS * NA_DH), lambda i: (i, 0, 0))
    return pl.pallas_call(
        _even_lat_kernel,
        grid=(DEC_BATCH,),
        in_specs=[seq_spec(n) for n in EVEN_SIZES]
        + [pl.BlockSpec((None, 2 * RET_HEADS, RET_DK, RET_DV), lambda i: (i, 0, 0, 0)),
           cache_spec, cache_spec,
           pl.BlockSpec((2 * RET_HEADS, LANES), lambda i: (0, 0)),
           pl.BlockSpec(memory_space=pltpu.SMEM)],
        out_specs=seq_spec(D_MODEL),
        out_shape=jax.ShapeDtypeStruct((DEC_BATCH * DEC_SEQ, D_MODEL), BF16),
        scratch_shapes=[pltpu.VMEM((NA_HEADS * RPB_ROWS, GRID_W, LANES), F32),
                        pltpu.VMEM((DEC_SEQ, DEC_SEQ), F32),
                        pltpu.VMEM((RET_HEADS, DEC_SEQ, DEC_SEQ), BF16)],
        compiler_params=_params(1),
        name="even_lat",
    )(qr, kr, vr, gr, qn, kn, vn, state, cache_k, cache_v, decay_rows, rpb_flat)


def _gated_conv(bg, cg, u, w_ref):
    length, ch = u.shape
    z = cg * u
    row = lax.broadcasted_iota(jnp.int32, (length, ch), 0)
    z_prev = jnp.where(row == 0, 0.0, pltpu.roll(z, 1, 0))
    z_next = jnp.where(row == length - 1, 0.0, pltpu.roll(z, length - 1, 0))
    return bg * (z_prev * w_ref[0:1, :] + z * w_ref[1:2, :] + z_next * w_ref[2:3, :])


def _odd_ctx_kernel(xn_ref, xc_ref, mod_ref, wi_ref, w_ref, qg_ref, kg_ref, wo_ref, lng_ref, lnb_ref,
                    y_ref, ko_ref, vo_ref, nxt_ref, cur_ref, mix_ref):
    cur = _project_next(xn_ref, mod_ref, wi_ref, nxt_ref, cur_ref)
    o_bg, o_cg, o_u, o_q, o_k, o_v = _offsets(ODD_SIZES)
    conv = _gated_conv(cur[:, o_bg:o_bg + CONV_CH], cur[:, o_cg:o_cg + CONV_CH], cur[:, o_u:o_u + CONV_CH], w_ref)
    mix_ref[:, 0:CONV_CH] = conv.astype(mix_ref.dtype)
    group = ATT_HEADS // ATT_KV_HEADS
    scale = ATT_DH ** -0.5
    for kv in range(ATT_KV_HEADS):
        k = _rms_norm(cur[:, o_k + kv * ATT_DH:o_k + (kv + 1) * ATT_DH], kg_ref[...])
        v = cur[:, o_v + kv * ATT_DH:o_v + (kv + 1) * ATT_DH]
        ko_ref[:, kv, :] = k
        vo_ref[:, kv, :] = v
        k = k.astype(BF16)
        v = v.astype(BF16)
        for g in range(group):
            h = kv * group + g
            q = _rms_norm(cur[:, o_q + h * ATT_DH:o_q + (h + 1) * ATT_DH], qg_ref[...]).astype(BF16)
            o = _softmax_pv([(_dot_nt(q, k) * scale, v)])
            mix_ref[:, CONV_CH + h * ATT_DH:CONV_CH + (h + 1) * ATT_DH] = o.astype(mix_ref.dtype)
    y = _dot(mix_ref[...], wo_ref[...])
    y_ref[...] = _post_norm(xc_ref[...], y, mod_ref[2:3, :], lng_ref[...], lnb_ref[...])
    cur_ref[...] = nxt_ref[...]


def _odd_ctx(x, mods, w_in, conv_w, qn_g, kn_g, w_out, ln_g, ln_b):
    head, tail, cur, full = _ctx_layer_specs(sum(ODD_SIZES))
    kv_shape = (SEQ, ATT_KV_HEADS, ATT_DH)
    return pl.pallas_call(
        _odd_ctx_kernel,
        grid=(BATCH + 1,),
        in_specs=head + [full((3, CONV_CH)), full((1, ATT_DH)), full((1, ATT_DH))] + tail,
        out_specs=[cur((SEQ, D_MODEL)), cur(kv_shape), cur(kv_shape)],
        out_shape=[jax.ShapeDtypeStruct((BATCH * SEQ, D_MODEL), F32),
                   jax.ShapeDtypeStruct((BATCH * SEQ, ATT_KV_HEADS, ATT_DH), F32),
                   jax.ShapeDtypeStruct((BATCH * SEQ, ATT_KV_HEADS, ATT_DH), F32)],
        scratch_shapes=[pltpu.VMEM((SEQ, sum(ODD_SIZES)), F32),
                        pltpu.VMEM((SEQ, sum(ODD_SIZES)), F32),
                        pltpu.VMEM((SEQ, D_MODEL), BF16)],
        compiler_params=_params(1),
        name="odd_ctx",
    )(x, x, mods, w_in, conv_w, qn_g, kn_g, w_out, ln_g.reshape(1, D_MODEL), ln_b.reshape(1, D_MODEL))


def _rope(x, cos, sin, even_lane):
    swapped = jnp.where(even_lane, pltpu.roll(x, LANES - 1, 1), pltpu.roll(x, 1, 1))
    return x * cos + swapped * sin


def _odd_lat_kernel(bg_ref, cg_ref, u_ref, q_ref, k_ref, v_ref, ck_ref, cv_ref, w_ref, qg_ref, kg_ref,
                    cos_ref, sin_ref, mix_ref):
    mix_ref[:, 0:CONV_CH] = _gated_conv(bg_ref[...], cg_ref[...], u_ref[...], w_ref).astype(mix_ref.dtype)
    group = ATT_HEADS // ATT_KV_HEADS
    scale = ATT_DH ** -0.5
    length = q_ref.shape[0]
    even_lane = lax.broadcasted_iota(jnp.int32, (length, ATT_DH), 1) % 2 == 0
    cos = cos_ref[...]
    sin = sin_ref[...]
    for kv in range(ATT_KV_HEADS):
        sl = slice(kv * ATT_DH, (kv + 1) * ATT_DH)
        k = _rope(_rms_norm(k_ref[:, sl], kg_ref[...]), cos, sin, even_lane).astype(BF16)
        v = v_ref[:, sl]
        kc = ck_ref[:, sl].astype(BF16)
        vc = cv_ref[:, sl].astype(BF16)
        for g in range(group):
            h = kv * group + g
            q = _rms_norm(q_ref[:, h * ATT_DH:(h + 1) * ATT_DH], qg_ref[...])
            q = _rope(q, cos, sin, even_lane).astype(BF16)
            o = _softmax_pv([(_dot_nt(q, k) * scale, v), (_dot_nt(q, kc) * scale, vc)])
            mix_ref[:, CONV_CH + h * ATT_DH:CONV_CH + (h + 1) * ATT_DH] = o.astype(mix_ref.dtype)


def _odd_lat(bg, cg, u, q, k, v, cache_k, cache_v, conv_w, qn_g, kn_g, cos, sin):
    def seq_spec(n):
        return pl.BlockSpec((DEC_SEQ, n), lambda i: (i, 0))

    def full(shape):
        return pl.BlockSpec(shape, lambda i: (0,) * len(shape))
    kv_w = ATT_KV_HEADS * ATT_DH
    cache_spec = pl.BlockSpec((None, PAST_LEN, kv_w), lambda i: (i, 0, 0))
    return pl.pallas_call(
        _odd_lat_kernel,
        grid=(DEC_BATCH,),
        in_specs=[seq_spec(n) for n in ODD_SIZES]
        + [cache_spec, cache_spec, full((3, CONV_CH)), full((1, ATT_DH)), full((1, ATT_DH)),
           full((DEC_SEQ, ATT_DH)), full((DEC_SEQ, ATT_DH))],
        out_specs=seq_spec(D_MODEL),
        out_shape=jax.ShapeDtypeStruct((DEC_BATCH * DEC_SEQ, D_MODEL), BF16),
        compiler_params=_params(1),
        name="odd_lat",
    )(bg, cg, u, q, k, v, cache_k, cache_v, conv_w, qn_g, kn_g, cos, sin)


def _rope_tables():
    t = jnp.arange(DEC_SEQ)
    row = (t // GRID_W).astype(F32)
    col = (t % GRID_W).astype(F32)
    n_freq = ATT_DH // 4
    freqs = ROPE_THETA ** (-jnp.arange(n_freq, dtype=F32) / n_freq)
    ang = jnp.concatenate([row[:, None] * freqs, col[:, None] * freqs], axis=-1)
    cos = jnp.repeat(jnp.cos(ang), 2, axis=-1)
    sin = jnp.stack([-jnp.sin(ang), jnp.sin(ang)], axis=-1).reshape(DEC_SEQ, ATT_DH)
    return cos, sin


def _route(scores, sel):
    n_tok = sel.shape[1]
    neg = -jnp.inf
    sub = lax.broadcasted_iota(jnp.int32, (GROUP_SIZE, n_tok), 0).astype(F32)
    blocks = [sel[g * GROUP_SIZE:(g + 1) * GROUP_SIZE, :] for g in range(N_GROUPS)]
    grp = []
    for blk in blocks:
        m1 = jnp.max(blk, axis=0, keepdims=True)
        i1 = jnp.min(jnp.where(blk == m1, sub, float(GROUP_SIZE)), axis=0, keepdims=True)
        m2 = jnp.max(jnp.where(sub == i1, neg, blk), axis=0, keepdims=True)
        grp.append(m1 + m2)
    masked = []
    for g in range(N_GROUPS):
        ahead = jnp.zeros((1, n_tok), F32)
        for o in range(N_GROUPS):
            if o == g:
                continue
            wins = grp[o] >= grp[g] if o < g else grp[o] > grp[g]
            ahead = ahead + jnp.where(wins, 1.0, 0.0)
        masked.append(jnp.where(ahead < float(TOPK_GROUPS), blocks[g], neg))
    val = jnp.concatenate(masked, axis=0)
    row = lax.broadcasted_iota(jnp.int32, (N_EXPERTS, n_tok), 0).astype(F32)
    w = jnp.zeros((N_EXPERTS, n_tok), F32)
    for _ in range(TOP_K):
        m = jnp.max(val, axis=0, keepdims=True)
        idx = jnp.min(jnp.where(val == m, row, float(N_EXPERTS)), axis=0, keepdims=True)
        pick = row == idx
        w = jnp.where(pick, scores, w)
        val = jnp.where(pick, neg, val)
    return w / jnp.sum(w, axis=0, keepdims=True) * ROUTED_SCALE


def _moe_kernel(y_ref, mod_ref, rwt_ref, rb_ref, wg_hbm, wu_hbm, wd_hbm, sg_ref, su_ref, sd_ref,
                lng_ref, lnb_ref, o_ref, xb_ref, gate_ref, acc_ref, wg_buf, wu_buf, wd_buf, sem,
                *, layer, n_steps):
    n_tok = y_ref.shape[0]

    def weight_copies(step, slot):
        experts = pl.ds(step * MOE_EPS, MOE_EPS)
        return (pltpu.make_async_copy(wg_hbm.at[layer, experts], wg_buf.at[slot], sem.at[0, slot]),
                pltpu.make_async_copy(wu_hbm.at[layer, experts], wu_buf.at[slot], sem.at[1, slot]),
                pltpu.make_async_copy(wd_hbm.at[layer, experts], wd_buf.at[slot], sem.at[2, slot]))

    for cp in weight_copies(0, 0):
        cp.start()

    x = y_ref[...] * (1.0 + mod_ref[4:5, :]) + mod_ref[3:4, :]
    x_hi, x_lo = _split(x)
    xb_ref[...] = x_hi
    w_hi, w_lo = _split(rwt_ref[...])
    logits = _dot_nt(w_hi, x_hi) + _dot_nt(w_lo, x_hi) + _dot_nt(w_hi, x_lo)
    scores = 1.0 / (1.0 + jnp.exp(-logits))
    gate_t = _route(scores, scores + rb_ref[...])
    gate_ref[...] = jnp.concatenate([gate_t, jnp.zeros_like(gate_t)], axis=0).T
    sgu = jnp.concatenate([sg_ref[...].astype(BF16), su_ref[...].astype(BF16)], axis=1)
    hs = _dot(x_hi, sgu)
    h_sh = (_silu(hs[:, :EXPERT_HIDDEN]) * hs[:, EXPERT_HIDDEN:]).astype(BF16)
    acc_ref[...] = _dot(h_sh, sd_ref[...].astype(BF16))

    def expert_group(step, slot):
        xb = xb_ref[...]
        g_rot = pltpu.roll(gate_ref[...], jnp.bitwise_and(LANES - step * MOE_EPS, LANES - 1), 1)
        acc = acc_ref[...]
        for p in range(MOE_EPS // 2):
            wg2 = jnp.concatenate([wg_buf[slot, 2 * p].astype(BF16), wg_buf[slot, 2 * p + 1].astype(BF16)], axis=1)
            wu2 = jnp.concatenate([wu_buf[slot, 2 * p].astype(BF16), wu_buf[slot, 2 * p + 1].astype(BF16)], axis=1)
            hg = _dot(xb, wg2)
            hu = _dot(xb, wu2)
            g2 = jnp.concatenate(
                [jnp.broadcast_to(g_rot[:, 2 * p:2 * p + 1], (n_tok, EXPERT_HIDDEN)),
                 jnp.broadcast_to(g_rot[:, 2 * p + 1:2 * p + 2], (n_tok, EXPERT_HIDDEN))], axis=1)
            h = (_silu(hg) * hu * g2).astype(BF16)
            wd2 = wd_buf[slot, 2 * p:2 * p + 2].reshape(2 * EXPERT_HIDDEN, D_MODEL).astype(BF16)
            acc = acc + _dot(h, wd2)
        acc_ref[...] = acc

    def two_groups(k, carry):
        step = 2 * k
        for cp in weight_copies(step, 0):
            cp.wait()
        for cp in weight_copies(step + 1, 1):
            cp.start()
        expert_group(step, 0)

        for cp in weight_copies(step + 1, 1):
            cp.wait()

        @pl.when(step + 2 < n_steps)
        def _():
            for cp in weight_copies(step + 2, 0):
                cp.start()

        expert_group(step + 1, 1)
        return carry

    lax.fori_loop(0, n_steps // 2, two_groups, 0)
    o_ref[...] = _post_norm(y_ref[...], acc_ref[...], mod_ref[5:6, :], lng_ref[...], lnb_ref[...])


def _moe(y, mods, layer, rwt, rb, exp_w_gate, exp_w_up, exp_w_down, sg, su, sd, ln_g, ln_b, seq_len):
    t = y.shape[0]
    tile = MOE_TILE
    n_steps = N_EXPERTS // MOE_EPS
    assert n_steps % 2 == 0
    mod_idx = (lambda i: (i * tile // seq_len, 0, 0)) if mods.shape[0] > 1 else (lambda i: (0, 0, 0))

    def full(shape):
        return pl.BlockSpec(shape, lambda i: (0,) * len(shape))
    hbm = pl.BlockSpec(memory_space=pl.ANY)
    return pl.pallas_call(
        functools.partial(_moe_kernel, layer=layer, n_steps=n_steps),
        grid=(t // tile,),
        in_specs=[pl.BlockSpec((tile, D_MODEL), lambda i: (i, 0)),
                  pl.BlockSpec((None, N_MOD, D_MODEL), mod_idx),
                  full((N_EXPERTS, D_MODEL)),
                  full((N_EXPERTS, 1)),
                  hbm, hbm, hbm,
                  full((D_MODEL, EXPERT_HIDDEN)),
                  full((D_MODEL, EXPERT_HIDDEN)),
                  full((EXPERT_HIDDEN, D_MODEL)),
                  full((1, D_MODEL)),
                  full((1, D_MODEL))],
        out_specs=pl.BlockSpec((tile, D_MODEL), lambda i: (i, 0)),
        out_shape=jax.ShapeDtypeStruct((t, D_MODEL), F32),
        scratch_shapes=[pltpu.VMEM((tile, D_MODEL), BF16),
                        pltpu.VMEM((tile, LANES), F32),
                        pltpu.VMEM((tile, D_MODEL), F32),
                        pltpu.VMEM((2, MOE_EPS, D_MODEL, EXPERT_HIDDEN), F32),
                        pltpu.VMEM((2, MOE_EPS, D_MODEL, EXPERT_HIDDEN), F32),
                        pltpu.VMEM((2, MOE_EPS, EXPERT_HIDDEN, D_MODEL), F32),
                        pltpu.SemaphoreType.DMA((3, 2))],
        compiler_params=_params(1),
        name="moe",
    )(y, mods, rwt, rb, exp_w_gate, exp_w_up, exp_w_down, sg, su, sd,
      ln_g.reshape(1, D_MODEL), ln_b.reshape(1, D_MODEL))


def kernel(x_prompt, x_sample, c, state_ret, cache_na_k, cache_na_v, cache_att_k, cache_att_v, c_ctx,
           w_mod, b_mod, ln_g, ln_b, w_in_even, w_out_even, ret_decay_logit, na_rpb,
           w_in_odd, w_out_odd, conv_w, q_norm_g, k_norm_g, router_w, router_b,
           exp_w_gate, exp_w_up, exp_w_down, sh_w_gate, sh_w_up, sh_w_down):
    yp = x_prompt.reshape(BATCH * SEQ, D_MODEL)
    ys = x_sample.reshape(DEC_BATCH * DEC_SEQ, D_MODEL)

    cond = jnp.zeros((COND_ROWS, D_MODEL), F32).at[0].set(c_ctx).at[1:1 + DEC_BATCH].set(c)
    mods = _adaln(cond, w_mod, b_mod).reshape(DEPTH, COND_ROWS, N_MOD, D_MODEL)

    out = {}
    for l in range(DEPTH):
        i = l // 2
        mp = mods[l, 0:1]
        ms = mods[l, 1:1 + DEC_BATCH]
        if l % 2 == 0:
            w_in = w_in_even[i].astype(BF16)
            w_out = w_out_even[i].astype(BF16)
            decay_rows = jnp.broadcast_to(ret_decay_logit[i].reshape(2 * RET_HEADS, 1), (2 * RET_HEADS, LANES))
            dts = (BF16, BF16, BF16, F32, BF16, BF16, BF16)
            sp = _inproj(ys, ms, w_in, EVEN_SIZES, dts, DEC_SEQ)
            yp, st, na_k, na_v = _even_ctx(yp, mp, w_in, decay_rows, w_out, ln_g[l, 0], ln_b[l, 0])
            mix_s = _even_lat(
                *sp,
                state_ret[:, i].reshape(DEC_BATCH, 2 * RET_HEADS, RET_DK, RET_DV),
                cache_na_k[:, i].reshape(DEC_BATCH, PAST_LEN, NA_HEADS * NA_DH),
                cache_na_v[:, i].reshape(DEC_BATCH, PAST_LEN, NA_HEADS * NA_DH),
                decay_rows, na_rpb[i].reshape(-1))
            out.setdefault("st", []).append(st.reshape(BATCH, 2, RET_HEADS, RET_DK, RET_DV))
            out.setdefault("na_k", []).append(na_k.reshape(BATCH, SEQ, NA_HEADS, NA_DH))
            out.setdefault("na_v", []).append(na_v.reshape(BATCH, SEQ, NA_HEADS, NA_DH))
        else:
            w_in = w_in_odd[i].astype(BF16)
            w_out = w_out_odd[i].astype(BF16)
            dts = (F32, F32, F32, F32, F32, BF16)
            sp = _inproj(ys, ms, w_in, ODD_SIZES, dts, DEC_SEQ)
            qg = q_norm_g[i].reshape(1, ATT_DH)
            kg = k_norm_g[i].reshape(1, ATT_DH)
            yp, k_new, att_v = _odd_ctx(yp, mp, w_in, conv_w[i], qg, kg, w_out, ln_g[l, 0], ln_b[l, 0])
            cos, sin = _rope_tables()
            kv_w = ATT_KV_HEADS * ATT_DH
            mix_s = _odd_lat(*sp,
                             cache_att_k[:, i].reshape(DEC_BATCH, PAST_LEN, kv_w),
                             cache_att_v[:, i].reshape(DEC_BATCH, PAST_LEN, kv_w),
                             conv_w[i], qg, kg, cos, sin)
            out.setdefault("att_k", []).append(k_new.reshape(BATCH, SEQ, ATT_KV_HEADS, ATT_DH))
            out.setdefault("att_v", []).append(att_v.reshape(BATCH, SEQ, ATT_KV_HEADS, ATT_DH))
        ys = _outproj(mix_s, ys, ms, w_out, ln_g[l, 0], ln_b[l, 0], DEC_SEQ)
        rwt = router_w[l].T
        rb = router_b[l].reshape(N_EXPERTS, 1)
        moe_w = (rwt, rb, exp_w_gate, exp_w_up, exp_w_down, sh_w_gate[l], sh_w_up[l], sh_w_down[l],
                 ln_g[l, 1], ln_b[l, 1])
        yp = _moe(yp, mp, l, *moe_w, SEQ)
        ys = _moe(ys, ms, l, *moe_w, DEC_SEQ)

    return (yp.reshape(BATCH, SEQ, D_MODEL),
            ys.reshape(DEC_BATCH, DEC_SEQ, D_MODEL),
            jnp.stack(out["st"], axis=1),
            jnp.stack(out["na_k"], axis=1),
            jnp.stack(out["na_v"], axis=1),
            jnp.stack(out["att_k"], axis=1),
            jnp.stack(out["att_v"], axis=1))
```

```python
import functools

import jax
import jax.numpy as jnp
from jax import lax
from jax.experimental import pallas as pl
from jax.experimental.pallas import tpu as pltpu

F32 = jnp.float32
BF16 = jnp.bfloat16

D_MODEL = 1024
BATCH = 32
SEQ = 256
DEPTH = 2
DEC_BATCH = 4
DEC_SEQ = 1024
PAST_LEN = 512
GRID_W = 64
GRID_H = DEC_SEQ // GRID_W
MIX_HALF = D_MODEL // 2
RET_HEADS = 4
RET_DV = MIX_HALF // RET_HEADS
RET_DK = RET_DV // 2
NA_HEADS = 4
NA_DH = MIX_HALF // NA_HEADS
NA_WIN_R = 8
NA_WIN_C = 16
CONV_CH = MIX_HALF
ATT_HEADS = 4
ATT_KV_HEADS = 2
ATT_DH = MIX_HALF // ATT_HEADS
ROPE_THETA = 10000.0
N_EXPERTS = 64
EXPERT_HIDDEN = D_MODEL // 8
TOP_K = 8
N_GROUPS = 8
GROUP_SIZE = N_EXPERTS // N_GROUPS
TOPK_GROUPS = 4
ROUTED_SCALE = 2.5
DEEPNORM_ALPHA = (2 * DEPTH) ** 0.25
LN_EPS = 1e-5
RMS_EPS = 1e-6
EVEN_SIZES = (RET_HEADS * RET_DK, RET_HEADS * RET_DK, RET_HEADS * RET_DV, RET_HEADS * RET_DV,
              NA_HEADS * NA_DH, NA_HEADS * NA_DH, NA_HEADS * NA_DH)
ODD_SIZES = (CONV_CH, CONV_CH, CONV_CH, ATT_HEADS * ATT_DH, ATT_KV_HEADS * ATT_DH, ATT_KV_HEADS * ATT_DH)
N_MOD = 6
RPB_ROWS = 2 * NA_WIN_R - 1
RPB_COLS = 2 * NA_WIN_C - 1

LANES = 128
COND_ROWS = 16
V7X_VMEM_LIMIT = 56 * 1024 * 1024

PROJ_TILE = 512
LAT_QBLK = 256
MOE_TILE = 1024
MOE_EPS = 4
ADALN_TN = 1536


def _params(n_axes, vmem_limit=V7X_VMEM_LIMIT):
    return pltpu.CompilerParams(dimension_semantics=("arbitrary",) * n_axes,
                                vmem_limit_bytes=vmem_limit)


def _dot(a, b):
    return jnp.dot(a, b, preferred_element_type=F32)


def _dot_nt(a, b):
    return lax.dot_general(a, b, (((1,), (1,)), ((), ())), preferred_element_type=F32)


def _split(a):
    hi = a.astype(BF16)
    lo = (a - hi.astype(F32)).astype(BF16)
    return hi, lo


def _silu(x):
    return x / (1.0 + jnp.exp(-x))


def _log_sigmoid(x):
    return jnp.minimum(x, 0.0) - jnp.log(1.0 + jnp.exp(-jnp.abs(x)))


def _layer_norm(z, g, b):
    mu = jnp.mean(z, axis=-1, keepdims=True)
    zc = z - mu
    var = jnp.mean(zc * zc, axis=-1, keepdims=True)
    return zc * lax.rsqrt(var + LN_EPS) * g + b


def _post_norm(x, y, gate, g, b):
    return _layer_norm(DEEPNORM_ALPHA * x + (1.0 + gate) * y, g, b)


def _head_norm(x):
    mu = jnp.mean(x, axis=-1, keepdims=True)
    xc = x - mu
    var = jnp.mean(xc * xc, axis=-1, keepdims=True)
    return xc * lax.rsqrt(var + LN_EPS)


def _rms_norm(x, g):
    return x * lax.rsqrt(jnp.mean(x * x, axis=-1, keepdims=True) + RMS_EPS) * g


def _softmax_pv(parts):
    m = None
    for s, _ in parts:
        mi = jnp.max(s, axis=-1, keepdims=True)
        m = mi if m is None else jnp.maximum(m, mi)
    l = None
    o = None
    for s, v in parts:
        p = jnp.exp(s - m)
        li = jnp.sum(p, axis=-1, keepdims=True)
        oi = _dot(p.astype(BF16), v)
        l = li if l is None else l + li
        o = oi if o is None else o + oi
    return o / l


def _adaln_kernel(c_ref, w_ref, b_ref, o_ref):
    a_hi, a_lo = _split(_silu(c_ref[...]))
    w_hi, w_lo = _split(w_ref[...])
    o_ref[...] = _dot(a_hi, w_hi) + _dot(a_lo, w_hi) + _dot(a_hi, w_lo) + b_ref[...]


def _adaln(cond, w_mod, b_mod):
    n = w_mod.shape[-1]
    return pl.pallas_call(
        _adaln_kernel,
        grid=(DEPTH, n // ADALN_TN),
        in_specs=[pl.BlockSpec((COND_ROWS, D_MODEL), lambda l, j: (0, 0)),
                  pl.BlockSpec((None, D_MODEL, ADALN_TN), lambda l, j: (l, 0, j)),
                  pl.BlockSpec((None, 1, ADALN_TN), lambda l, j: (l, 0, j))],
        out_specs=pl.BlockSpec((None, COND_ROWS, ADALN_TN), lambda l, j: (l, 0, j)),
        out_shape=jax.ShapeDtypeStruct((DEPTH, COND_ROWS, n), F32),
        compiler_params=_params(2),
        name="adaln",
    )(cond, w_mod, b_mod.reshape(DEPTH, 1, n))


def _inproj_kernel(x_ref, mod_ref, w_ref, *out_refs, sizes):
    shift = mod_ref[0:1, :]
    scale = mod_ref[1:2, :]
    h = (x_ref[...] * (1.0 + scale) + shift).astype(BF16)
    y = _dot(h, w_ref[...])
    for o_ref, off, n in zip(out_refs, _offsets(sizes), sizes):
        o_ref[...] = y[:, off:off + n].astype(o_ref.dtype)


def _inproj(x, mods, w_bf16, sizes, dtypes, seq_len):
    t = x.shape[0]
    tile = PROJ_TILE
    mod_idx = (lambda i: (i * tile // seq_len, 0, 0)) if mods.shape[0] > 1 else (lambda i: (0, 0, 0))
    return pl.pallas_call(
        functools.partial(_inproj_kernel, sizes=sizes),
        grid=(t // tile,),
        in_specs=[pl.BlockSpec((tile, D_MODEL), lambda i: (i, 0)),
                  pl.BlockSpec((None, N_MOD, D_MODEL), mod_idx),
                  pl.BlockSpec(w_bf16.shape, lambda i: (0, 0))],
        out_specs=[pl.BlockSpec((tile, n), lambda i: (i, 0)) for n in sizes],
        out_shape=[jax.ShapeDtypeStruct((t, n), dt) for n, dt in zip(sizes, dtypes)],
        compiler_params=_params(1),
        name="inproj",
    )(x, mods, w_bf16)


def _outproj_kernel(mix_ref, x_ref, mod_ref, w_ref, g_ref, b_ref, o_ref):
    y = _dot(mix_ref[...], w_ref[...])
    o_ref[...] = _post_norm(x_ref[...], y, mod_ref[2:3, :], g_ref[...], b_ref[...])


def _outproj(mix, x, mods, w_bf16, ln_g, ln_b, seq_len):
    t = x.shape[0]
    tile = PROJ_TILE
    mod_idx = (lambda i: (i * tile // seq_len, 0, 0)) if mods.shape[0] > 1 else (lambda i: (0, 0, 0))
    return pl.pallas_call(
        _outproj_kernel,
        grid=(t // tile,),
        in_specs=[pl.BlockSpec((tile, D_MODEL), lambda i: (i, 0)),
                  pl.BlockSpec((tile, D_MODEL), lambda i: (i, 0)),
                  pl.BlockSpec((None, N_MOD, D_MODEL), mod_idx),
                  pl.BlockSpec((D_MODEL, D_MODEL), lambda i: (0, 0)),
                  pl.BlockSpec((1, D_MODEL), lambda i: (0, 0)),
                  pl.BlockSpec((1, D_MODEL), lambda i: (0, 0))],
        out_specs=pl.BlockSpec((tile, D_MODEL), lambda i: (i, 0)),
        out_shape=jax.ShapeDtypeStruct((t, D_MODEL), F32),
        compiler_params=_params(1),
        name="outproj",
    )(mix, x, mods, w_bf16, ln_g.reshape(1, D_MODEL), ln_b.reshape(1, D_MODEL))


def _decay_matrix(length, lgf, lgb, row0=0, rows=None):
    rows = length if rows is None else rows
    ii = lax.broadcasted_iota(jnp.int32, (rows, length), 0) + row0
    jj = lax.broadcasted_iota(jnp.int32, (rows, length), 1)
    rel = (ii - jj).astype(F32)
    fwd = jnp.where(rel >= 0.0, jnp.exp(lgf * jnp.maximum(rel, 0.0)), 0.0)
    bwd = jnp.where(rel <= 0.0, jnp.exp(lgb * jnp.maximum(-rel, 0.0)), 0.0)
    return fwd + bwd


def _retention_head(q, k, v, lgf, lgb, s0f, s0b, dmat=None):
    length = q.shape[0]
    s = _dot_nt(q.astype(BF16), k.astype(BF16))
    if dmat is None:
        dmat = _decay_matrix(length, lgf, lgb)
    if dmat.dtype == BF16:
        o = _dot(s.astype(BF16) * dmat, v)
    else:
        o = _dot((s * dmat).astype(BF16), v)
    if s0f is not None:
        pos = lax.broadcasted_iota(jnp.int32, (length, 1), 0).astype(F32)
        qf = q * jnp.exp(lgf * (pos + 1.0))
        qb = q * jnp.exp(lgb * (length - pos))
        o = o + _dot(qf.astype(BF16), s0f.astype(BF16)) + _dot(qb.astype(BF16), s0b.astype(BF16))
    return o


def _offsets(sizes):
    return [sum(sizes[:k]) for k in range(len(sizes))]


def _project_next(xn_ref, mod_ref, wi_ref, nxt_ref, cur_ref):
    @pl.when(pl.program_id(0) == 0)
    def _():
        cur_ref[...] = jnp.zeros(cur_ref.shape, cur_ref.dtype)

    h = (xn_ref[...] * (1.0 + mod_ref[1:2, :]) + mod_ref[0:1, :]).astype(BF16)
    nxt_ref[...] = _dot(h, wi_ref[...])
    return cur_ref


def _even_ctx_kernel(xn_ref, xc_ref, mod_ref, wi_ref, dl_ref, wo_ref, lng_ref, lnb_ref,
                     y_ref, st_ref, nak_ref, nav_ref, nxt_ref, cur_ref, mix_ref, dmat_ref, kdec_ref):
    length = xc_ref.shape[0]

    @pl.when(pl.program_id(0) == 0)
    def _():
        lg = _log_sigmoid(dl_ref[...])
        pos = lax.broadcasted_iota(jnp.int32, (length, 1), 0).astype(F32)
        for h in range(RET_HEADS):
            lgf = lg[h:h + 1, 0:1]
            lgb = lg[RET_HEADS + h:RET_HEADS + h + 1, 0:1]
            dmat_ref[h] = _decay_matrix(length, lgf, lgb)
            kdec_ref[h] = jnp.broadcast_to(jnp.exp(lgf * (length - 1.0 - pos)), (length, RET_DK))
            kdec_ref[RET_HEADS + h] = jnp.broadcast_to(jnp.exp(lgb * pos), (length, RET_DK))

    cur = _project_next(xn_ref, mod_ref, wi_ref, nxt_ref, cur_ref)
    o_qr, o_kr, o_vr, o_gr, o_qn, o_kn, o_vn = _offsets(EVEN_SIZES)
    eye = jnp.where(lax.broadcasted_iota(jnp.int32, (2 * RET_DK, 2 * RET_DK), 0)
                    == lax.broadcasted_iota(jnp.int32, (2 * RET_DK, 2 * RET_DK), 1), 1.0, 0.0).astype(BF16)
    for h in range(RET_HEADS):
        q = cur[:, o_qr + h * RET_DK:o_qr + (h + 1) * RET_DK]
        k = cur[:, o_kr + h * RET_DK:o_kr + (h + 1) * RET_DK] * (RET_DK ** -0.5)
        v = cur[:, o_vr + h * RET_DV:o_vr + (h + 1) * RET_DV].astype(BF16)
        o = _retention_head(q, k, v, None, None, None, None, dmat=dmat_ref[h])
        k_dec = jnp.concatenate([k * kdec_ref[h], k * kdec_ref[RET_HEADS + h]], axis=1).astype(BF16)
        states = _dot(_dot_nt(eye, k_dec).astype(BF16), v)
        st_ref[h] = states[:RET_DK]
        st_ref[RET_HEADS + h] = states[RET_DK:]
        g = cur[:, o_gr + h * RET_DV:o_gr + (h + 1) * RET_DV]
        mix_ref[:, h * RET_DV:(h + 1) * RET_DV] = (_head_norm(o) * _silu(g)).astype(mix_ref.dtype)
    base = RET_HEADS * RET_DV
    for h in range(NA_HEADS):
        q = cur[:, o_qn + h * NA_DH:o_qn + (h + 1) * NA_DH].astype(BF16)
        k = cur[:, o_kn + h * NA_DH:o_kn + (h + 1) * NA_DH]
        v = cur[:, o_vn + h * NA_DH:o_vn + (h + 1) * NA_DH]
        nak_ref[:, h, :] = k
        nav_ref[:, h, :] = v
        s = _dot_nt(q, k.astype(BF16)) * (NA_DH ** -0.5)
        mix_ref[:, base + h * NA_DH:base + (h + 1) * NA_DH] = _softmax_pv([(s, v.astype(BF16))]).astype(mix_ref.dtype)
    y = _dot(mix_ref[...], wo_ref[...])
    y_ref[...] = _post_norm(xc_ref[...], y, mod_ref[2:3, :], lng_ref[...], lnb_ref[...])
    cur_ref[...] = nxt_ref[...]


def _ctx_layer_specs(n_proj):
    def nxt(n):
        return pl.BlockSpec((SEQ, n), lambda i: (jnp.minimum(i, BATCH - 1), 0))

    def cur(shape):
        return pl.BlockSpec(shape, lambda i: (jnp.maximum(i - 1, 0),) + (0,) * (len(shape) - 1))

    def full(shape):
        return pl.BlockSpec(shape, lambda i: (0,) * len(shape))

    def const(shape):
        return pl.BlockSpec(shape, lambda i: (0,) * len(shape), pipeline_mode=pl.Buffered(1))
    head = [nxt(D_MODEL), cur((SEQ, D_MODEL)), full((None, N_MOD, D_MODEL)), const((D_MODEL, n_proj))]
    tail = [const((D_MODEL, D_MODEL)), full((1, D_MODEL)), full((1, D_MODEL))]
    return head, tail, cur, full


def _even_ctx(x, mods, w_in, decay_rows, w_out, ln_g, ln_b):
    head, tail, cur, full = _ctx_layer_specs(sum(EVEN_SIZES))
    return pl.pallas_call(
        _even_ctx_kernel,
        grid=(BATCH + 1,),
        in_specs=head + [full((2 * RET_HEADS, LANES))] + tail,
        out_specs=[cur((SEQ, D_MODEL)),
                   cur((None, 2 * RET_HEADS, RET_DK, RET_DV)),
                   cur((SEQ, NA_HEADS, NA_DH)), cur((SEQ, NA_HEADS, NA_DH))],
        out_shape=[jax.ShapeDtypeStruct((BATCH * SEQ, D_MODEL), F32),
                   jax.ShapeDtypeStruct((BATCH, 2 * RET_HEADS, RET_DK, RET_DV), F32),
                   jax.ShapeDtypeStruct((BATCH * SEQ, NA_HEADS, NA_DH), F32),
                   jax.ShapeDtypeStruct((BATCH * SEQ, NA_HEADS, NA_DH), F32)],
        scratch_shapes=[pltpu.VMEM((SEQ, sum(EVEN_SIZES)), F32),
                        pltpu.VMEM((SEQ, sum(EVEN_SIZES)), F32),
                        pltpu.VMEM((SEQ, D_MODEL), BF16),
                        pltpu.VMEM((RET_HEADS, SEQ, SEQ), F32),
                        pltpu.VMEM((2 * RET_HEADS, SEQ, RET_DK), F32)],
        compiler_params=_params(1),
        name="even_ctx",
    )(x, x, mods, w_in, decay_rows, w_out, ln_g.reshape(1, D_MODEL), ln_b.reshape(1, D_MODEL))


def _build_rpb_tiles(rpb_ref, tile_ref):
    qc = lax.broadcasted_iota(jnp.int32, (GRID_W, LANES), 0)
    kc = lax.broadcasted_iota(jnp.int32, (GRID_W, LANES), 1) % GRID_W
    diff = kc - qc + (NA_WIN_C - 1)
    start = jnp.clip(qc - NA_WIN_C // 2, 0, GRID_W - NA_WIN_C)
    win = (kc >= start) & (kc < start + NA_WIN_C)

    def body(idx, carry):
        t = jnp.zeros((GRID_W, LANES), F32)
        for d in range(RPB_COLS):
            t = jnp.where(diff == d, rpb_ref[idx * RPB_COLS + d], t)
        tile_ref[idx] = jnp.where(win, t, -jnp.inf)
        return carry

    lax.fori_loop(0, NA_HEADS * RPB_ROWS, body, 0)


def _build_bias(tile_ref, bias_ref, head):
    left = lax.broadcasted_iota(jnp.int32, (GRID_W, LANES), 1) < GRID_W
    neg = jnp.full((GRID_W, LANES), -jnp.inf, F32)
    rows_w = min(NA_WIN_R, GRID_H)
    for qr in range(GRID_H):
        rs = min(max(qr - rows_w // 2, 0), GRID_H - rows_w)

        def tile(kr):
            if rs <= kr < rs + rows_w:
                return tile_ref[head * RPB_ROWS + kr - qr + NA_WIN_R - 1]
            return neg

        pieces = [jnp.where(left, tile(2 * a), tile(2 * a + 1)) for a in range(GRID_H // 2)]
        bias_ref[qr * GRID_W:(qr + 1) * GRID_W, :] = jnp.concatenate(pieces, axis=1)


def _even_lat_kernel(qr_ref, kr_ref, vr_ref, gr_ref, qn_ref, kn_ref, vn_ref, s0_ref, ck_ref, cv_ref,
                     dl_ref, rpb_ref, mix_ref, tile_ref, bias_ref, dmat_ref):
    length = qr_ref.shape[0]
    lg = _log_sigmoid(dl_ref[...])

    @pl.when(pl.program_id(0) == 0)
    def _():
        _build_rpb_tiles(rpb_ref, tile_ref)
        for h in range(RET_HEADS):
            lgf = lg[h:h + 1, 0:1]
            lgb = lg[RET_HEADS + h:RET_HEADS + h + 1, 0:1]

            def rows_body(r, carry, h=h, lgf=lgf, lgb=lgb):
                r0 = pl.multiple_of(r * LANES, LANES)
                dmat_ref[h, pl.ds(r0, LANES), :] = _decay_matrix(length, lgf, lgb, r0, LANES).astype(BF16)
                return carry

            lax.fori_loop(0, length // LANES, rows_body, 0)

    for h in range(RET_HEADS):
        lgf = lg[h:h + 1, 0:1]
        lgb = lg[RET_HEADS + h:RET_HEADS + h + 1, 0:1]
        q = qr_ref[:, h * RET_DK:(h + 1) * RET_DK].astype(F32)
        k = kr_ref[:, h * RET_DK:(h + 1) * RET_DK].astype(F32) * (RET_DK ** -0.5)
        v = vr_ref[:, h * RET_DV:(h + 1) * RET_DV]
        o = _retention_head(q, k, v, lgf, lgb, s0_ref[h], s0_ref[RET_HEADS + h], dmat=dmat_ref[h])
        g = gr_ref[:, h * RET_DV:(h + 1) * RET_DV]
        mix_ref[:, h * RET_DV:(h + 1) * RET_DV] = (_head_norm(o) * _silu(g)).astype(mix_ref.dtype)
    base = RET_HEADS * RET_DV
    scale = NA_DH ** -0.5
    for h in range(NA_HEADS):
        _build_bias(tile_ref, bias_ref, h)
        sl = slice(h * NA_DH, (h + 1) * NA_DH)
        q = qn_ref[:, sl]
        s_band = _dot_nt(q, kn_ref[:, sl]) * scale + bias_ref[...]
        s_ctx = _dot_nt(q, ck_ref[:, sl].astype(BF16)) * scale
        o = _softmax_pv([(s_band, vn_ref[:, sl]), (s_ctx, cv_ref[:, sl].astype(BF16))])
        mix_ref[:, base + h * NA_DH:base + (h + 1) * NA_DH] = o.astype(mix_ref.dtype)


def _even_lat(qr, kr, vr, gr, qn, kn, vn, state, cache_k, cache_v, decay_rows, rpb_flat):
    def seq_spec(n):
        return pl.BlockSpec((DEC_SEQ, n), lambda i: (i, 0))
    cache_spec = pl.BlockSpec((None, PAST_LEN, NA_HEADS * NA_DH), lambda i: (i, 0, 0))
    return pl.pallas_call(
        _even_lat_kernel,
        grid=(DEC_BATCH,),
        in_specs=[seq_spec(n) for n in EVEN_SIZES]
        + [pl.BlockSpec((None, 2 * RET_HEADS, RET_DK, RET_DV), lambda i: (i, 0, 0, 0)),
           cache_spec, cache_spec,
           pl.BlockSpec((2 * RET_HEADS, LANES), lambda i: (0, 0)),
           pl.BlockSpec(memory_space=pltpu.SMEM)],
        out_specs=seq_spec(D_MODEL),
        out_shape=jax.ShapeDtypeStruct((DEC_BATCH * DEC_SEQ, D_MODEL), BF16),
        scratch_shapes=[pltpu.VMEM((NA_HEADS * RPB_ROWS, GRID_W, LANES), F32),
                        pltpu.VMEM((DEC_SEQ, DEC_SEQ), F32),
                        pltpu.VMEM((RET_HEADS, DEC_SEQ, DEC_SEQ), BF16)],
        compiler_params=_params(1),
        name="even_lat",
    )(qr, kr, vr, gr, qn, kn, vn, state, cache_k, cache_v, decay_rows, rpb_flat)


def _gated_conv(bg, cg, u, w_ref):
    length, ch = u.shape
    z = cg * u
    row = lax.broadcasted_iota(jnp.int32, (length, ch), 0)
    z_prev = jnp.where(row == 0, 0.0, pltpu.roll(z, 1, 0))
    z_next = jnp.where(row == length - 1, 0.0, pltpu.roll(z, length - 1, 0))
    return bg * (z_prev * w_ref[0:1, :] + z * w_ref[1:2, :] + z_next * w_ref[2:3, :])


def _odd_ctx_kernel(xn_ref, xc_ref, mod_ref, wi_ref, w_ref, qg_ref, kg_ref, wo_ref, lng_ref, lnb_ref,
                    y_ref, ko_ref, vo_ref, nxt_ref, cur_ref, mix_ref):
    cur = _project_next(xn_ref, mod_ref, wi_ref, nxt_ref, cur_ref)
    o_bg, o_cg, o_u, o_q, o_k, o_v = _offsets(ODD_SIZES)
    conv = _gated_conv(cur[:, o_bg:o_bg + CONV_CH], cur[:, o_cg:o_cg + CONV_CH], cur[:, o_u:o_u + CONV_CH], w_ref)
    mix_ref[:, 0:CONV_CH] = conv.astype(mix_ref.dtype)
    group = ATT_HEADS // ATT_KV_HEADS
    scale = ATT_DH ** -0.5
    for kv in range(ATT_KV_HEADS):
        k = _rms_norm(cur[:, o_k + kv * ATT_DH:o_k + (kv + 1) * ATT_DH], kg_ref[...])
        v = cur[:, o_v + kv * ATT_DH:o_v + (kv + 1) * ATT_DH]
        ko_ref[:, kv, :] = k
        vo_ref[:, kv, :] = v
        k = k.astype(BF16)
        v = v.astype(BF16)
        for g in range(group):
            h = kv * group + g
            q = _rms_norm(cur[:, o_q + h * ATT_DH:o_q + (h + 1) * ATT_DH], qg_ref[...]).astype(BF16)
            o = _softmax_pv([(_dot_nt(q, k) * scale, v)])
            mix_ref[:, CONV_CH + h * ATT_DH:CONV_CH + (h + 1) * ATT_DH] = o.astype(mix_ref.dtype)
    y = _dot(mix_ref[...], wo_ref[...])
    y_ref[...] = _post_norm(xc_ref[...], y, mod_ref[2:3, :], lng_ref[...], lnb_ref[...])
    cur_ref[...] = nxt_ref[...]


def _odd_ctx(x, mods, w_in, conv_w, qn_g, kn_g, w_out, ln_g, ln_b):
    head, tail, cur, full = _ctx_layer_specs(sum(ODD_SIZES))
    kv_shape = (SEQ, ATT_KV_HEADS, ATT_DH)
    return pl.pallas_call(
        _odd_ctx_kernel,
        grid=(BATCH + 1,),
        in_specs=head + [full((3, CONV_CH)), full((1, ATT_DH)), full((1, ATT_DH))] + tail,
        out_specs=[cur((SEQ, D_MODEL)), cur(kv_shape), cur(kv_shape)],
        out_shape=[jax.ShapeDtypeStruct((BATCH * SEQ, D_MODEL), F32),
                   jax.ShapeDtypeStruct((BATCH * SEQ, ATT_KV_HEADS, ATT_DH), F32),
                   jax.ShapeDtypeStruct((BATCH * SEQ, ATT_KV_HEADS, ATT_DH), F32)],
        scratch_shapes=[pltpu.VMEM((SEQ, sum(ODD_SIZES)), F32),
                        pltpu.VMEM((SEQ, sum(ODD_SIZES)), F32),
                        pltpu.VMEM((SEQ, D_MODEL), BF16)],
        compiler_params=_params(1),
        name="odd_ctx",
    )(x, x, mods, w_in, conv_w, qn_g, kn_g, w_out, ln_g.reshape(1, D_MODEL), ln_b.reshape(1, D_MODEL))


def _rope(x, cos, sin, even_lane):
    swapped = jnp.where(even_lane, pltpu.roll(x, LANES - 1, 1), pltpu.roll(x, 1, 1))
    return x * cos + swapped * sin


def _odd_lat_kernel(x_ref, mod_ref, wi_ref, ck_ref, cv_ref, w_ref, qg_ref, kg_ref, cos_ref, sin_ref,
                    wo_ref, lng_ref, lnb_ref, y_ref, proj_ref, qb_ref, kb_ref, vb_ref, mix_ref):
    length = x_ref.shape[0]
    h_in = (x_ref[...] * (1.0 + mod_ref[1:2, :]) + mod_ref[0:1, :]).astype(BF16)
    proj_ref[...] = _dot(h_in, wi_ref[...])
    o_bg, o_cg, o_u, o_q, o_k, o_v = _offsets(ODD_SIZES)
    conv = _gated_conv(proj_ref[:, o_bg:o_bg + CONV_CH], proj_ref[:, o_cg:o_cg + CONV_CH],
                       proj_ref[:, o_u:o_u + CONV_CH], w_ref)
    mix_ref[:, 0:CONV_CH] = conv.astype(mix_ref.dtype)
    group = ATT_HEADS // ATT_KV_HEADS
    scale = ATT_DH ** -0.5
    even_lane = lax.broadcasted_iota(jnp.int32, (length, ATT_DH), 1) % 2 == 0
    cos = cos_ref[...]
    sin = sin_ref[...]
    for kv in range(ATT_KV_HEADS):
        sl = slice(kv * ATT_DH, (kv + 1) * ATT_DH)
        k = _rms_norm(proj_ref[:, o_k + kv * ATT_DH:o_k + (kv + 1) * ATT_DH], kg_ref[...])
        kb_ref[0:length, sl] = _rope(k, cos, sin, even_lane).astype(BF16)
        vb_ref[0:length, sl] = proj_ref[:, o_v + kv * ATT_DH:o_v + (kv + 1) * ATT_DH].astype(BF16)
    kb_ref[length:, :] = ck_ref[...].astype(BF16)
    vb_ref[length:, :] = cv_ref[...].astype(BF16)
    for h in range(ATT_HEADS):
        q = _rms_norm(proj_ref[:, o_q + h * ATT_DH:o_q + (h + 1) * ATT_DH], qg_ref[...])
        qb_ref[:, h * ATT_DH:(h + 1) * ATT_DH] = _rope(q, cos, sin, even_lane).astype(BF16)

    def q_block(b, carry):
        rows = pl.ds(pl.multiple_of(b * LAT_QBLK, LAT_QBLK), LAT_QBLK)
        for h in range(ATT_HEADS):
            sl = slice((h // group) * ATT_DH, (h // group + 1) * ATT_DH)
            q = qb_ref[rows, h * ATT_DH:(h + 1) * ATT_DH]
            o = _softmax_pv([(_dot_nt(q, kb_ref[:, sl]) * scale, vb_ref[:, sl])])
            mix_ref[rows, CONV_CH + h * ATT_DH:CONV_CH + (h + 1) * ATT_DH] = o.astype(mix_ref.dtype)
        return carry

    lax.fori_loop(0, length // LAT_QBLK, q_block, 0)
    y = _dot(mix_ref[...], wo_ref[...])
    y_ref[...] = _post_norm(x_ref[...], y, mod_ref[2:3, :], lng_ref[...], lnb_ref[...])


def _odd_lat(x, mods, w_in, cache_k, cache_v, conv_w, qn_g, kn_g, cos, sin, w_out, ln_g, ln_b):
    def full(shape):
        return pl.BlockSpec(shape, lambda i: (0,) * len(shape))

    def const(shape):
        return pl.BlockSpec(shape, lambda i: (0,) * len(shape), pipeline_mode=pl.Buffered(1))
    kv_w = ATT_KV_HEADS * ATT_DH
    cache_spec = pl.BlockSpec((None, PAST_LEN, kv_w), lambda i: (i, 0, 0))
    return pl.pallas_call(
        _odd_lat_kernel,
        grid=(DEC_BATCH,),
        in_specs=[pl.BlockSpec((DEC_SEQ, D_MODEL), lambda i: (i, 0)),
                  pl.BlockSpec((None, N_MOD, D_MODEL), lambda i: (i, 0, 0)),
                  const((D_MODEL, sum(ODD_SIZES))),
                  cache_spec, cache_spec, full((3, CONV_CH)), full((1, ATT_DH)), full((1, ATT_DH)),
                  const((DEC_SEQ, ATT_DH)), const((DEC_SEQ, ATT_DH)),
                  const((D_MODEL, D_MODEL)), full((1, D_MODEL)), full((1, D_MODEL))],
        out_specs=pl.BlockSpec((DEC_SEQ, D_MODEL), lambda i: (i, 0)),
        out_shape=jax.ShapeDtypeStruct((DEC_BATCH * DEC_SEQ, D_MODEL), F32),
        scratch_shapes=[pltpu.VMEM((DEC_SEQ, sum(ODD_SIZES)), F32),
                        pltpu.VMEM((DEC_SEQ, ATT_HEADS * ATT_DH), BF16),
                        pltpu.VMEM((DEC_SEQ + PAST_LEN, kv_w), BF16),
                        pltpu.VMEM((DEC_SEQ + PAST_LEN, kv_w), BF16),
                        pltpu.VMEM((DEC_SEQ, D_MODEL), BF16)],
        compiler_params=_params(1),
        name="odd_lat",
    )(x, mods, w_in, cache_k, cache_v, conv_w, qn_g, kn_g, cos, sin, w_out,
      ln_g.reshape(1, D_MODEL), ln_b.reshape(1, D_MODEL))


def _rope_tables():
    t = jnp.arange(DEC_SEQ)
    row = (t // GRID_W).astype(F32)
    col = (t % GRID_W).astype(F32)
    n_freq = ATT_DH // 4
    freqs = ROPE_THETA ** (-jnp.arange(n_freq, dtype=F32) / n_freq)
    ang = jnp.concatenate([row[:, None] * freqs, col[:, None] * freqs], axis=-1)
    cos = jnp.repeat(jnp.cos(ang), 2, axis=-1)
    sin = jnp.stack([-jnp.sin(ang), jnp.sin(ang)], axis=-1).reshape(DEC_SEQ, ATT_DH)
    return cos, sin


def _route(scores, sel):
    n_tok = sel.shape[1]
    neg = -jnp.inf
    sub = lax.broadcasted_iota(jnp.int32, (GROUP_SIZE, n_tok), 0).astype(F32)
    blocks = [sel[g * GROUP_SIZE:(g + 1) * GROUP_SIZE, :] for g in range(N_GROUPS)]
    grp = []
    for blk in blocks:
        m1 = jnp.max(blk, axis=0, keepdims=True)
        i1 = jnp.min(jnp.where(blk == m1, sub, float(GROUP_SIZE)), axis=0, keepdims=True)
        m2 = jnp.max(jnp.where(sub == i1, neg, blk), axis=0, keepdims=True)
        grp.append(m1 + m2)
    masked = []
    for g in range(N_GROUPS):
        ahead = jnp.zeros((1, n_tok), F32)
        for o in range(N_GROUPS):
            if o == g:
                continue
            wins = grp[o] >= grp[g] if o < g else grp[o] > grp[g]
            ahead = ahead + jnp.where(wins, 1.0, 0.0)
        masked.append(jnp.where(ahead < float(TOPK_GROUPS), blocks[g], neg))
    val = jnp.concatenate(masked, axis=0)
    row = lax.broadcasted_iota(jnp.int32, (N_EXPERTS, n_tok), 0).astype(F32)
    w = jnp.zeros((N_EXPERTS, n_tok), F32)
    for _ in range(TOP_K):
        m = jnp.max(val, axis=0, keepdims=True)
        idx = jnp.min(jnp.where(val == m, row, float(N_EXPERTS)), axis=0, keepdims=True)
        pick = row == idx
        w = jnp.where(pick, scores, w)
        val = jnp.where(pick, neg, val)
    return w / jnp.sum(w, axis=0, keepdims=True) * ROUTED_SCALE


def _moe_kernel(y_ref, mod_ref, rwt_ref, rb_ref, wg_hbm, wu_hbm, wd_hbm, sg_ref, su_ref, sd_ref,
                lng_ref, lnb_ref, o_ref, xb_ref, gate_ref, acc_ref, wg_buf, wu_buf, wd_buf, sem,
                *, layer, n_steps):
    n_tok = y_ref.shape[0]

    def weight_copies(step, slot):
        experts = pl.ds(step * MOE_EPS, MOE_EPS)
        return (pltpu.make_async_copy(wg_hbm.at[layer, experts], wg_buf.at[slot], sem.at[0, slot]),
                pltpu.make_async_copy(wu_hbm.at[layer, experts], wu_buf.at[slot], sem.at[1, slot]),
                pltpu.make_async_copy(wd_hbm.at[layer, experts], wd_buf.at[slot], sem.at[2, slot]))

    for cp in weight_copies(0, 0):
        cp.start()

    x = y_ref[...] * (1.0 + mod_ref[4:5, :]) + mod_ref[3:4, :]
    x_hi, x_lo = _split(x)
    xb_ref[...] = x_hi
    w_hi, w_lo = _split(rwt_ref[...])
    logits = _dot_nt(w_hi, x_hi) + _dot_nt(w_lo, x_hi) + _dot_nt(w_hi, x_lo)
    scores = 1.0 / (1.0 + jnp.exp(-logits))
    gate_t = _route(scores, scores + rb_ref[...])
    gate_ref[...] = jnp.concatenate([gate_t, jnp.zeros_like(gate_t)], axis=0).T
    sgu = jnp.concatenate([sg_ref[...].astype(BF16), su_ref[...].astype(BF16)], axis=1)
    hs = _dot(x_hi, sgu)
    h_sh = (_silu(hs[:, :EXPERT_HIDDEN]) * hs[:, EXPERT_HIDDEN:]).astype(BF16)
    acc_ref[...] = _dot(h_sh, sd_ref[...].astype(BF16))

    def expert_group(step, slot):
        xb = xb_ref[...]
        g_rot = pltpu.roll(gate_ref[...], jnp.bitwise_and(LANES - step * MOE_EPS, LANES - 1), 1)
        acc = acc_ref[...]
        for p in range(MOE_EPS // 2):
            wg2 = jnp.concatenate([wg_buf[slot, 2 * p].astype(BF16), wg_buf[slot, 2 * p + 1].astype(BF16)], axis=1)
            wu2 = jnp.concatenate([wu_buf[slot, 2 * p].astype(BF16), wu_buf[slot, 2 * p + 1].astype(BF16)], axis=1)
            hg = _dot(xb, wg2)
            hu = _dot(xb, wu2)
            g2 = jnp.concatenate(
                [jnp.broadcast_to(g_rot[:, 2 * p:2 * p + 1], (n_tok, EXPERT_HIDDEN)),
                 jnp.broadcast_to(g_rot[:, 2 * p + 1:2 * p + 2], (n_tok, EXPERT_HIDDEN))], axis=1)
            h = (_silu(hg) * hu * g2).astype(BF16)
            wd2 = wd_buf[slot, 2 * p:2 * p + 2].reshape(2 * EXPERT_HIDDEN, D_MODEL).astype(BF16)
            acc = acc + _dot(h, wd2)
        acc_ref[...] = acc

    def two_groups(k, carry):
        step = 2 * k
        for cp in weight_copies(step, 0):
            cp.wait()
        for cp in weight_copies(step + 1, 1):
            cp.start()
        expert_group(step, 0)

        for cp in weight_copies(step + 1, 1):
            cp.wait()

        @pl.when(step + 2 < n_steps)
        def _():
            for cp in weight_copies(step + 2, 0):
                cp.start()

        expert_group(step + 1, 1)
        return carry

    lax.fori_loop(0, n_steps // 2, two_groups, 0)
    o_ref[...] = _post_norm(y_ref[...], acc_ref[...], mod_ref[5:6, :], lng_ref[...], lnb_ref[...])


def _moe(y, mods, layer, rwt, rb, exp_w_gate, exp_w_up, exp_w_down, sg, su, sd, ln_g, ln_b, seq_len):
    t = y.shape[0]
    tile = MOE_TILE
    n_steps = N_EXPERTS // MOE_EPS
    assert n_steps % 2 == 0
    mod_idx = (lambda i: (i * tile // seq_len, 0, 0)) if mods.shape[0] > 1 else (lambda i: (0, 0, 0))

    def full(shape):
        return pl.BlockSpec(shape, lambda i: (0,) * len(shape))
    hbm = pl.BlockSpec(memory_space=pl.ANY)
    return pl.pallas_call(
        functools.partial(_moe_kernel, layer=layer, n_steps=n_steps),
        grid=(t // tile,),
        in_specs=[pl.BlockSpec((tile, D_MODEL), lambda i: (i, 0)),
                  pl.BlockSpec((None, N_MOD, D_MODEL), mod_idx),
                  full((N_EXPERTS, D_MODEL)),
                  full((N_EXPERTS, 1)),
                  hbm, hbm, hbm,
                  full((D_MODEL, EXPERT_HIDDEN)),
                  full((D_MODEL, EXPERT_HIDDEN)),
                  full((EXPERT_HIDDEN, D_MODEL)),
                  full((1, D_MODEL)),
                  full((1, D_MODEL))],
        out_specs=pl.BlockSpec((tile, D_MODEL), lambda i: (i, 0)),
        out_shape=jax.ShapeDtypeStruct((t, D_MODEL), F32),
        scratch_shapes=[pltpu.VMEM((tile, D_MODEL), BF16),
                        pltpu.VMEM((tile, LANES), F32),
                        pltpu.VMEM((tile, D_MODEL), F32),
                        pltpu.VMEM((2, MOE_EPS, D_MODEL, EXPERT_HIDDEN), F32),
                        pltpu.VMEM((2, MOE_EPS, D_MODEL, EXPERT_HIDDEN), F32),
                        pltpu.VMEM((2, MOE_EPS, EXPERT_HIDDEN, D_MODEL), F32),
                        pltpu.SemaphoreType.DMA((3, 2))],
        compiler_params=_params(1),
        name="moe",
    )(y, mods, rwt, rb, exp_w_gate, exp_w_up, exp_w_down, sg, su, sd,
      ln_g.reshape(1, D_MODEL), ln_b.reshape(1, D_MODEL))


def kernel(x_prompt, x_sample, c, state_ret, cache_na_k, cache_na_v, cache_att_k, cache_att_v, c_ctx,
           w_mod, b_mod, ln_g, ln_b, w_in_even, w_out_even, ret_decay_logit, na_rpb,
           w_in_odd, w_out_odd, conv_w, q_norm_g, k_norm_g, router_w, router_b,
           exp_w_gate, exp_w_up, exp_w_down, sh_w_gate, sh_w_up, sh_w_down):
    yp = x_prompt.reshape(BATCH * SEQ, D_MODEL)
    ys = x_sample.reshape(DEC_BATCH * DEC_SEQ, D_MODEL)

    cond = jnp.zeros((COND_ROWS, D_MODEL), F32).at[0].set(c_ctx).at[1:1 + DEC_BATCH].set(c)
    mods = _adaln(cond, w_mod, b_mod).reshape(DEPTH, COND_ROWS, N_MOD, D_MODEL)

    out = {}
    for l in range(DEPTH):
        i = l // 2
        mp = mods[l, 0:1]
        ms = mods[l, 1:1 + DEC_BATCH]
        if l % 2 == 0:
            w_in = w_in_even[i].astype(BF16)
            w_out = w_out_even[i].astype(BF16)
            decay_rows = jnp.broadcast_to(ret_decay_logit[i].reshape(2 * RET_HEADS, 1), (2 * RET_HEADS, LANES))
            dts = (BF16, BF16, BF16, F32, BF16, BF16, BF16)
            sp = _inproj(ys, ms, w_in, EVEN_SIZES, dts, DEC_SEQ)
            yp, st, na_k, na_v = _even_ctx(yp, mp, w_in, decay_rows, w_out, ln_g[l, 0], ln_b[l, 0])
            mix_s = _even_lat(
                *sp,
                state_ret[:, i].reshape(DEC_BATCH, 2 * RET_HEADS, RET_DK, RET_DV),
                cache_na_k[:, i].reshape(DEC_BATCH, PAST_LEN, NA_HEADS * NA_DH),
                cache_na_v[:, i].reshape(DEC_BATCH, PAST_LEN, NA_HEADS * NA_DH),
                decay_rows, na_rpb[i].reshape(-1))
            ys = _outproj(mix_s, ys, ms, w_out, ln_g[l, 0], ln_b[l, 0], DEC_SEQ)
            out.setdefault("st", []).append(st.reshape(BATCH, 2, RET_HEADS, RET_DK, RET_DV))
            out.setdefault("na_k", []).append(na_k.reshape(BATCH, SEQ, NA_HEADS, NA_DH))
            out.setdefault("na_v", []).append(na_v.reshape(BATCH, SEQ, NA_HEADS, NA_DH))
        else:
            w_in = w_in_odd[i].astype(BF16)
            w_out = w_out_odd[i].astype(BF16)
            qg = q_norm_g[i].reshape(1, ATT_DH)
            kg = k_norm_g[i].reshape(1, ATT_DH)
            yp, k_new, att_v = _odd_ctx(yp, mp, w_in, conv_w[i], qg, kg, w_out, ln_g[l, 0], ln_b[l, 0])
            cos, sin = _rope_tables()
            kv_w = ATT_KV_HEADS * ATT_DH
            ys = _odd_lat(ys, ms, w_in,
                          cache_att_k[:, i].reshape(DEC_BATCH, PAST_LEN, kv_w),
                          cache_att_v[:, i].reshape(DEC_BATCH, PAST_LEN, kv_w),
                          conv_w[i], qg, kg, cos, sin, w_out, ln_g[l, 0], ln_b[l, 0])
            out.setdefault("att_k", []).append(k_new.reshape(BATCH, SEQ, ATT_KV_HEADS, ATT_DH))
            out.setdefault("att_v", []).append(att_v.reshape(BATCH, SEQ, ATT_KV_HEADS, ATT_DH))
        rwt = router_w[l].T
        rb = router_b[l].reshape(N_EXPERTS, 1)
        moe_w = (rwt, rb, exp_w_gate, exp_w_up, exp_w_down, sh_w_gate[l], sh_w_up[l], sh_w_down[l],
                 ln_g[l, 1], ln_b[l, 1])
        yp = _moe(yp, mp, l, *moe_w, SEQ)
        ys = _moe(ys, ms, l, *moe_w, DEC_SEQ)

    return (yp.reshape(BATCH, SEQ, D_MODEL),
            ys.reshape(DEC_BATCH, DEC_SEQ, D_MODEL),
            jnp.stack(out["st"], axis=1),
            jnp.stack(out["na_k"], axis=1),
            jnp.stack(out["na_v"], axis=1),
            jnp.stack(out["att_k"], axis=1),
            jnp.stack(out["att_v"], axis=1))
```

```python
import functools

import jax
import jax.numpy as jnp
from jax import lax
from jax.experimental import pallas as pl
from jax.experimental.pallas import tpu as pltpu

F32 = jnp.float32
BF16 = jnp.bfloat16

D_MODEL = 1024
BATCH = 32
SEQ = 256
DEPTH = 2
DEC_BATCH = 4
DEC_SEQ = 1024
PAST_LEN = 512
GRID_W = 64
GRID_H = DEC_SEQ // GRID_W
MIX_HALF = D_MODEL // 2
RET_HEADS = 4
RET_DV = MIX_HALF // RET_HEADS
RET_DK = RET_DV // 2
NA_HEADS = 4
NA_DH = MIX_HALF // NA_HEADS
NA_WIN_R = 8
NA_WIN_C = 16
CONV_CH = MIX_HALF
ATT_HEADS = 4
ATT_KV_HEADS = 2
ATT_DH = MIX_HALF // ATT_HEADS
ROPE_THETA = 10000.0
N_EXPERTS = 64
EXPERT_HIDDEN = D_MODEL // 8
TOP_K = 8
N_GROUPS = 8
GROUP_SIZE = N_EXPERTS // N_GROUPS
TOPK_GROUPS = 4
ROUTED_SCALE = 2.5
DEEPNORM_ALPHA = (2 * DEPTH) ** 0.25
LN_EPS = 1e-5
RMS_EPS = 1e-6
EVEN_SIZES = (RET_HEADS * RET_DK, RET_HEADS * RET_DK, RET_HEADS * RET_DV, RET_HEADS * RET_DV,
              NA_HEADS * NA_DH, NA_HEADS * NA_DH, NA_HEADS * NA_DH)
ODD_SIZES = (CONV_CH, CONV_CH, CONV_CH, ATT_HEADS * ATT_DH, ATT_KV_HEADS * ATT_DH, ATT_KV_HEADS * ATT_DH)
N_MOD = 6
RPB_ROWS = 2 * NA_WIN_R - 1
RPB_COLS = 2 * NA_WIN_C - 1

LANES = 128
COND_ROWS = 16
V7X_VMEM_LIMIT = 56 * 1024 * 1024

PROJ_TILE = 512
LAT_QBLK = 256
MOE_TILE = 1024
MOE_EPS = 4
ADALN_TN = 1536


def _params(n_axes, vmem_limit=V7X_VMEM_LIMIT):
    return pltpu.CompilerParams(dimension_semantics=("arbitrary",) * n_axes,
                                vmem_limit_bytes=vmem_limit)


def _dot(a, b):
    return jnp.dot(a, b, preferred_element_type=F32)


def _dot_nt(a, b):
    return lax.dot_general(a, b, (((1,), (1,)), ((), ())), preferred_element_type=F32)


def _split(a):
    hi = a.astype(BF16)
    lo = (a - hi.astype(F32)).astype(BF16)
    return hi, lo


def _silu(x):
    return x / (1.0 + jnp.exp(-x))


def _log_sigmoid(x):
    return jnp.minimum(x, 0.0) - jnp.log(1.0 + jnp.exp(-jnp.abs(x)))


def _layer_norm(z, g, b):
    mu = jnp.mean(z, axis=-1, keepdims=True)
    zc = z - mu
    var = jnp.mean(zc * zc, axis=-1, keepdims=True)
    return zc * lax.rsqrt(var + LN_EPS) * g + b


def _post_norm(x, y, gate, g, b):
    return _layer_norm(DEEPNORM_ALPHA * x + (1.0 + gate) * y, g, b)


def _head_norm(x):
    mu = jnp.mean(x, axis=-1, keepdims=True)
    xc = x - mu
    var = jnp.mean(xc * xc, axis=-1, keepdims=True)
    return xc * lax.rsqrt(var + LN_EPS)


def _rms_norm(x, g):
    return x * lax.rsqrt(jnp.mean(x * x, axis=-1, keepdims=True) + RMS_EPS) * g


def _softmax_pv(parts):
    m = None
    for s, _ in parts:
        mi = jnp.max(s, axis=-1, keepdims=True)
        m = mi if m is None else jnp.maximum(m, mi)
    l = None
    o = None
    for s, v in parts:
        p = jnp.exp(s - m)
        li = jnp.sum(p, axis=-1, keepdims=True)
        oi = _dot(p.astype(BF16), v)
        l = li if l is None else l + li
        o = oi if o is None else o + oi
    return o / l


def _adaln_kernel(c_ref, w_ref, b_ref, o_ref):
    a_hi, a_lo = _split(_silu(c_ref[...]))
    w_hi, w_lo = _split(w_ref[...])
    o_ref[...] = _dot(a_hi, w_hi) + _dot(a_lo, w_hi) + _dot(a_hi, w_lo) + b_ref[...]


def _adaln(cond, w_mod, b_mod):
    n = w_mod.shape[-1]
    return pl.pallas_call(
        _adaln_kernel,
        grid=(DEPTH, n // ADALN_TN),
        in_specs=[pl.BlockSpec((COND_ROWS, D_MODEL), lambda l, j: (0, 0)),
                  pl.BlockSpec((None, D_MODEL, ADALN_TN), lambda l, j: (l, 0, j)),
                  pl.BlockSpec((None, 1, ADALN_TN), lambda l, j: (l, 0, j))],
        out_specs=pl.BlockSpec((None, COND_ROWS, ADALN_TN), lambda l, j: (l, 0, j)),
        out_shape=jax.ShapeDtypeStruct((DEPTH, COND_ROWS, n), F32),
        compiler_params=_params(2),
        name="adaln",
    )(cond, w_mod, b_mod.reshape(DEPTH, 1, n))


def _inproj_kernel(x_ref, mod_ref, w_ref, *out_refs, sizes):
    shift = mod_ref[0:1, :]
    scale = mod_ref[1:2, :]
    h = (x_ref[...] * (1.0 + scale) + shift).astype(BF16)
    y = _dot(h, w_ref[...])
    for o_ref, off, n in zip(out_refs, _offsets(sizes), sizes):
        o_ref[...] = y[:, off:off + n].astype(o_ref.dtype)


def _inproj(x, mods, w_bf16, sizes, dtypes, seq_len):
    t = x.shape[0]
    tile = PROJ_TILE
    mod_idx = (lambda i: (i * tile // seq_len, 0, 0)) if mods.shape[0] > 1 else (lambda i: (0, 0, 0))
    return pl.pallas_call(
        functools.partial(_inproj_kernel, sizes=sizes),
        grid=(t // tile,),
        in_specs=[pl.BlockSpec((tile, D_MODEL), lambda i: (i, 0)),
                  pl.BlockSpec((None, N_MOD, D_MODEL), mod_idx),
                  pl.BlockSpec(w_bf16.shape, lambda i: (0, 0))],
        out_specs=[pl.BlockSpec((tile, n), lambda i: (i, 0)) for n in sizes],
        out_shape=[jax.ShapeDtypeStruct((t, n), dt) for n, dt in zip(sizes, dtypes)],
        compiler_params=_params(1),
        name="inproj",
    )(x, mods, w_bf16)


def _outproj_kernel(mix_ref, x_ref, mod_ref, w_ref, g_ref, b_ref, o_ref):
    y = _dot(mix_ref[...], w_ref[...])
    o_ref[...] = _post_norm(x_ref[...], y, mod_ref[2:3, :], g_ref[...], b_ref[...])


def _outproj(mix, x, mods, w_bf16, ln_g, ln_b, seq_len):
    t = x.shape[0]
    tile = PROJ_TILE
    mod_idx = (lambda i: (i * tile // seq_len, 0, 0)) if mods.shape[0] > 1 else (lambda i: (0, 0, 0))
    return pl.pallas_call(
        _outproj_kernel,
        grid=(t // tile,),
        in_specs=[pl.BlockSpec((tile, D_MODEL), lambda i: (i, 0)),
                  pl.BlockSpec((tile, D_MODEL), lambda i: (i, 0)),
                  pl.BlockSpec((None, N_MOD, D_MODEL), mod_idx),
                  pl.BlockSpec((D_MODEL, D_MODEL), lambda i: (0, 0)),
                  pl.BlockSpec((1, D_MODEL), lambda i: (0, 0)),
                  pl.BlockSpec((1, D_MODEL), lambda i: (0, 0))],
        out_specs=pl.BlockSpec((tile, D_MODEL), lambda i: (i, 0)),
        out_shape=jax.ShapeDtypeStruct((t, D_MODEL), F32),
        compiler_params=_params(1),
        name="outproj",
    )(mix, x, mods, w_bf16, ln_g.reshape(1, D_MODEL), ln_b.reshape(1, D_MODEL))


def _decay_matrix(length, lgf, lgb, row0=0, rows=None):
    rows = length if rows is None else rows
    ii = lax.broadcasted_iota(jnp.int32, (rows, length), 0) + row0
    jj = lax.broadcasted_iota(jnp.int32, (rows, length), 1)
    rel = (ii - jj).astype(F32)
    fwd = jnp.where(rel >= 0.0, jnp.exp(lgf * jnp.maximum(rel, 0.0)), 0.0)
    bwd = jnp.where(rel <= 0.0, jnp.exp(lgb * jnp.maximum(-rel, 0.0)), 0.0)
    return fwd + bwd


def _retention_head(q, k, v, lgf, lgb, s0f, s0b, dmat=None):
    length = q.shape[0]
    s = _dot_nt(q.astype(BF16), k.astype(BF16))
    if dmat is None:
        dmat = _decay_matrix(length, lgf, lgb)
    if dmat.dtype == BF16:
        o = _dot(s.astype(BF16) * dmat, v)
    else:
        o = _dot((s * dmat).astype(BF16), v)
    if s0f is not None:
        pos = lax.broadcasted_iota(jnp.int32, (length, 1), 0).astype(F32)
        qf = q * jnp.exp(lgf * (pos + 1.0))
        qb = q * jnp.exp(lgb * (length - pos))
        o = o + _dot(qf.astype(BF16), s0f.astype(BF16)) + _dot(qb.astype(BF16), s0b.astype(BF16))
    return o


def _offsets(sizes):
    return [sum(sizes[:k]) for k in range(len(sizes))]


def _project_next(xn_ref, mod_ref, wi_ref, nxt_ref, cur_ref):
    @pl.when(pl.program_id(0) == 0)
    def _():
        cur_ref[...] = jnp.zeros(cur_ref.shape, cur_ref.dtype)

    h = (xn_ref[...] * (1.0 + mod_ref[1:2, :]) + mod_ref[0:1, :]).astype(BF16)
    nxt_ref[...] = _dot(h, wi_ref[...])
    return cur_ref


def _even_ctx_kernel(xn_ref, xc_ref, mod_ref, wi_ref, dl_ref, wo_ref, lng_ref, lnb_ref,
                     y_ref, st_ref, nak_ref, nav_ref, nxt_ref, cur_ref, mix_ref, dmat_ref, kdec_ref):
    length = xc_ref.shape[0]

    @pl.when(pl.program_id(0) == 0)
    def _():
        lg = _log_sigmoid(dl_ref[...])
        pos = lax.broadcasted_iota(jnp.int32, (length, 1), 0).astype(F32)
        for h in range(RET_HEADS):
            lgf = lg[h:h + 1, 0:1]
            lgb = lg[RET_HEADS + h:RET_HEADS + h + 1, 0:1]
            dmat_ref[h] = _decay_matrix(length, lgf, lgb)
            kdec_ref[h] = jnp.broadcast_to(jnp.exp(lgf * (length - 1.0 - pos)), (length, RET_DK))
            kdec_ref[RET_HEADS + h] = jnp.broadcast_to(jnp.exp(lgb * pos), (length, RET_DK))

    cur = _project_next(xn_ref, mod_ref, wi_ref, nxt_ref, cur_ref)
    o_qr, o_kr, o_vr, o_gr, o_qn, o_kn, o_vn = _offsets(EVEN_SIZES)
    eye = jnp.where(lax.broadcasted_iota(jnp.int32, (2 * RET_DK, 2 * RET_DK), 0)
                    == lax.broadcasted_iota(jnp.int32, (2 * RET_DK, 2 * RET_DK), 1), 1.0, 0.0).astype(BF16)
    for h in range(RET_HEADS):
        q = cur[:, o_qr + h * RET_DK:o_qr + (h + 1) * RET_DK]
        k = cur[:, o_kr + h * RET_DK:o_kr + (h + 1) * RET_DK] * (RET_DK ** -0.5)
        v = cur[:, o_vr + h * RET_DV:o_vr + (h + 1) * RET_DV].astype(BF16)
        o = _retention_head(q, k, v, None, None, None, None, dmat=dmat_ref[h])
        k_dec = jnp.concatenate([k * kdec_ref[h], k * kdec_ref[RET_HEADS + h]], axis=1).astype(BF16)
        states = _dot(_dot_nt(eye, k_dec).astype(BF16), v)
        st_ref[h] = states[:RET_DK]
        st_ref[RET_HEADS + h] = states[RET_DK:]
        g = cur[:, o_gr + h * RET_DV:o_gr + (h + 1) * RET_DV]
        mix_ref[:, h * RET_DV:(h + 1) * RET_DV] = (_head_norm(o) * _silu(g)).astype(mix_ref.dtype)
    base = RET_HEADS * RET_DV
    for h in range(NA_HEADS):
        q = cur[:, o_qn + h * NA_DH:o_qn + (h + 1) * NA_DH].astype(BF16)
        k = cur[:, o_kn + h * NA_DH:o_kn + (h + 1) * NA_DH]
        v = cur[:, o_vn + h * NA_DH:o_vn + (h + 1) * NA_DH]
        nak_ref[:, h, :] = k
        nav_ref[:, h, :] = v
        s = _dot_nt(q, k.astype(BF16)) * (NA_DH ** -0.5)
        mix_ref[:, base + h * NA_DH:base + (h + 1) * NA_DH] = _softmax_pv([(s, v.astype(BF16))]).astype(mix_ref.dtype)
    y = _dot(mix_ref[...], wo_ref[...])
    y_ref[...] = _post_norm(xc_ref[...], y, mod_ref[2:3, :], lng_ref[...], lnb_ref[...])
    cur_ref[...] = nxt_ref[...]


def _ctx_layer_specs(n_proj):
    def nxt(n):
        return pl.BlockSpec((SEQ, n), lambda i: (jnp.minimum(i, BATCH - 1), 0))

    def cur(shape):
        return pl.BlockSpec(shape, lambda i: (jnp.maximum(i - 1, 0),) + (0,) * (len(shape) - 1))

    def full(shape):
        return pl.BlockSpec(shape, lambda i: (0,) * len(shape))

    def const(shape):
        return pl.BlockSpec(shape, lambda i: (0,) * len(shape), pipeline_mode=pl.Buffered(1))
    head = [nxt(D_MODEL), cur((SEQ, D_MODEL)), full((None, N_MOD, D_MODEL)), const((D_MODEL, n_proj))]
    tail = [const((D_MODEL, D_MODEL)), full((1, D_MODEL)), full((1, D_MODEL))]
    return head, tail, cur, full


def _even_ctx(x, mods, w_in, decay_rows, w_out, ln_g, ln_b):
    head, tail, cur, full = _ctx_layer_specs(sum(EVEN_SIZES))
    return pl.pallas_call(
        _even_ctx_kernel,
        grid=(BATCH + 1,),
        in_specs=head + [full((2 * RET_HEADS, LANES))] + tail,
        out_specs=[cur((SEQ, D_MODEL)),
                   cur((None, 2 * RET_HEADS, RET_DK, RET_DV)),
                   cur((SEQ, NA_HEADS, NA_DH)), cur((SEQ, NA_HEADS, NA_DH))],
        out_shape=[jax.ShapeDtypeStruct((BATCH * SEQ, D_MODEL), F32),
                   jax.ShapeDtypeStruct((BATCH, 2 * RET_HEADS, RET_DK, RET_DV), F32),
                   jax.ShapeDtypeStruct((BATCH * SEQ, NA_HEADS, NA_DH), F32),
                   jax.ShapeDtypeStruct((BATCH * SEQ, NA_HEADS, NA_DH), F32)],
        scratch_shapes=[pltpu.VMEM((SEQ, sum(EVEN_SIZES)), F32),
                        pltpu.VMEM((SEQ, sum(EVEN_SIZES)), F32),
                        pltpu.VMEM((SEQ, D_MODEL), BF16),
                        pltpu.VMEM((RET_HEADS, SEQ, SEQ), F32),
                        pltpu.VMEM((2 * RET_HEADS, SEQ, RET_DK), F32)],
        compiler_params=_params(1),
        name="even_ctx",
    )(x, x, mods, w_in, decay_rows, w_out, ln_g.reshape(1, D_MODEL), ln_b.reshape(1, D_MODEL))


def _build_rpb_tiles(rpb_ref, tile_ref):
    qc = lax.broadcasted_iota(jnp.int32, (GRID_W, LANES), 0)
    kc = lax.broadcasted_iota(jnp.int32, (GRID_W, LANES), 1) % GRID_W
    diff = kc - qc + (NA_WIN_C - 1)
    start = jnp.clip(qc - NA_WIN_C // 2, 0, GRID_W - NA_WIN_C)
    win = (kc >= start) & (kc < start + NA_WIN_C)

    def body(idx, carry):
        t = jnp.zeros((GRID_W, LANES), F32)
        for d in range(RPB_COLS):
            t = jnp.where(diff == d, rpb_ref[idx * RPB_COLS + d], t)
        tile_ref[idx] = jnp.where(win, t, -jnp.inf)
        return carry

    lax.fori_loop(0, NA_HEADS * RPB_ROWS, body, 0)


def _build_bias(tile_ref, bias_ref, head):
    left = lax.broadcasted_iota(jnp.int32, (GRID_W, LANES), 1) < GRID_W
    neg = jnp.full((GRID_W, LANES), -jnp.inf, F32)
    rows_w = min(NA_WIN_R, GRID_H)
    for qr in range(GRID_H):
        rs = min(max(qr - rows_w // 2, 0), GRID_H - rows_w)

        def tile(kr):
            if rs <= kr < rs + rows_w:
                return tile_ref[head * RPB_ROWS + kr - qr + NA_WIN_R - 1]
            return neg

        pieces = [jnp.where(left, tile(2 * a), tile(2 * a + 1)) for a in range(GRID_H // 2)]
        bias_ref[qr * GRID_W:(qr + 1) * GRID_W, :] = jnp.concatenate(pieces, axis=1)


def _even_lat_kernel(qr_ref, kr_ref, vr_ref, gr_ref, qn_ref, kn_ref, vn_ref, s0_ref, ck_ref, cv_ref,
                     dl_ref, rpb_ref, mix_ref, tile_ref, bias_ref, dmat_ref):
    length = qr_ref.shape[0]
    lg = _log_sigmoid(dl_ref[...])

    @pl.when(pl.program_id(0) == 0)
    def _():
        _build_rpb_tiles(rpb_ref, tile_ref)
        for h in range(RET_HEADS):
            lgf = lg[h:h + 1, 0:1]
            lgb = lg[RET_HEADS + h:RET_HEADS + h + 1, 0:1]

            def rows_body(r, carry, h=h, lgf=lgf, lgb=lgb):
                r0 = pl.multiple_of(r * LANES, LANES)
                dmat_ref[h, pl.ds(r0, LANES), :] = _decay_matrix(length, lgf, lgb, r0, LANES).astype(BF16)
                return carry

            lax.fori_loop(0, length // LANES, rows_body, 0)

    for h in range(RET_HEADS):
        lgf = lg[h:h + 1, 0:1]
        lgb = lg[RET_HEADS + h:RET_HEADS + h + 1, 0:1]
        q = qr_ref[:, h * RET_DK:(h + 1) * RET_DK].astype(F32)
        k = kr_ref[:, h * RET_DK:(h + 1) * RET_DK].astype(F32) * (RET_DK ** -0.5)
        v = vr_ref[:, h * RET_DV:(h + 1) * RET_DV]
        o = _retention_head(q, k, v, lgf, lgb, s0_ref[0, h], s0_ref[1, h], dmat=dmat_ref[h])
        g = gr_ref[:, h * RET_DV:(h + 1) * RET_DV]
        mix_ref[:, h * RET_DV:(h + 1) * RET_DV] = (_head_norm(o) * _silu(g)).astype(mix_ref.dtype)
    base = RET_HEADS * RET_DV
    scale = NA_DH ** -0.5
    for h in range(NA_HEADS):
        _build_bias(tile_ref, bias_ref, h)
        sl = slice(h * NA_DH, (h + 1) * NA_DH)
        q = qn_ref[:, sl]
        s_band = _dot_nt(q, kn_ref[:, sl]) * scale + bias_ref[...]
        s_ctx = _dot_nt(q, ck_ref[:, sl].astype(BF16)) * scale
        o = _softmax_pv([(s_band, vn_ref[:, sl]), (s_ctx, cv_ref[:, sl].astype(BF16))])
        mix_ref[:, base + h * NA_DH:base + (h + 1) * NA_DH] = o.astype(mix_ref.dtype)


def _even_lat(qr, kr, vr, gr, qn, kn, vn, state, cache_k, cache_v, idx, decay_rows, rpb_flat):
    def seq_spec(n):
        return pl.BlockSpec((DEC_SEQ, n), lambda i: (i, 0))
    cache_spec = pl.BlockSpec((None, PAST_LEN, NA_HEADS * NA_DH), lambda i: (i, 0, 0))
    return pl.pallas_call(
        _even_lat_kernel,
        grid=(DEC_BATCH,),
        in_specs=[seq_spec(n) for n in EVEN_SIZES]
        + [pl.BlockSpec((None, None, 2, RET_HEADS, RET_DK, RET_DV), lambda i: (i, idx, 0, 0, 0, 0)),
           cache_spec, cache_spec,
           pl.BlockSpec((2 * RET_HEADS, LANES), lambda i: (0, 0)),
           pl.BlockSpec(memory_space=pltpu.SMEM)],
        out_specs=seq_spec(D_MODEL),
        out_shape=jax.ShapeDtypeStruct((DEC_BATCH * DEC_SEQ, D_MODEL), BF16),
        scratch_shapes=[pltpu.VMEM((NA_HEADS * RPB_ROWS, GRID_W, LANES), F32),
                        pltpu.VMEM((DEC_SEQ, DEC_SEQ), F32),
                        pltpu.VMEM((RET_HEADS, DEC_SEQ, DEC_SEQ), BF16)],
        compiler_params=_params(1),
        name="even_lat",
    )(qr, kr, vr, gr, qn, kn, vn, state, cache_k, cache_v, decay_rows, rpb_flat)


def _gated_conv(bg, cg, u, w_ref):
    length, ch = u.shape
    z = cg * u
    row = lax.broadcasted_iota(jnp.int32, (length, ch), 0)
    z_prev = jnp.where(row == 0, 0.0, pltpu.roll(z, 1, 0))
    z_next = jnp.where(row == length - 1, 0.0, pltpu.roll(z, length - 1, 0))
    return bg * (z_prev * w_ref[0:1, :] + z * w_ref[1:2, :] + z_next * w_ref[2:3, :])


def _odd_ctx_kernel(xn_ref, xc_ref, mod_ref, wi_ref, w_ref, qg_ref, kg_ref, wo_ref, lng_ref, lnb_ref,
                    y_ref, ko_ref, vo_ref, nxt_ref, cur_ref, mix_ref):
    cur = _project_next(xn_ref, mod_ref, wi_ref, nxt_ref, cur_ref)
    o_bg, o_cg, o_u, o_q, o_k, o_v = _offsets(ODD_SIZES)
    conv = _gated_conv(cur[:, o_bg:o_bg + CONV_CH], cur[:, o_cg:o_cg + CONV_CH], cur[:, o_u:o_u + CONV_CH], w_ref)
    mix_ref[:, 0:CONV_CH] = conv.astype(mix_ref.dtype)
    group = ATT_HEADS // ATT_KV_HEADS
    scale = ATT_DH ** -0.5
    for kv in range(ATT_KV_HEADS):
        k = _rms_norm(cur[:, o_k + kv * ATT_DH:o_k + (kv + 1) * ATT_DH], kg_ref[...])
        v = cur[:, o_v + kv * ATT_DH:o_v + (kv + 1) * ATT_DH]
        ko_ref[:, kv, :] = k
        vo_ref[:, kv, :] = v
        k = k.astype(BF16)
        v = v.astype(BF16)
        for g in range(group):
            h = kv * group + g
            q = _rms_norm(cur[:, o_q + h * ATT_DH:o_q + (h + 1) * ATT_DH], qg_ref[...]).astype(BF16)
            o = _softmax_pv([(_dot_nt(q, k) * scale, v)])
            mix_ref[:, CONV_CH + h * ATT_DH:CONV_CH + (h + 1) * ATT_DH] = o.astype(mix_ref.dtype)
    y = _dot(mix_ref[...], wo_ref[...])
    y_ref[...] = _post_norm(xc_ref[...], y, mod_ref[2:3, :], lng_ref[...], lnb_ref[...])
    cur_ref[...] = nxt_ref[...]


def _odd_ctx(x, mods, w_in, conv_w, qn_g, kn_g, w_out, ln_g, ln_b):
    head, tail, cur, full = _ctx_layer_specs(sum(ODD_SIZES))
    kv_shape = (SEQ, ATT_KV_HEADS, ATT_DH)
    return pl.pallas_call(
        _odd_ctx_kernel,
        grid=(BATCH + 1,),
        in_specs=head + [full((3, CONV_CH)), full((1, ATT_DH)), full((1, ATT_DH))] + tail,
        out_specs=[cur((SEQ, D_MODEL)), cur(kv_shape), cur(kv_shape)],
        out_shape=[jax.ShapeDtypeStruct((BATCH * SEQ, D_MODEL), F32),
                   jax.ShapeDtypeStruct((BATCH * SEQ, ATT_KV_HEADS, ATT_DH), F32),
                   jax.ShapeDtypeStruct((BATCH * SEQ, ATT_KV_HEADS, ATT_DH), F32)],
        scratch_shapes=[pltpu.VMEM((SEQ, sum(ODD_SIZES)), F32),
                        pltpu.VMEM((SEQ, sum(ODD_SIZES)), F32),
                        pltpu.VMEM((SEQ, D_MODEL), BF16)],
        compiler_params=_params(1),
        name="odd_ctx",
    )(x, x, mods, w_in, conv_w, qn_g, kn_g, w_out, ln_g.reshape(1, D_MODEL), ln_b.reshape(1, D_MODEL))


def _rope(x, cos, sin, even_lane):
    swapped = jnp.where(even_lane, pltpu.roll(x, LANES - 1, 1), pltpu.roll(x, 1, 1))
    return x * cos + swapped * sin


def _odd_lat_kernel(x_ref, mod_ref, wi_ref, ck_ref, cv_ref, w_ref, qg_ref, kg_ref, cos_ref, sin_ref,
                    wo_ref, lng_ref, lnb_ref, y_ref, proj_ref, qb_ref, kb_ref, vb_ref, mix_ref):
    length = x_ref.shape[0]
    h_in = (x_ref[...] * (1.0 + mod_ref[1:2, :]) + mod_ref[0:1, :]).astype(BF16)
    proj_ref[...] = _dot(h_in, wi_ref[...])
    o_bg, o_cg, o_u, o_q, o_k, o_v = _offsets(ODD_SIZES)
    conv = _gated_conv(proj_ref[:, o_bg:o_bg + CONV_CH], proj_ref[:, o_cg:o_cg + CONV_CH],
                       proj_ref[:, o_u:o_u + CONV_CH], w_ref)
    mix_ref[:, 0:CONV_CH] = conv.astype(mix_ref.dtype)
    group = ATT_HEADS // ATT_KV_HEADS
    scale = ATT_DH ** -0.5
    even_lane = lax.broadcasted_iota(jnp.int32, (length, ATT_DH), 1) % 2 == 0
    cos = cos_ref[...]
    sin = sin_ref[...]
    for kv in range(ATT_KV_HEADS):
        sl = slice(kv * ATT_DH, (kv + 1) * ATT_DH)
        k = _rms_norm(proj_ref[:, o_k + kv * ATT_DH:o_k + (kv + 1) * ATT_DH], kg_ref[...])
        kb_ref[0:length, sl] = _rope(k, cos, sin, even_lane).astype(BF16)
        vb_ref[0:length, sl] = proj_ref[:, o_v + kv * ATT_DH:o_v + (kv + 1) * ATT_DH].astype(BF16)
        kb_ref[length:, sl] = ck_ref[:, kv, :].astype(BF16)
        vb_ref[length:, sl] = cv_ref[:, kv, :].astype(BF16)
    for h in range(ATT_HEADS):
        q = _rms_norm(proj_ref[:, o_q + h * ATT_DH:o_q + (h + 1) * ATT_DH], qg_ref[...])
        qb_ref[:, h * ATT_DH:(h + 1) * ATT_DH] = _rope(q, cos, sin, even_lane).astype(BF16)

    def q_block(b, carry):
        rows = pl.ds(pl.multiple_of(b * LAT_QBLK, LAT_QBLK), LAT_QBLK)
        for h in range(ATT_HEADS):
            sl = slice((h // group) * ATT_DH, (h // group + 1) * ATT_DH)
            q = qb_ref[rows, h * ATT_DH:(h + 1) * ATT_DH]
            o = _softmax_pv([(_dot_nt(q, kb_ref[:, sl]) * scale, vb_ref[:, sl])])
            mix_ref[rows, CONV_CH + h * ATT_DH:CONV_CH + (h + 1) * ATT_DH] = o.astype(mix_ref.dtype)
        return carry

    lax.fori_loop(0, length // LAT_QBLK, q_block, 0)
    y = _dot(mix_ref[...], wo_ref[...])
    y_ref[...] = _post_norm(x_ref[...], y, mod_ref[2:3, :], lng_ref[...], lnb_ref[...])


def _odd_lat(x, mods, w_in, cache_k, cache_v, idx, conv_w, qn_g, kn_g, cos, sin, w_out, ln_g, ln_b):
    def full(shape):
        return pl.BlockSpec(shape, lambda i: (0,) * len(shape))

    def const(shape):
        return pl.BlockSpec(shape, lambda i: (0,) * len(shape), pipeline_mode=pl.Buffered(1))
    kv_w = ATT_KV_HEADS * ATT_DH
    cache_spec = pl.BlockSpec((None, None, PAST_LEN, ATT_KV_HEADS, ATT_DH), lambda i: (i, idx, 0, 0, 0))
    return pl.pallas_call(
        _odd_lat_kernel,
        grid=(DEC_BATCH,),
        in_specs=[pl.BlockSpec((DEC_SEQ, D_MODEL), lambda i: (i, 0)),
                  pl.BlockSpec((None, N_MOD, D_MODEL), lambda i: (i, 0, 0)),
                  const((D_MODEL, sum(ODD_SIZES))),
                  cache_spec, cache_spec, full((3, CONV_CH)), full((1, ATT_DH)), full((1, ATT_DH)),
                  const((DEC_SEQ, ATT_DH)), const((DEC_SEQ, ATT_DH)),
                  const((D_MODEL, D_MODEL)), full((1, D_MODEL)), full((1, D_MODEL))],
        out_specs=pl.BlockSpec((DEC_SEQ, D_MODEL), lambda i: (i, 0)),
        out_shape=jax.ShapeDtypeStruct((DEC_BATCH * DEC_SEQ, D_MODEL), F32),
        scratch_shapes=[pltpu.VMEM((DEC_SEQ, sum(ODD_SIZES)), F32),
                        pltpu.VMEM((DEC_SEQ, ATT_HEADS * ATT_DH), BF16),
                        pltpu.VMEM((DEC_SEQ + PAST_LEN, kv_w), BF16),
                        pltpu.VMEM((DEC_SEQ + PAST_LEN, kv_w), BF16),
                        pltpu.VMEM((DEC_SEQ, D_MODEL), BF16)],
        compiler_params=_params(1),
        name="odd_lat",
    )(x, mods, w_in, cache_k, cache_v, conv_w, qn_g, kn_g, cos, sin, w_out,
      ln_g.reshape(1, D_MODEL), ln_b.reshape(1, D_MODEL))


def _rope_tables():
    t = jnp.arange(DEC_SEQ)
    row = (t // GRID_W).astype(F32)
    col = (t % GRID_W).astype(F32)
    n_freq = ATT_DH // 4
    freqs = ROPE_THETA ** (-jnp.arange(n_freq, dtype=F32) / n_freq)
    ang = jnp.concatenate([row[:, None] * freqs, col[:, None] * freqs], axis=-1)
    cos = jnp.repeat(jnp.cos(ang), 2, axis=-1)
    sin = jnp.stack([-jnp.sin(ang), jnp.sin(ang)], axis=-1).reshape(DEC_SEQ, ATT_DH)
    return cos, sin


def _route(scores, sel):
    n_tok = sel.shape[1]
    neg = -jnp.inf
    sub = lax.broadcasted_iota(jnp.int32, (GROUP_SIZE, n_tok), 0).astype(F32)
    blocks = [sel[g * GROUP_SIZE:(g + 1) * GROUP_SIZE, :] for g in range(N_GROUPS)]
    grp = []
    for blk in blocks:
        m1 = jnp.max(blk, axis=0, keepdims=True)
        i1 = jnp.min(jnp.where(blk == m1, sub, float(GROUP_SIZE)), axis=0, keepdims=True)
        m2 = jnp.max(jnp.where(sub == i1, neg, blk), axis=0, keepdims=True)
        grp.append(m1 + m2)
    masked = []
    for g in range(N_GROUPS):
        ahead = jnp.zeros((1, n_tok), F32)
        for o in range(N_GROUPS):
            if o == g:
                continue
            wins = grp[o] >= grp[g] if o < g else grp[o] > grp[g]
            ahead = ahead + jnp.where(wins, 1.0, 0.0)
        masked.append(jnp.where(ahead < float(TOPK_GROUPS), blocks[g], neg))
    val = jnp.concatenate(masked, axis=0)
    row = lax.broadcasted_iota(jnp.int32, (N_EXPERTS, n_tok), 0).astype(F32)
    w = jnp.zeros((N_EXPERTS, n_tok), F32)
    for _ in range(TOP_K):
        m = jnp.max(val, axis=0, keepdims=True)
        idx = jnp.min(jnp.where(val == m, row, float(N_EXPERTS)), axis=0, keepdims=True)
        pick = row == idx
        w = jnp.where(pick, scores, w)
        val = jnp.where(pick, neg, val)
    return w / jnp.sum(w, axis=0, keepdims=True) * ROUTED_SCALE


def _moe_kernel(y_ref, mod_ref, rwt_ref, rb_ref, wg_hbm, wu_hbm, wd_hbm, sg_ref, su_ref, sd_ref,
                lng_ref, lnb_ref, o_ref, xb_ref, gate_ref, acc_ref, wg_buf, wu_buf, wd_buf, sem,
                *, layer, n_steps):
    n_tok = y_ref.shape[0]

    def weight_copies(step, slot):
        experts = pl.ds(step * MOE_EPS, MOE_EPS)
        return (pltpu.make_async_copy(wg_hbm.at[layer, experts], wg_buf.at[slot], sem.at[0, slot]),
                pltpu.make_async_copy(wu_hbm.at[layer, experts], wu_buf.at[slot], sem.at[1, slot]),
                pltpu.make_async_copy(wd_hbm.at[layer, experts], wd_buf.at[slot], sem.at[2, slot]))

    for cp in weight_copies(0, 0):
        cp.start()

    x = y_ref[...] * (1.0 + mod_ref[4:5, :]) + mod_ref[3:4, :]
    x_hi, x_lo = _split(x)
    xb_ref[...] = x_hi
    w_hi, w_lo = _split(rwt_ref[...])
    logits = _dot_nt(w_hi, x_hi) + _dot_nt(w_lo, x_hi) + _dot_nt(w_hi, x_lo)
    scores = 1.0 / (1.0 + jnp.exp(-logits))
    gate_t = _route(scores, scores + rb_ref[...])
    gate_ref[...] = jnp.concatenate([gate_t, jnp.zeros_like(gate_t)], axis=0).T
    sgu = jnp.concatenate([sg_ref[...].astype(BF16), su_ref[...].astype(BF16)], axis=1)
    hs = _dot(x_hi, sgu)
    h_sh = (_silu(hs[:, :EXPERT_HIDDEN]) * hs[:, EXPERT_HIDDEN:]).astype(BF16)
    acc_ref[...] = _dot(h_sh, sd_ref[...].astype(BF16))

    def expert_group(step, slot):
        xb = xb_ref[...]
        g_rot = pltpu.roll(gate_ref[...], jnp.bitwise_and(LANES - step * MOE_EPS, LANES - 1), 1)
        acc = acc_ref[...]
        for p in range(MOE_EPS // 2):
            wg2 = jnp.concatenate([wg_buf[slot, 2 * p].astype(BF16), wg_buf[slot, 2 * p + 1].astype(BF16)], axis=1)
            wu2 = jnp.concatenate([wu_buf[slot, 2 * p].astype(BF16), wu_buf[slot, 2 * p + 1].astype(BF16)], axis=1)
            hg = _dot(xb, wg2)
            hu = _dot(xb, wu2)
            g2 = jnp.concatenate(
                [jnp.broadcast_to(g_rot[:, 2 * p:2 * p + 1], (n_tok, EXPERT_HIDDEN)),
                 jnp.broadcast_to(g_rot[:, 2 * p + 1:2 * p + 2], (n_tok, EXPERT_HIDDEN))], axis=1)
            h = (_silu(hg) * hu * g2).astype(BF16)
            wd2 = wd_buf[slot, 2 * p:2 * p + 2].reshape(2 * EXPERT_HIDDEN, D_MODEL).astype(BF16)
            acc = acc + _dot(h, wd2)
        acc_ref[...] = acc

    def two_groups(k, carry):
        step = 2 * k
        for cp in weight_copies(step, 0):
            cp.wait()
        for cp in weight_copies(step + 1, 1):
            cp.start()
        expert_group(step, 0)

        for cp in weight_copies(step + 1, 1):
            cp.wait()

        @pl.when(step + 2 < n_steps)
        def _():
            for cp in weight_copies(step + 2, 0):
                cp.start()

        expert_group(step + 1, 1)
        return carry

    lax.fori_loop(0, n_steps // 2, two_groups, 0)
    o_ref[...] = _post_norm(y_ref[...], acc_ref[...], mod_ref[5:6, :], lng_ref[...], lnb_ref[...])


def _moe(y, mods, layer, rwt, rb, exp_w_gate, exp_w_up, exp_w_down, sg, su, sd, ln_g, ln_b, seq_len):
    t = y.shape[0]
    tile = MOE_TILE
    n_steps = N_EXPERTS // MOE_EPS
    assert n_steps % 2 == 0
    mod_idx = (lambda i: (i * tile // seq_len, 0, 0)) if mods.shape[0] > 1 else (lambda i: (0, 0, 0))

    def full(shape):
        return pl.BlockSpec(shape, lambda i: (0,) * len(shape))
    hbm = pl.BlockSpec(memory_space=pl.ANY)
    return pl.pallas_call(
        functools.partial(_moe_kernel, layer=layer, n_steps=n_steps),
        grid=(t // tile,),
        in_specs=[pl.BlockSpec((tile, D_MODEL), lambda i: (i, 0)),
                  pl.BlockSpec((None, N_MOD, D_MODEL), mod_idx),
                  full((N_EXPERTS, D_MODEL)),
                  full((N_EXPERTS, 1)),
                  hbm, hbm, hbm,
                  full((D_MODEL, EXPERT_HIDDEN)),
                  full((D_MODEL, EXPERT_HIDDEN)),
                  full((EXPERT_HIDDEN, D_MODEL)),
                  full((1, D_MODEL)),
                  full((1, D_MODEL))],
        out_specs=pl.BlockSpec((tile, D_MODEL), lambda i: (i, 0)),
        out_shape=jax.ShapeDtypeStruct((t, D_MODEL), F32),
        scratch_shapes=[pltpu.VMEM((tile, D_MODEL), BF16),
                        pltpu.VMEM((tile, LANES), F32),
                        pltpu.VMEM((tile, D_MODEL), F32),
                        pltpu.VMEM((2, MOE_EPS, D_MODEL, EXPERT_HIDDEN), F32),
                        pltpu.VMEM((2, MOE_EPS, D_MODEL, EXPERT_HIDDEN), F32),
                        pltpu.VMEM((2, MOE_EPS, EXPERT_HIDDEN, D_MODEL), F32),
                        pltpu.SemaphoreType.DMA((3, 2))],
        compiler_params=_params(1),
        name="moe",
    )(y, mods, rwt, rb, exp_w_gate, exp_w_up, exp_w_down, sg, su, sd,
      ln_g.reshape(1, D_MODEL), ln_b.reshape(1, D_MODEL))


def kernel(x_prompt, x_sample, c, state_ret, cache_na_k, cache_na_v, cache_att_k, cache_att_v, c_ctx,
           w_mod, b_mod, ln_g, ln_b, w_in_even, w_out_even, ret_decay_logit, na_rpb,
           w_in_odd, w_out_odd, conv_w, q_norm_g, k_norm_g, router_w, router_b,
           exp_w_gate, exp_w_up, exp_w_down, sh_w_gate, sh_w_up, sh_w_down):
    yp = x_prompt.reshape(BATCH * SEQ, D_MODEL)
    ys = x_sample.reshape(DEC_BATCH * DEC_SEQ, D_MODEL)

    cond = jnp.zeros((COND_ROWS, D_MODEL), F32).at[0].set(c_ctx).at[1:1 + DEC_BATCH].set(c)
    mods = _adaln(cond, w_mod, b_mod).reshape(DEPTH, COND_ROWS, N_MOD, D_MODEL)

    out = {}
    for l in range(DEPTH):
        i = l // 2
        mp = mods[l, 0:1]
        ms = mods[l, 1:1 + DEC_BATCH]
        if l % 2 == 0:
            w_in = w_in_even[i].astype(BF16)
            w_out = w_out_even[i].astype(BF16)
            decay_rows = jnp.broadcast_to(ret_decay_logit[i].reshape(2 * RET_HEADS, 1), (2 * RET_HEADS, LANES))
            dts = (BF16, BF16, BF16, F32, BF16, BF16, BF16)
            sp = _inproj(ys, ms, w_in, EVEN_SIZES, dts, DEC_SEQ)
            yp, st, na_k, na_v = _even_ctx(yp, mp, w_in, decay_rows, w_out, ln_g[l, 0], ln_b[l, 0])
            mix_s = _even_lat(*sp, state_ret,
                              cache_na_k[:, i].reshape(DEC_BATCH, PAST_LEN, NA_HEADS * NA_DH),
                              cache_na_v[:, i].reshape(DEC_BATCH, PAST_LEN, NA_HEADS * NA_DH),
                              i, decay_rows, na_rpb[i].reshape(-1))
            ys = _outproj(mix_s, ys, ms, w_out, ln_g[l, 0], ln_b[l, 0], DEC_SEQ)
            out.setdefault("st", []).append(st.reshape(BATCH, 2, RET_HEADS, RET_DK, RET_DV))
            out.setdefault("na_k", []).append(na_k.reshape(BATCH, SEQ, NA_HEADS, NA_DH))
            out.setdefault("na_v", []).append(na_v.reshape(BATCH, SEQ, NA_HEADS, NA_DH))
        else:
            w_in = w_in_odd[i].astype(BF16)
            w_out = w_out_odd[i].astype(BF16)
            qg = q_norm_g[i].reshape(1, ATT_DH)
            kg = k_norm_g[i].reshape(1, ATT_DH)
            yp, k_new, att_v = _odd_ctx(yp, mp, w_in, conv_w[i], qg, kg, w_out, ln_g[l, 0], ln_b[l, 0])
            cos, sin = _rope_tables()
            ys = _odd_lat(ys, ms, w_in, cache_att_k, cache_att_v, i,
                          conv_w[i], qg, kg, cos, sin, w_out, ln_g[l, 0], ln_b[l, 0])
            out.setdefault("att_k", []).append(k_new.reshape(BATCH, SEQ, ATT_KV_HEADS, ATT_DH))
            out.setdefault("att_v", []).append(att_v.reshape(BATCH, SEQ, ATT_KV_HEADS, ATT_DH))
        rwt = router_w[l].T
        rb = router_b[l].reshape(N_EXPERTS, 1)
        moe_w = (rwt, rb, exp_w_gate, exp_w_up, exp_w_down, sh_w_gate[l], sh_w_up[l], sh_w_down[l],
                 ln_g[l, 1], ln_b[l, 1])
        yp = _moe(yp, mp, l, *moe_w, SEQ)
        ys = _moe(ys, ms, l, *moe_w, DEC_SEQ)

    return (yp.reshape(BATCH, SEQ, D_MODEL),
            ys.reshape(DEC_BATCH, DEC_SEQ, D_MODEL),
            jnp.stack(out["st"], axis=1),
            jnp.stack(out["na_k"], axis=1),
            jnp.stack(out["na_v"], axis=1),
            jnp.stack(out["att_k"], axis=1),
            jnp.stack(out["att_v"], axis=1))
```

```python
import functools

import jax
import jax.numpy as jnp
from jax import lax
from jax.experimental import pallas as pl
from jax.experimental.pallas import tpu as pltpu

F32 = jnp.float32
BF16 = jnp.bfloat16

D_MODEL = 1024
BATCH = 32
SEQ = 256
DEPTH = 2
DEC_BATCH = 4
DEC_SEQ = 1024
PAST_LEN = 512
GRID_W = 64
GRID_H = DEC_SEQ // GRID_W
MIX_HALF = D_MODEL // 2
RET_HEADS = 4
RET_DV = MIX_HALF // RET_HEADS
RET_DK = RET_DV // 2
NA_HEADS = 4
NA_DH = MIX_HALF // NA_HEADS
NA_WIN_R = 8
NA_WIN_C = 16
CONV_CH = MIX_HALF
ATT_HEADS = 4
ATT_KV_HEADS = 2
ATT_DH = MIX_HALF // ATT_HEADS
ROPE_THETA = 10000.0
N_EXPERTS = 64
EXPERT_HIDDEN = D_MODEL // 8
TOP_K = 8
N_GROUPS = 8
GROUP_SIZE = N_EXPERTS // N_GROUPS
TOPK_GROUPS = 4
ROUTED_SCALE = 2.5
DEEPNORM_ALPHA = (2 * DEPTH) ** 0.25
LN_EPS = 1e-5
RMS_EPS = 1e-6
EVEN_SIZES = (RET_HEADS * RET_DK, RET_HEADS * RET_DK, RET_HEADS * RET_DV, RET_HEADS * RET_DV,
              NA_HEADS * NA_DH, NA_HEADS * NA_DH, NA_HEADS * NA_DH)
ODD_SIZES = (CONV_CH, CONV_CH, CONV_CH, ATT_HEADS * ATT_DH, ATT_KV_HEADS * ATT_DH, ATT_KV_HEADS * ATT_DH)
N_MOD = 6
RPB_ROWS = 2 * NA_WIN_R - 1
RPB_COLS = 2 * NA_WIN_C - 1

LANES = 128
COND_ROWS = 16
V7X_VMEM_LIMIT = 56 * 1024 * 1024

PROJ_TILE = 512
LAT_QBLK = 256
MOE_TILE = 1024
MOE_EPS = 4
ADALN_TN = 1536


def _params(n_axes, vmem_limit=V7X_VMEM_LIMIT):
    return pltpu.CompilerParams(dimension_semantics=("arbitrary",) * n_axes,
                                vmem_limit_bytes=vmem_limit)


def _dot(a, b):
    return jnp.dot(a, b, preferred_element_type=F32)


def _dot_nt(a, b):
    return lax.dot_general(a, b, (((1,), (1,)), ((), ())), preferred_element_type=F32)


def _split(a):
    hi = a.astype(BF16)
    lo = (a - hi.astype(F32)).astype(BF16)
    return hi, lo


def _silu(x):
    return x / (1.0 + jnp.exp(-x))


def _log_sigmoid(x):
    return jnp.minimum(x, 0.0) - jnp.log(1.0 + jnp.exp(-jnp.abs(x)))


def _layer_norm(z, g, b):
    mu = jnp.mean(z, axis=-1, keepdims=True)
    zc = z - mu
    var = jnp.mean(zc * zc, axis=-1, keepdims=True)
    return zc * lax.rsqrt(var + LN_EPS) * g + b


def _post_norm(x, y, gate, g, b):
    return _layer_norm(DEEPNORM_ALPHA * x + (1.0 + gate) * y, g, b)


def _head_norm(x):
    mu = jnp.mean(x, axis=-1, keepdims=True)
    xc = x - mu
    var = jnp.mean(xc * xc, axis=-1, keepdims=True)
    return xc * lax.rsqrt(var + LN_EPS)


def _rms_norm(x, g):
    return x * lax.rsqrt(jnp.mean(x * x, axis=-1, keepdims=True) + RMS_EPS) * g


def _softmax_pv(parts):
    m = None
    for s, _ in parts:
        mi = jnp.max(s, axis=-1, keepdims=True)
        m = mi if m is None else jnp.maximum(m, mi)
    l = None
    o = None
    for s, v in parts:
        p = jnp.exp(s - m)
        li = jnp.sum(p, axis=-1, keepdims=True)
        oi = _dot(p.astype(BF16), v)
        l = li if l is None else l + li
        o = oi if o is None else o + oi
    return o / l


def _adaln_kernel(c_ref, w_ref, b_ref, o_ref):
    a_hi, a_lo = _split(_silu(c_ref[...]))
    w_hi, w_lo = _split(w_ref[...])
    o_ref[...] = _dot(a_hi, w_hi) + _dot(a_lo, w_hi) + _dot(a_hi, w_lo) + b_ref[...]


def _adaln(cond, w_mod, b_mod):
    n = w_mod.shape[-1]
    return pl.pallas_call(
        _adaln_kernel,
        grid=(DEPTH, n // ADALN_TN),
        in_specs=[pl.BlockSpec((COND_ROWS, D_MODEL), lambda l, j: (0, 0)),
                  pl.BlockSpec((None, D_MODEL, ADALN_TN), lambda l, j: (l, 0, j)),
                  pl.BlockSpec((None, 1, ADALN_TN), lambda l, j: (l, 0, j))],
        out_specs=pl.BlockSpec((None, COND_ROWS, ADALN_TN), lambda l, j: (l, 0, j)),
        out_shape=jax.ShapeDtypeStruct((DEPTH, COND_ROWS, n), F32),
        compiler_params=_params(2),
        name="adaln",
    )(cond, w_mod, b_mod.reshape(DEPTH, 1, n))


def _inproj_kernel(x_ref, mod_ref, w_ref, *out_refs, sizes):
    shift = mod_ref[0:1, :]
    scale = mod_ref[1:2, :]
    h = (x_ref[...] * (1.0 + scale) + shift).astype(BF16)
    y = _dot(h, w_ref[...])
    for o_ref, off, n in zip(out_refs, _offsets(sizes), sizes):
        o_ref[...] = y[:, off:off + n].astype(o_ref.dtype)


def _inproj(x, mods, w_bf16, sizes, dtypes, seq_len):
    t = x.shape[0]
    tile = PROJ_TILE
    mod_idx = (lambda i: (i * tile // seq_len, 0, 0)) if mods.shape[0] > 1 else (lambda i: (0, 0, 0))
    return pl.pallas_call(
        functools.partial(_inproj_kernel, sizes=sizes),
        grid=(t // tile,),
        in_specs=[pl.BlockSpec((tile, D_MODEL), lambda i: (i, 0)),
                  pl.BlockSpec((None, N_MOD, D_MODEL), mod_idx),
                  pl.BlockSpec(w_bf16.shape, lambda i: (0, 0))],
        out_specs=[pl.BlockSpec((tile, n), lambda i: (i, 0)) for n in sizes],
        out_shape=[jax.ShapeDtypeStruct((t, n), dt) for n, dt in zip(sizes, dtypes)],
        compiler_params=_params(1),
        name="inproj",
    )(x, mods, w_bf16)


def _outproj_kernel(mix_ref, x_ref, mod_ref, w_ref, g_ref, b_ref, o_ref):
    y = _dot(mix_ref[...], w_ref[...])
    o_ref[...] = _post_norm(x_ref[...], y, mod_ref[2:3, :], g_ref[...], b_ref[...])


def _outproj(mix, x, mods, w_bf16, ln_g, ln_b, seq_len):
    t = x.shape[0]
    tile = PROJ_TILE
    mod_idx = (lambda i: (i * tile // seq_len, 0, 0)) if mods.shape[0] > 1 else (lambda i: (0, 0, 0))
    return pl.pallas_call(
        _outproj_kernel,
        grid=(t // tile,),
        in_specs=[pl.BlockSpec((tile, D_MODEL), lambda i: (i, 0)),
                  pl.BlockSpec((tile, D_MODEL), lambda i: (i, 0)),
                  pl.BlockSpec((None, N_MOD, D_MODEL), mod_idx),
                  pl.BlockSpec((D_MODEL, D_MODEL), lambda i: (0, 0)),
                  pl.BlockSpec((1, D_MODEL), lambda i: (0, 0)),
                  pl.BlockSpec((1, D_MODEL), lambda i: (0, 0))],
        out_specs=pl.BlockSpec((tile, D_MODEL), lambda i: (i, 0)),
        out_shape=jax.ShapeDtypeStruct((t, D_MODEL), F32),
        compiler_params=_params(1),
        name="outproj",
    )(mix, x, mods, w_bf16, ln_g.reshape(1, D_MODEL), ln_b.reshape(1, D_MODEL))


def _decay_matrix(length, lgf, lgb, row0=0, rows=None):
    rows = length if rows is None else rows
    ii = lax.broadcasted_iota(jnp.int32, (rows, length), 0) + row0
    jj = lax.broadcasted_iota(jnp.int32, (rows, length), 1)
    rel = (ii - jj).astype(F32)
    fwd = jnp.where(rel >= 0.0, jnp.exp(lgf * jnp.maximum(rel, 0.0)), 0.0)
    bwd = jnp.where(rel <= 0.0, jnp.exp(lgb * jnp.maximum(-rel, 0.0)), 0.0)
    return fwd + bwd


def _retention_head(q, k, v, lgf, lgb, s0f, s0b, dmat, row0=0):
    rows, length = q.shape[0], k.shape[0]
    s = _dot_nt(q.astype(BF16), k.astype(BF16))
    if dmat.dtype == BF16:
        o = _dot(s.astype(BF16) * dmat, v)
    else:
        o = _dot((s * dmat).astype(BF16), v)
    if s0f is not None:
        pos = (lax.broadcasted_iota(jnp.int32, (rows, 1), 0) + row0).astype(F32)
        qf = q * jnp.exp(lgf * (pos + 1.0))
        qb = q * jnp.exp(lgb * (length - pos))
        o = o + _dot(qf.astype(BF16), s0f.astype(BF16)) + _dot(qb.astype(BF16), s0b.astype(BF16))
    return o


def _offsets(sizes):
    return [sum(sizes[:k]) for k in range(len(sizes))]


def _project(x_ref, mod_ref, wi_ref, dst_ref):
    h = (x_ref[...] * (1.0 + mod_ref[1:2, :]) + mod_ref[0:1, :]).astype(BF16)
    dst_ref[...] = _dot(h, wi_ref[...])


def _cast_rows(src_ref, dst_ref):
    def body(r, carry):
        rows = pl.ds(pl.multiple_of(r * LANES, LANES), LANES)
        dst_ref[rows, :] = src_ref[rows, :].astype(BF16)
        return carry
    lax.fori_loop(0, src_ref.shape[0] // LANES, body, 0)


def _project_next(xn_ref, xc_ref, mod_ref, wif_ref, wof_ref, wi_ref, wo_ref, nxt_ref, cur_ref):
    @pl.when(pl.program_id(0) == 0)
    def _():
        _cast_rows(wif_ref, wi_ref)
        _cast_rows(wof_ref, wo_ref)
        _project(xc_ref, mod_ref, wi_ref, cur_ref)

    _project(xn_ref, mod_ref, wi_ref, nxt_ref)
    return cur_ref


def _even_ctx_kernel(xn_ref, xc_ref, mod_ref, wif_ref, dl_ref, wof_ref, lng_ref, lnb_ref,
                     y_ref, st_ref, nak_ref, nav_ref, wi_ref, wo_ref,
                     nxt_ref, cur_ref, mix_ref, dmat_ref, kdec_ref):
    length = xc_ref.shape[0]

    @pl.when(pl.program_id(0) == 0)
    def _():
        lg = _log_sigmoid(dl_ref[...])
        pos = lax.broadcasted_iota(jnp.int32, (length, 1), 0).astype(F32)
        for h in range(RET_HEADS):
            lgf = lg[h:h + 1, 0:1]
            lgb = lg[RET_HEADS + h:RET_HEADS + h + 1, 0:1]
            dmat_ref[h] = _decay_matrix(length, lgf, lgb)
            kdec_ref[h] = jnp.broadcast_to(jnp.exp(lgf * (length - 1.0 - pos)), (length, RET_DK))
            kdec_ref[RET_HEADS + h] = jnp.broadcast_to(jnp.exp(lgb * pos), (length, RET_DK))

    cur = _project_next(xn_ref, xc_ref, mod_ref, wif_ref, wof_ref, wi_ref, wo_ref, nxt_ref, cur_ref)
    o_qr, o_kr, o_vr, o_gr, o_qn, o_kn, o_vn = _offsets(EVEN_SIZES)
    eye = jnp.where(lax.broadcasted_iota(jnp.int32, (2 * RET_DK, 2 * RET_DK), 0)
                    == lax.broadcasted_iota(jnp.int32, (2 * RET_DK, 2 * RET_DK), 1), 1.0, 0.0).astype(BF16)
    for h in range(RET_HEADS):
        q = cur[:, o_qr + h * RET_DK:o_qr + (h + 1) * RET_DK]
        k = cur[:, o_kr + h * RET_DK:o_kr + (h + 1) * RET_DK] * (RET_DK ** -0.5)
        v = cur[:, o_vr + h * RET_DV:o_vr + (h + 1) * RET_DV].astype(BF16)
        o = _retention_head(q, k, v, None, None, None, None, dmat=dmat_ref[h])
        k_dec = jnp.concatenate([k * kdec_ref[h], k * kdec_ref[RET_HEADS + h]], axis=1).astype(BF16)
        states = _dot(_dot_nt(eye, k_dec).astype(BF16), v)
        st_ref[h] = states[:RET_DK]
        st_ref[RET_HEADS + h] = states[RET_DK:]
        g = cur[:, o_gr + h * RET_DV:o_gr + (h + 1) * RET_DV]
        mix_ref[:, h * RET_DV:(h + 1) * RET_DV] = (_head_norm(o) * _silu(g)).astype(mix_ref.dtype)
    base = RET_HEADS * RET_DV
    for h in range(NA_HEADS):
        q = cur[:, o_qn + h * NA_DH:o_qn + (h + 1) * NA_DH].astype(BF16)
        k = cur[:, o_kn + h * NA_DH:o_kn + (h + 1) * NA_DH]
        v = cur[:, o_vn + h * NA_DH:o_vn + (h + 1) * NA_DH]
        nak_ref[:, h, :] = k
        nav_ref[:, h, :] = v
        s = _dot_nt(q, k.astype(BF16)) * (NA_DH ** -0.5)
        mix_ref[:, base + h * NA_DH:base + (h + 1) * NA_DH] = _softmax_pv([(s, v.astype(BF16))]).astype(mix_ref.dtype)
    y = _dot(mix_ref[...], wo_ref[...])
    y_ref[...] = _post_norm(xc_ref[...], y, mod_ref[2:3, :], lng_ref[...], lnb_ref[...])
    cur_ref[...] = nxt_ref[...]


def _ctx_layer_specs(n_proj, idx):
    def nxt(n):
        return pl.BlockSpec((SEQ, n), lambda i: (jnp.minimum(i + 1, BATCH - 1), 0))

    def cur(shape):
        return pl.BlockSpec(shape, lambda i: (i,) + (0,) * (len(shape) - 1))

    def full(shape):
        return pl.BlockSpec(shape, lambda i: (0,) * len(shape))

    def weight(n):
        return pl.BlockSpec((None, D_MODEL, n), lambda i: (idx, 0, 0), pipeline_mode=pl.Buffered(1))
    head = [nxt(D_MODEL), cur((SEQ, D_MODEL)), full((None, N_MOD, D_MODEL)), weight(n_proj)]
    tail = [weight(D_MODEL), full((1, D_MODEL)), full((1, D_MODEL))]
    w_specs = [full((D_MODEL, n_proj)), full((D_MODEL, D_MODEL))]
    w_shapes = [jax.ShapeDtypeStruct((D_MODEL, n_proj), BF16), jax.ShapeDtypeStruct((D_MODEL, D_MODEL), BF16)]
    return head, tail, cur, full, w_specs, w_shapes


def _even_ctx(x, mods, w_in, w_out, idx, decay_rows, ln_g, ln_b):
    head, tail, cur, full, w_specs, w_shapes = _ctx_layer_specs(sum(EVEN_SIZES), idx)
    return pl.pallas_call(
        _even_ctx_kernel,
        grid=(BATCH,),
        in_specs=head + [full((2 * RET_HEADS, LANES))] + tail,
        out_specs=[cur((SEQ, D_MODEL)),
                   cur((None, 2 * RET_HEADS, RET_DK, RET_DV)),
                   cur((SEQ, NA_HEADS, NA_DH)), cur((SEQ, NA_HEADS, NA_DH))] + w_specs,
        out_shape=[jax.ShapeDtypeStruct((BATCH * SEQ, D_MODEL), F32),
                   jax.ShapeDtypeStruct((BATCH, 2 * RET_HEADS, RET_DK, RET_DV), F32),
                   jax.ShapeDtypeStruct((BATCH * SEQ, NA_HEADS, NA_DH), F32),
                   jax.ShapeDtypeStruct((BATCH * SEQ, NA_HEADS, NA_DH), F32)] + w_shapes,
        scratch_shapes=[pltpu.VMEM((SEQ, sum(EVEN_SIZES)), F32),
                        pltpu.VMEM((SEQ, sum(EVEN_SIZES)), F32),
                        pltpu.VMEM((SEQ, D_MODEL), BF16),
                        pltpu.VMEM((RET_HEADS, SEQ, SEQ), F32),
                        pltpu.VMEM((2 * RET_HEADS, SEQ, RET_DK), F32)],
        compiler_params=_params(1),
        name="even_ctx",
    )(x, x, mods, w_in, decay_rows, w_out, ln_g.reshape(1, D_MODEL), ln_b.reshape(1, D_MODEL))


def _build_rpb_tiles(rpb_ref, tile_ref):
    qc = lax.broadcasted_iota(jnp.int32, (GRID_W, LANES), 0)
    kc = lax.broadcasted_iota(jnp.int32, (GRID_W, LANES), 1) % GRID_W
    diff = kc - qc + (NA_WIN_C - 1)
    start = jnp.clip(qc - NA_WIN_C // 2, 0, GRID_W - NA_WIN_C)
    win = (kc >= start) & (kc < start + NA_WIN_C)

    def body(idx, carry):
        t = jnp.zeros((GRID_W, LANES), F32)
        for d in range(RPB_COLS):
            t = jnp.where(diff == d, rpb_ref[idx * RPB_COLS + d], t)
        tile_ref[idx] = jnp.where(win, t, -jnp.inf)
        return carry

    lax.fori_loop(0, NA_HEADS * RPB_ROWS, body, 0)


RET_QBLK = 512
NA_Q_ROWS = 4
NA_KEY_ROWS = 12


def _band_start(qr):
    rows_w = min(NA_WIN_R, GRID_H)
    return min(max(qr - rows_w // 2, 0), GRID_H - rows_w)


def _key_window(qrow0):
    w0 = min(_band_start(qrow0), GRID_H - NA_KEY_ROWS)
    assert _band_start(qrow0 + NA_Q_ROWS - 1) + min(NA_WIN_R, GRID_H) <= w0 + NA_KEY_ROWS
    return w0


def _build_bias(tile_ref, bias_ref, head, qrow0):
    left = lax.broadcasted_iota(jnp.int32, (GRID_W, LANES), 1) < GRID_W
    neg = jnp.full((GRID_W, LANES), -jnp.inf, F32)
    rows_w = min(NA_WIN_R, GRID_H)
    w0 = _key_window(qrow0)
    for dq in range(NA_Q_ROWS):
        qr = qrow0 + dq
        rs = _band_start(qr)

        def tile(kr):
            if rs <= kr < rs + rows_w:
                return tile_ref[head * RPB_ROWS + kr - qr + NA_WIN_R - 1]
            return neg

        pieces = [jnp.where(left, tile(w0 + 2 * a), tile(w0 + 2 * a + 1)) for a in range(NA_KEY_ROWS // 2)]
        bias_ref[dq * GRID_W:(dq + 1) * GRID_W, :] = jnp.concatenate(pieces, axis=1)


def _even_lat_kernel(qr_ref, kr_ref, vr_ref, gr_ref, qn_ref, kn_ref, vn_ref, s0_ref, ck_ref, cv_ref,
                     dl_ref, rpb_ref, mix_ref, tile_ref, bias_ref, dmat_ref):
    length = qr_ref.shape[0]
    lg = _log_sigmoid(dl_ref[...])

    @pl.when(pl.program_id(0) == 0)
    def _():
        _build_rpb_tiles(rpb_ref, tile_ref)
        for h in range(RET_HEADS):
            lgf = lg[h:h + 1, 0:1]
            lgb = lg[RET_HEADS + h:RET_HEADS + h + 1, 0:1]

            def rows_body(r, carry, h=h, lgf=lgf, lgb=lgb):
                r0 = pl.multiple_of(r * LANES, LANES)
                dmat_ref[h, pl.ds(r0, LANES), :] = _decay_matrix(length, lgf, lgb, r0, LANES).astype(BF16)
                return carry

            lax.fori_loop(0, length // LANES, rows_body, 0)

    for h in range(RET_HEADS):
        lgf = lg[h:h + 1, 0:1]
        lgb = lg[RET_HEADS + h:RET_HEADS + h + 1, 0:1]
        k = kr_ref[:, h * RET_DK:(h + 1) * RET_DK].astype(F32) * (RET_DK ** -0.5)
        v = vr_ref[:, h * RET_DV:(h + 1) * RET_DV]
        for r0 in range(0, length, RET_QBLK):
            rows = slice(r0, r0 + RET_QBLK)
            q = qr_ref[rows, h * RET_DK:(h + 1) * RET_DK].astype(F32)
            o = _retention_head(q, k, v, lgf, lgb, s0_ref[0, h], s0_ref[1, h], dmat_ref[h, rows, :], r0)
            g = gr_ref[rows, h * RET_DV:(h + 1) * RET_DV]
            mix_ref[rows, h * RET_DV:(h + 1) * RET_DV] = (_head_norm(o) * _silu(g)).astype(mix_ref.dtype)
    base = RET_HEADS * RET_DV
    scale = NA_DH ** -0.5
    for h in range(NA_HEADS):
        sl = slice(h * NA_DH, (h + 1) * NA_DH)
        kc = ck_ref[:, h, :].astype(BF16)
        vc = cv_ref[:, h, :].astype(BF16)
        for qrow0 in range(0, GRID_H, NA_Q_ROWS):
            _build_bias(tile_ref, bias_ref, h, qrow0)
            rows = slice(qrow0 * GRID_W, (qrow0 + NA_Q_ROWS) * GRID_W)
            k0 = _key_window(qrow0) * GRID_W
            keys = slice(k0, k0 + NA_KEY_ROWS * GRID_W)
            q = qn_ref[rows, sl]
            s_band = _dot_nt(q, kn_ref[keys, sl]) * scale + bias_ref[...]
            s_ctx = _dot_nt(q, kc) * scale
            o = _softmax_pv([(s_band, vn_ref[keys, sl]), (s_ctx, vc)])
            mix_ref[rows, base + h * NA_DH:base + (h + 1) * NA_DH] = o.astype(mix_ref.dtype)


def _even_lat(qr, kr, vr, gr, qn, kn, vn, state, cache_k, cache_v, idx, decay_rows, rpb_flat):
    def seq_spec(n):
        return pl.BlockSpec((DEC_SEQ, n), lambda i: (i, 0))
    cache_spec = pl.BlockSpec((None, None, PAST_LEN, NA_HEADS, NA_DH), lambda i: (i, idx, 0, 0, 0),
                              pipeline_mode=pl.Buffered(1))
    return pl.pallas_call(
        _even_lat_kernel,
        grid=(DEC_BATCH,),
        in_specs=[seq_spec(n) for n in EVEN_SIZES]
        + [pl.BlockSpec((None, None, 2, RET_HEADS, RET_DK, RET_DV), lambda i: (i, idx, 0, 0, 0, 0)),
           cache_spec, cache_spec,
           pl.BlockSpec((2 * RET_HEADS, LANES), lambda i: (0, 0)),
           pl.BlockSpec(memory_space=pltpu.SMEM)],
        out_specs=seq_spec(D_MODEL),
        out_shape=jax.ShapeDtypeStruct((DEC_BATCH * DEC_SEQ, D_MODEL), BF16),
        scratch_shapes=[pltpu.VMEM((NA_HEADS * RPB_ROWS, GRID_W, LANES), F32),
                        pltpu.VMEM((NA_Q_ROWS * GRID_W, NA_KEY_ROWS * GRID_W), F32),
                        pltpu.VMEM((RET_HEADS, DEC_SEQ, DEC_SEQ), BF16)],
        compiler_params=_params(1),
        name="even_lat",
    )(qr, kr, vr, gr, qn, kn, vn, state, cache_k, cache_v, decay_rows, rpb_flat)


def _gated_conv(bg, cg, u, w_ref):
    length, ch = u.shape
    z = cg * u
    row = lax.broadcasted_iota(jnp.int32, (length, ch), 0)
    z_prev = jnp.where(row == 0, 0.0, pltpu.roll(z, 1, 0))
    z_next = jnp.where(row == length - 1, 0.0, pltpu.roll(z, length - 1, 0))
    return bg * (z_prev * w_ref[0:1, :] + z * w_ref[1:2, :] + z_next * w_ref[2:3, :])


def _odd_ctx_kernel(xn_ref, xc_ref, mod_ref, wif_ref, w_ref, qg_ref, kg_ref, wof_ref, lng_ref, lnb_ref,
                    y_ref, ko_ref, vo_ref, wi_ref, wo_ref, nxt_ref, cur_ref, mix_ref):
    cur = _project_next(xn_ref, xc_ref, mod_ref, wif_ref, wof_ref, wi_ref, wo_ref, nxt_ref, cur_ref)
    o_bg, o_cg, o_u, o_q, o_k, o_v = _offsets(ODD_SIZES)
    conv = _gated_conv(cur[:, o_bg:o_bg + CONV_CH], cur[:, o_cg:o_cg + CONV_CH], cur[:, o_u:o_u + CONV_CH], w_ref)
    mix_ref[:, 0:CONV_CH] = conv.astype(mix_ref.dtype)
    group = ATT_HEADS // ATT_KV_HEADS
    scale = ATT_DH ** -0.5
    for kv in range(ATT_KV_HEADS):
        k = _rms_norm(cur[:, o_k + kv * ATT_DH:o_k + (kv + 1) * ATT_DH], kg_ref[...])
        v = cur[:, o_v + kv * ATT_DH:o_v + (kv + 1) * ATT_DH]
        ko_ref[:, kv, :] = k
        vo_ref[:, kv, :] = v
        k = k.astype(BF16)
        v = v.astype(BF16)
        for g in range(group):
            h = kv * group + g
            q = _rms_norm(cur[:, o_q + h * ATT_DH:o_q + (h + 1) * ATT_DH], qg_ref[...]).astype(BF16)
            o = _softmax_pv([(_dot_nt(q, k) * scale, v)])
            mix_ref[:, CONV_CH + h * ATT_DH:CONV_CH + (h + 1) * ATT_DH] = o.astype(mix_ref.dtype)
    y = _dot(mix_ref[...], wo_ref[...])
    y_ref[...] = _post_norm(xc_ref[...], y, mod_ref[2:3, :], lng_ref[...], lnb_ref[...])
    cur_ref[...] = nxt_ref[...]


def _odd_ctx(x, mods, w_in, w_out, idx, conv_w, qn_g, kn_g, ln_g, ln_b):
    head, tail, cur, full, w_specs, w_shapes = _ctx_layer_specs(sum(ODD_SIZES), idx)
    kv_shape = (SEQ, ATT_KV_HEADS, ATT_DH)
    return pl.pallas_call(
        _odd_ctx_kernel,
        grid=(BATCH,),
        in_specs=head + [full((3, CONV_CH)), full((1, ATT_DH)), full((1, ATT_DH))] + tail,
        out_specs=[cur((SEQ, D_MODEL)), cur(kv_shape), cur(kv_shape)] + w_specs,
        out_shape=[jax.ShapeDtypeStruct((BATCH * SEQ, D_MODEL), F32),
                   jax.ShapeDtypeStruct((BATCH * SEQ, ATT_KV_HEADS, ATT_DH), F32),
                   jax.ShapeDtypeStruct((BATCH * SEQ, ATT_KV_HEADS, ATT_DH), F32)] + w_shapes,
        scratch_shapes=[pltpu.VMEM((SEQ, sum(ODD_SIZES)), F32),
                        pltpu.VMEM((SEQ, sum(ODD_SIZES)), F32),
                        pltpu.VMEM((SEQ, D_MODEL), BF16)],
        compiler_params=_params(1),
        name="odd_ctx",
    )(x, x, mods, w_in, conv_w, qn_g, kn_g, w_out, ln_g.reshape(1, D_MODEL), ln_b.reshape(1, D_MODEL))


def _rope(x, cos, sin, even_lane):
    swapped = jnp.where(even_lane, pltpu.roll(x, LANES - 1, 1), pltpu.roll(x, 1, 1))
    return x * cos + swapped * sin


def _odd_lat_kernel(x_ref, mod_ref, wi_ref, ck_ref, cv_ref, w_ref, qg_ref, kg_ref, cos_ref, sin_ref,
                    wo_ref, lng_ref, lnb_ref, y_ref, proj_ref, qb_ref, kb_ref, vb_ref, mix_ref):
    length = x_ref.shape[0]
    h_in = (x_ref[...] * (1.0 + mod_ref[1:2, :]) + mod_ref[0:1, :]).astype(BF16)
    proj_ref[...] = _dot(h_in, wi_ref[...])
    o_bg, o_cg, o_u, o_q, o_k, o_v = _offsets(ODD_SIZES)
    conv = _gated_conv(proj_ref[:, o_bg:o_bg + CONV_CH], proj_ref[:, o_cg:o_cg + CONV_CH],
                       proj_ref[:, o_u:o_u + CONV_CH], w_ref)
    mix_ref[:, 0:CONV_CH] = conv.astype(mix_ref.dtype)
    group = ATT_HEADS // ATT_KV_HEADS
    scale = ATT_DH ** -0.5
    even_lane = lax.broadcasted_iota(jnp.int32, (length, ATT_DH), 1) % 2 == 0
    cos = cos_ref[...]
    sin = sin_ref[...]
    for kv in range(ATT_KV_HEADS):
        sl = slice(kv * ATT_DH, (kv + 1) * ATT_DH)
        k = _rms_norm(proj_ref[:, o_k + kv * ATT_DH:o_k + (kv + 1) * ATT_DH], kg_ref[...])
        kb_ref[0:length, sl] = _rope(k, cos, sin, even_lane).astype(BF16)
        vb_ref[0:length, sl] = proj_ref[:, o_v + kv * ATT_DH:o_v + (kv + 1) * ATT_DH].astype(BF16)
        kb_ref[length:, sl] = ck_ref[:, kv, :].astype(BF16)
        vb_ref[length:, sl] = cv_ref[:, kv, :].astype(BF16)
    for h in range(ATT_HEADS):
        q = _rms_norm(proj_ref[:, o_q + h * ATT_DH:o_q + (h + 1) * ATT_DH], qg_ref[...])
        qb_ref[:, h * ATT_DH:(h + 1) * ATT_DH] = _rope(q, cos, sin, even_lane).astype(BF16)

    def q_block(b, carry):
        rows = pl.ds(pl.multiple_of(b * LAT_QBLK, LAT_QBLK), LAT_QBLK)
        for h in range(ATT_HEADS):
            sl = slice((h // group) * ATT_DH, (h // group + 1) * ATT_DH)
            q = qb_ref[rows, h * ATT_DH:(h + 1) * ATT_DH]
            o = _softmax_pv([(_dot_nt(q, kb_ref[:, sl]) * scale, vb_ref[:, sl])])
            mix_ref[rows, CONV_CH + h * ATT_DH:CONV_CH + (h + 1) * ATT_DH] = o.astype(mix_ref.dtype)
        return carry

    lax.fori_loop(0, length // LAT_QBLK, q_block, 0)
    y = _dot(mix_ref[...], wo_ref[...])
    y_ref[...] = _post_norm(x_ref[...], y, mod_ref[2:3, :], lng_ref[...], lnb_ref[...])


def _odd_lat(x, mods, w_in, cache_k, cache_v, idx, conv_w, qn_g, kn_g, cos, sin, w_out, ln_g, ln_b):
    def full(shape):
        return pl.BlockSpec(shape, lambda i: (0,) * len(shape))

    def const(shape):
        return pl.BlockSpec(shape, lambda i: (0,) * len(shape), pipeline_mode=pl.Buffered(1))
    kv_w = ATT_KV_HEADS * ATT_DH
    cache_spec = pl.BlockSpec((None, None, PAST_LEN, ATT_KV_HEADS, ATT_DH), lambda i: (i, idx, 0, 0, 0))
    return pl.pallas_call(
        _odd_lat_kernel,
        grid=(DEC_BATCH,),
        in_specs=[pl.BlockSpec((DEC_SEQ, D_MODEL), lambda i: (i, 0)),
                  pl.BlockSpec((None, N_MOD, D_MODEL), lambda i: (i, 0, 0)),
                  const((D_MODEL, sum(ODD_SIZES))),
                  cache_spec, cache_spec, full((3, CONV_CH)), full((1, ATT_DH)), full((1, ATT_DH)),
                  const((DEC_SEQ, ATT_DH)), const((DEC_SEQ, ATT_DH)),
                  const((D_MODEL, D_MODEL)), full((1, D_MODEL)), full((1, D_MODEL))],
        out_specs=pl.BlockSpec((DEC_SEQ, D_MODEL), lambda i: (i, 0)),
        out_shape=jax.ShapeDtypeStruct((DEC_BATCH * DEC_SEQ, D_MODEL), F32),
        scratch_shapes=[pltpu.VMEM((DEC_SEQ, sum(ODD_SIZES)), F32),
                        pltpu.VMEM((DEC_SEQ, ATT_HEADS * ATT_DH), BF16),
                        pltpu.VMEM((DEC_SEQ + PAST_LEN, kv_w), BF16),
                        pltpu.VMEM((DEC_SEQ + PAST_LEN, kv_w), BF16),
                        pltpu.VMEM((DEC_SEQ, D_MODEL), BF16)],
        compiler_params=_params(1),
        name="odd_lat",
    )(x, mods, w_in, cache_k, cache_v, conv_w, qn_g, kn_g, cos, sin, w_out,
      ln_g.reshape(1, D_MODEL), ln_b.reshape(1, D_MODEL))


def _rope_tables():
    t = jnp.arange(DEC_SEQ)
    row = (t // GRID_W).astype(F32)
    col = (t % GRID_W).astype(F32)
    n_freq = ATT_DH // 4
    freqs = ROPE_THETA ** (-jnp.arange(n_freq, dtype=F32) / n_freq)
    ang = jnp.concatenate([row[:, None] * freqs, col[:, None] * freqs], axis=-1)
    cos = jnp.repeat(jnp.cos(ang), 2, axis=-1)
    sin = jnp.stack([-jnp.sin(ang), jnp.sin(ang)], axis=-1).reshape(DEC_SEQ, ATT_DH)
    return cos, sin


def _route(scores, sel):
    n_tok = sel.shape[1]
    neg = -jnp.inf
    sub = lax.broadcasted_iota(jnp.int32, (GROUP_SIZE, n_tok), 0).astype(F32)
    blocks = [sel[g * GROUP_SIZE:(g + 1) * GROUP_SIZE, :] for g in range(N_GROUPS)]
    grp = []
    for blk in blocks:
        m1 = jnp.max(blk, axis=0, keepdims=True)
        i1 = jnp.min(jnp.where(blk == m1, sub, float(GROUP_SIZE)), axis=0, keepdims=True)
        m2 = jnp.max(jnp.where(sub == i1, neg, blk), axis=0, keepdims=True)
        grp.append(m1 + m2)
    masked = []
    for g in range(N_GROUPS):
        ahead = jnp.zeros((1, n_tok), F32)
        for o in range(N_GROUPS):
            if o == g:
                continue
            wins = grp[o] >= grp[g] if o < g else grp[o] > grp[g]
            ahead = ahead + jnp.where(wins, 1.0, 0.0)
        masked.append(jnp.where(ahead < float(TOPK_GROUPS), blocks[g], neg))
    val = jnp.concatenate(masked, axis=0)
    row = lax.broadcasted_iota(jnp.int32, (N_EXPERTS, n_tok), 0).astype(F32)
    w = jnp.zeros((N_EXPERTS, n_tok), F32)
    for _ in range(TOP_K):
        m = jnp.max(val, axis=0, keepdims=True)
        idx = jnp.min(jnp.where(val == m, row, float(N_EXPERTS)), axis=0, keepdims=True)
        pick = row == idx
        w = jnp.where(pick, scores, w)
        val = jnp.where(pick, neg, val)
    return w / jnp.sum(w, axis=0, keepdims=True) * ROUTED_SCALE


def _moe_kernel(y_ref, mod_ref, rwt_ref, rb_ref, wg_hbm, wu_hbm, wd_hbm, sg_ref, su_ref, sd_ref,
                lng_ref, lnb_ref, o_ref, xb_ref, gate_ref, acc_ref, wg_buf, wu_buf, wd_buf, sem,
                *, layer, n_steps):
    n_tok = y_ref.shape[0]

    def weight_copies(step, slot):
        experts = pl.ds(step * MOE_EPS, MOE_EPS)
        return (pltpu.make_async_copy(wg_hbm.at[layer, experts], wg_buf.at[slot], sem.at[0, slot]),
                pltpu.make_async_copy(wu_hbm.at[layer, experts], wu_buf.at[slot], sem.at[1, slot]),
                pltpu.make_async_copy(wd_hbm.at[layer, experts], wd_buf.at[slot], sem.at[2, slot]))

    for cp in weight_copies(0, 0):
        cp.start()

    x = y_ref[...] * (1.0 + mod_ref[4:5, :]) + mod_ref[3:4, :]
    x_hi, x_lo = _split(x)
    xb_ref[...] = x_hi
    w_hi, w_lo = _split(rwt_ref[...])
    logits = _dot_nt(w_hi, x_hi) + _dot_nt(w_lo, x_hi) + _dot_nt(w_hi, x_lo)
    scores = 1.0 / (1.0 + jnp.exp(-logits))
    gate_t = _route(scores, scores + rb_ref[...])
    gate_ref[...] = jnp.concatenate([gate_t, jnp.zeros_like(gate_t)], axis=0).T
    sgu = jnp.concatenate([sg_ref[...].astype(BF16), su_ref[...].astype(BF16)], axis=1)
    hs = _dot(x_hi, sgu)
    h_sh = (_silu(hs[:, :EXPERT_HIDDEN]) * hs[:, EXPERT_HIDDEN:]).astype(BF16)
    acc_ref[...] = _dot(h_sh, sd_ref[...].astype(BF16))

    def expert_group(step, slot):
        xb = xb_ref[...]
        g_rot = pltpu.roll(gate_ref[...], jnp.bitwise_and(LANES - step * MOE_EPS, LANES - 1), 1)
        acc = acc_ref[...]
        for p in range(MOE_EPS // 2):
            wg2 = jnp.concatenate([wg_buf[slot, 2 * p].astype(BF16), wg_buf[slot, 2 * p + 1].astype(BF16)], axis=1)
            wu2 = jnp.concatenate([wu_buf[slot, 2 * p].astype(BF16), wu_buf[slot, 2 * p + 1].astype(BF16)], axis=1)
            hg = _dot(xb, wg2)
            hu = _dot(xb, wu2)
            g2 = jnp.concatenate(
                [jnp.broadcast_to(g_rot[:, 2 * p:2 * p + 1], (n_tok, EXPERT_HIDDEN)),
                 jnp.broadcast_to(g_rot[:, 2 * p + 1:2 * p + 2], (n_tok, EXPERT_HIDDEN))], axis=1)
            h = (_silu(hg) * hu * g2).astype(BF16)
            wd2 = wd_buf[slot, 2 * p:2 * p + 2].reshape(2 * EXPERT_HIDDEN, D_MODEL).astype(BF16)
            acc = acc + _dot(h, wd2)
        acc_ref[...] = acc

    def two_groups(k, carry):
        step = 2 * k
        for cp in weight_copies(step, 0):
            cp.wait()
        for cp in weight_copies(step + 1, 1):
            cp.start()
        expert_group(step, 0)

        for cp in weight_copies(step + 1, 1):
            cp.wait()

        @pl.when(step + 2 < n_steps)
        def _():
            for cp in weight_copies(step + 2, 0):
                cp.start()

        expert_group(step + 1, 1)
        return carry

    lax.fori_loop(0, n_steps // 2, two_groups, 0)
    o_ref[...] = _post_norm(y_ref[...], acc_ref[...], mod_ref[5:6, :], lng_ref[...], lnb_ref[...])


def _moe(y, mods, layer, rwt, rb, exp_w_gate, exp_w_up, exp_w_down, sg, su, sd, ln_g, ln_b, seq_len):
    t = y.shape[0]
    tile = MOE_TILE
    n_steps = N_EXPERTS // MOE_EPS
    assert n_steps % 2 == 0
    mod_idx = (lambda i: (i * tile // seq_len, 0, 0)) if mods.shape[0] > 1 else (lambda i: (0, 0, 0))

    def full(shape):
        return pl.BlockSpec(shape, lambda i: (0,) * len(shape))
    hbm = pl.BlockSpec(memory_space=pl.ANY)
    return pl.pallas_call(
        functools.partial(_moe_kernel, layer=layer, n_steps=n_steps),
        grid=(t // tile,),
        in_specs=[pl.BlockSpec((tile, D_MODEL), lambda i: (i, 0)),
                  pl.BlockSpec((None, N_MOD, D_MODEL), mod_idx),
                  full((N_EXPERTS, D_MODEL)),
                  full((N_EXPERTS, 1)),
                  hbm, hbm, hbm,
                  full((D_MODEL, EXPERT_HIDDEN)),
                  full((D_MODEL, EXPERT_HIDDEN)),
                  full((EXPERT_HIDDEN, D_MODEL)),
                  full((1, D_MODEL)),
                  full((1, D_MODEL))],
        out_specs=pl.BlockSpec((tile, D_MODEL), lambda i: (i, 0)),
        out_shape=jax.ShapeDtypeStruct((t, D_MODEL), F32),
        scratch_shapes=[pltpu.VMEM((tile, D_MODEL), BF16),
                        pltpu.VMEM((tile, LANES), F32),
                        pltpu.VMEM((tile, D_MODEL), F32),
                        pltpu.VMEM((2, MOE_EPS, D_MODEL, EXPERT_HIDDEN), F32),
                        pltpu.VMEM((2, MOE_EPS, D_MODEL, EXPERT_HIDDEN), F32),
                        pltpu.VMEM((2, MOE_EPS, EXPERT_HIDDEN, D_MODEL), F32),
                        pltpu.SemaphoreType.DMA((3, 2))],
        compiler_params=_params(1),
        name="moe",
    )(y, mods, rwt, rb, exp_w_gate, exp_w_up, exp_w_down, sg, su, sd,
      ln_g.reshape(1, D_MODEL), ln_b.reshape(1, D_MODEL))


def kernel(x_prompt, x_sample, c, state_ret, cache_na_k, cache_na_v, cache_att_k, cache_att_v, c_ctx,
           w_mod, b_mod, ln_g, ln_b, w_in_even, w_out_even, ret_decay_logit, na_rpb,
           w_in_odd, w_out_odd, conv_w, q_norm_g, k_norm_g, router_w, router_b,
           exp_w_gate, exp_w_up, exp_w_down, sh_w_gate, sh_w_up, sh_w_down):
    yp = x_prompt.reshape(BATCH * SEQ, D_MODEL)
    ys = x_sample.reshape(DEC_BATCH * DEC_SEQ, D_MODEL)

    cond = jnp.zeros((COND_ROWS, D_MODEL), F32).at[0].set(c_ctx).at[1:1 + DEC_BATCH].set(c)
    mods = _adaln(cond, w_mod, b_mod).reshape(DEPTH, COND_ROWS, N_MOD, D_MODEL)

    out = {}
    for l in range(DEPTH):
        i = l // 2
        mp = mods[l, 0:1]
        ms = mods[l, 1:1 + DEC_BATCH]
        if l % 2 == 0:
            decay_rows = jnp.broadcast_to(ret_decay_logit[i].reshape(2 * RET_HEADS, 1), (2 * RET_HEADS, LANES))
            dts = (BF16, BF16, BF16, F32, BF16, BF16, BF16)
            yp, st, na_k, na_v, w_in, w_out = _even_ctx(yp, mp, w_in_even, w_out_even, i, decay_rows,
                                                        ln_g[l, 0], ln_b[l, 0])
            sp = _inproj(ys, ms, w_in, EVEN_SIZES, dts, DEC_SEQ)
            mix_s = _even_lat(*sp, state_ret, cache_na_k, cache_na_v, i, decay_rows, na_rpb[i].reshape(-1))
            ys = _outproj(mix_s, ys, ms, w_out, ln_g[l, 0], ln_b[l, 0], DEC_SEQ)
            out.setdefault("st", []).append(st.reshape(BATCH, 2, RET_HEADS, RET_DK, RET_DV))
            out.setdefault("na_k", []).append(na_k.reshape(BATCH, SEQ, NA_HEADS, NA_DH))
            out.setdefault("na_v", []).append(na_v.reshape(BATCH, SEQ, NA_HEADS, NA_DH))
        else:
            qg = q_norm_g[i].reshape(1, ATT_DH)
            kg = k_norm_g[i].reshape(1, ATT_DH)
            yp, k_new, att_v, w_in, w_out = _odd_ctx(yp, mp, w_in_odd, w_out_odd, i, conv_w[i], qg, kg,
                                                     ln_g[l, 0], ln_b[l, 0])
            cos, sin = _rope_tables()
            ys = _odd_lat(ys, ms, w_in, cache_att_k, cache_att_v, i,
                          conv_w[i], qg, kg, cos, sin, w_out, ln_g[l, 0], ln_b[l, 0])
            out.setdefault("att_k", []).append(k_new.reshape(BATCH, SEQ, ATT_KV_HEADS, ATT_DH))
            out.setdefault("att_v", []).append(att_v.reshape(BATCH, SEQ, ATT_KV_HEADS, ATT_DH))
        rwt = router_w[l].T
        rb = router_b[l].reshape(N_EXPERTS, 1)
        moe_w = (rwt, rb, exp_w_gate, exp_w_up, exp_w_down, sh_w_gate[l], sh_w_up[l], sh_w_down[l],
                 ln_g[l, 1], ln_b[l, 1])
        yp = _moe(yp, mp, l, *moe_w, SEQ)
        ys = _moe(ys, ms, l, *moe_w, DEC_SEQ)

    return (yp.reshape(BATCH, SEQ, D_MODEL),
            ys.reshape(DEC_BATCH, DEC_SEQ, D_MODEL),
            jnp.stack(out["st"], axis=1),
            jnp.stack(out["na_k"], axis=1),
            jnp.stack(out["na_v"], axis=1),
            jnp.stack(out["att_k"], axis=1),
            jnp.stack(out["att_v"], axis=1))
```

```python
import functools

import jax
import jax.numpy as jnp
from jax import lax
from jax.experimental import pallas as pl
from jax.experimental.pallas import tpu as pltpu

F32 = jnp.float32
BF16 = jnp.bfloat16

D_MODEL = 1024
BATCH = 32
SEQ = 256
DEPTH = 2
DEC_BATCH = 4
DEC_SEQ = 1024
PAST_LEN = 512
GRID_W = 64
GRID_H = DEC_SEQ // GRID_W
MIX_HALF = D_MODEL // 2
RET_HEADS = 4
RET_DV = MIX_HALF // RET_HEADS
RET_DK = RET_DV // 2
NA_HEADS = 4
NA_DH = MIX_HALF // NA_HEADS
NA_WIN_R = 8
NA_WIN_C = 16
CONV_CH = MIX_HALF
ATT_HEADS = 4
ATT_KV_HEADS = 2
ATT_DH = MIX_HALF // ATT_HEADS
ROPE_THETA = 10000.0
N_EXPERTS = 64
EXPERT_HIDDEN = D_MODEL // 8
TOP_K = 8
N_GROUPS = 8
GROUP_SIZE = N_EXPERTS // N_GROUPS
TOPK_GROUPS = 4
ROUTED_SCALE = 2.5
DEEPNORM_ALPHA = (2 * DEPTH) ** 0.25
LN_EPS = 1e-5
RMS_EPS = 1e-6
EVEN_SIZES = (RET_HEADS * RET_DK, RET_HEADS * RET_DK, RET_HEADS * RET_DV, RET_HEADS * RET_DV,
              NA_HEADS * NA_DH, NA_HEADS * NA_DH, NA_HEADS * NA_DH)
ODD_SIZES = (CONV_CH, CONV_CH, CONV_CH, ATT_HEADS * ATT_DH, ATT_KV_HEADS * ATT_DH, ATT_KV_HEADS * ATT_DH)
N_MOD = 6
RPB_ROWS = 2 * NA_WIN_R - 1
RPB_COLS = 2 * NA_WIN_C - 1

LANES = 128
COND_ROWS = 16
V7X_VMEM_LIMIT = 56 * 1024 * 1024

PROJ_TILE = 512
LAT_QBLK = 256
MOE_TILE = 1024
MOE_EPS = 4
ADALN_TN = 1536


def _params(n_axes, vmem_limit=V7X_VMEM_LIMIT):
    return pltpu.CompilerParams(dimension_semantics=("arbitrary",) * n_axes,
                                vmem_limit_bytes=vmem_limit)


def _dot(a, b):
    return jnp.dot(a, b, preferred_element_type=F32)


def _dot_nt(a, b):
    return lax.dot_general(a, b, (((1,), (1,)), ((), ())), preferred_element_type=F32)


def _split(a):
    hi = a.astype(BF16)
    lo = (a - hi.astype(F32)).astype(BF16)
    return hi, lo


def _silu(x):
    return x / (1.0 + jnp.exp(-x))


def _log_sigmoid(x):
    return jnp.minimum(x, 0.0) - jnp.log(1.0 + jnp.exp(-jnp.abs(x)))


def _layer_norm(z, g, b):
    mu = jnp.mean(z, axis=-1, keepdims=True)
    zc = z - mu
    var = jnp.mean(zc * zc, axis=-1, keepdims=True)
    return zc * lax.rsqrt(var + LN_EPS) * g + b


def _post_norm(x, y, gate, g, b):
    return _layer_norm(DEEPNORM_ALPHA * x + (1.0 + gate) * y, g, b)


def _head_norm(x):
    mu = jnp.mean(x, axis=-1, keepdims=True)
    xc = x - mu
    var = jnp.mean(xc * xc, axis=-1, keepdims=True)
    return xc * lax.rsqrt(var + LN_EPS)


def _rms_norm(x, g):
    return x * lax.rsqrt(jnp.mean(x * x, axis=-1, keepdims=True) + RMS_EPS) * g


def _softmax_pv(parts):
    m = None
    for s, _ in parts:
        mi = jnp.max(s, axis=-1, keepdims=True)
        m = mi if m is None else jnp.maximum(m, mi)
    l = None
    o = None
    for s, v in parts:
        p = jnp.exp(s - m)
        li = jnp.sum(p, axis=-1, keepdims=True)
        oi = _dot(p.astype(BF16), v)
        l = li if l is None else l + li
        o = oi if o is None else o + oi
    return o / l


def _adaln_kernel(c_ref, w_ref, b_ref, o_ref):
    a_hi, a_lo = _split(_silu(c_ref[...]))
    w_hi, w_lo = _split(w_ref[...])
    o_ref[...] = _dot(a_hi, w_hi) + _dot(a_lo, w_hi) + _dot(a_hi, w_lo) + b_ref[...]


def _adaln(cond, w_mod, b_mod):
    n = w_mod.shape[-1]
    return pl.pallas_call(
        _adaln_kernel,
        grid=(DEPTH, n // ADALN_TN),
        in_specs=[pl.BlockSpec((COND_ROWS, D_MODEL), lambda l, j: (0, 0)),
                  pl.BlockSpec((None, D_MODEL, ADALN_TN), lambda l, j: (l, 0, j)),
                  pl.BlockSpec((None, 1, ADALN_TN), lambda l, j: (l, 0, j))],
        out_specs=pl.BlockSpec((None, COND_ROWS, ADALN_TN), lambda l, j: (l, 0, j)),
        out_shape=jax.ShapeDtypeStruct((DEPTH, COND_ROWS, n), F32),
        compiler_params=_params(2),
        name="adaln",
    )(cond, w_mod, b_mod.reshape(DEPTH, 1, n))


def _inproj_kernel(x_ref, mod_ref, w_ref, *out_refs, sizes):
    shift = mod_ref[0:1, :]
    scale = mod_ref[1:2, :]
    h = (x_ref[...] * (1.0 + scale) + shift).astype(BF16)
    y = _dot(h, w_ref[...])
    for o_ref, off, n in zip(out_refs, _offsets(sizes), sizes):
        o_ref[...] = y[:, off:off + n].astype(o_ref.dtype)


def _inproj(x, mods, w_bf16, sizes, dtypes, seq_len):
    t = x.shape[0]
    tile = PROJ_TILE
    mod_idx = (lambda i: (i * tile // seq_len, 0, 0)) if mods.shape[0] > 1 else (lambda i: (0, 0, 0))
    return pl.pallas_call(
        functools.partial(_inproj_kernel, sizes=sizes),
        grid=(t // tile,),
        in_specs=[pl.BlockSpec((tile, D_MODEL), lambda i: (i, 0)),
                  pl.BlockSpec((None, N_MOD, D_MODEL), mod_idx),
                  pl.BlockSpec(w_bf16.shape, lambda i: (0, 0))],
        out_specs=[pl.BlockSpec((tile, n), lambda i: (i, 0)) for n in sizes],
        out_shape=[jax.ShapeDtypeStruct((t, n), dt) for n, dt in zip(sizes, dtypes)],
        compiler_params=_params(1),
        name="inproj",
    )(x, mods, w_bf16)


def _outproj_kernel(mix_ref, x_ref, mod_ref, w_ref, g_ref, b_ref, o_ref):
    y = _dot(mix_ref[...], w_ref[...])
    o_ref[...] = _post_norm(x_ref[...], y, mod_ref[2:3, :], g_ref[...], b_ref[...])


def _outproj(mix, x, mods, w_bf16, ln_g, ln_b, seq_len):
    t = x.shape[0]
    tile = PROJ_TILE
    mod_idx = (lambda i: (i * tile // seq_len, 0, 0)) if mods.shape[0] > 1 else (lambda i: (0, 0, 0))
    return pl.pallas_call(
        _outproj_kernel,
        grid=(t // tile,),
        in_specs=[pl.BlockSpec((tile, D_MODEL), lambda i: (i, 0)),
                  pl.BlockSpec((tile, D_MODEL), lambda i: (i, 0)),
                  pl.BlockSpec((None, N_MOD, D_MODEL), mod_idx),
                  pl.BlockSpec((D_MODEL, D_MODEL), lambda i: (0, 0)),
                  pl.BlockSpec((1, D_MODEL), lambda i: (0, 0)),
                  pl.BlockSpec((1, D_MODEL), lambda i: (0, 0))],
        out_specs=pl.BlockSpec((tile, D_MODEL), lambda i: (i, 0)),
        out_shape=jax.ShapeDtypeStruct((t, D_MODEL), F32),
        compiler_params=_params(1),
        name="outproj",
    )(mix, x, mods, w_bf16, ln_g.reshape(1, D_MODEL), ln_b.reshape(1, D_MODEL))


def _decay_matrix(length, lgf, lgb, row0=0, rows=None):
    rows = length if rows is None else rows
    ii = lax.broadcasted_iota(jnp.int32, (rows, length), 0) + row0
    jj = lax.broadcasted_iota(jnp.int32, (rows, length), 1)
    rel = (ii - jj).astype(F32)
    fwd = jnp.where(rel >= 0.0, jnp.exp(lgf * jnp.maximum(rel, 0.0)), 0.0)
    bwd = jnp.where(rel <= 0.0, jnp.exp(lgb * jnp.maximum(-rel, 0.0)), 0.0)
    return fwd + bwd


def _retention_head(q, k, v, lgf, lgb, s0f, s0b, dmat, row0=0):
    rows, length = q.shape[0], k.shape[0]
    s = _dot_nt(q.astype(BF16), k.astype(BF16))
    if dmat.dtype == BF16:
        o = _dot(s.astype(BF16) * dmat, v)
    else:
        o = _dot((s * dmat).astype(BF16), v)
    if s0f is not None:
        pos = (lax.broadcasted_iota(jnp.int32, (rows, 1), 0) + row0).astype(F32)
        qf = q * jnp.exp(lgf * (pos + 1.0))
        qb = q * jnp.exp(lgb * (length - pos))
        o = o + _dot(qf.astype(BF16), s0f.astype(BF16)) + _dot(qb.astype(BF16), s0b.astype(BF16))
    return o


def _offsets(sizes):
    return [sum(sizes[:k]) for k in range(len(sizes))]


def _project(x_ref, mod_ref, wi_ref, dst_ref):
    h = (x_ref[...] * (1.0 + mod_ref[1:2, :]) + mod_ref[0:1, :]).astype(BF16)
    dst_ref[...] = _dot(h, wi_ref[...])


def _cast_rows(src_ref, dst_ref):
    def body(r, carry):
        rows = pl.ds(pl.multiple_of(r * LANES, LANES), LANES)
        dst_ref[rows, :] = src_ref[rows, :].astype(BF16)
        return carry
    lax.fori_loop(0, src_ref.shape[0] // LANES, body, 0)


def _project_next(xn_ref, xc_ref, mod_ref, wif_ref, wof_ref, wi_ref, wo_ref, nxt_ref, cur_ref):
    @pl.when(pl.program_id(0) == 0)
    def _():
        _cast_rows(wif_ref, wi_ref)
        _cast_rows(wof_ref, wo_ref)
        _project(xc_ref, mod_ref, wi_ref, cur_ref)

    _project(xn_ref, mod_ref, wi_ref, nxt_ref)
    return cur_ref


def _even_ctx_kernel(xn_ref, xc_ref, mod_ref, wif_ref, dl_ref, wof_ref, lng_ref, lnb_ref,
                     y_ref, st_ref, nak_ref, nav_ref, wi_ref, wo_ref,
                     nxt_ref, cur_ref, mix_ref, dmat_ref, kdec_ref):
    length = xc_ref.shape[0]

    @pl.when(pl.program_id(0) == 0)
    def _():
        lg = _log_sigmoid(dl_ref[...])
        pos = lax.broadcasted_iota(jnp.int32, (length, 1), 0).astype(F32)
        for h in range(RET_HEADS):
            lgf = lg[h:h + 1, 0:1]
            lgb = lg[RET_HEADS + h:RET_HEADS + h + 1, 0:1]
            dmat_ref[h] = _decay_matrix(length, lgf, lgb)
            kdec_ref[h] = jnp.broadcast_to(jnp.exp(lgf * (length - 1.0 - pos)), (length, RET_DK))
            kdec_ref[RET_HEADS + h] = jnp.broadcast_to(jnp.exp(lgb * pos), (length, RET_DK))

    cur = _project_next(xn_ref, xc_ref, mod_ref, wif_ref, wof_ref, wi_ref, wo_ref, nxt_ref, cur_ref)
    o_qr, o_kr, o_vr, o_gr, o_qn, o_kn, o_vn = _offsets(EVEN_SIZES)
    eye = jnp.where(lax.broadcasted_iota(jnp.int32, (2 * RET_DK, 2 * RET_DK), 0)
                    == lax.broadcasted_iota(jnp.int32, (2 * RET_DK, 2 * RET_DK), 1), 1.0, 0.0).astype(BF16)
    for h in range(RET_HEADS):
        q = cur[:, o_qr + h * RET_DK:o_qr + (h + 1) * RET_DK]
        k = cur[:, o_kr + h * RET_DK:o_kr + (h + 1) * RET_DK] * (RET_DK ** -0.5)
        v = cur[:, o_vr + h * RET_DV:o_vr + (h + 1) * RET_DV].astype(BF16)
        o = _retention_head(q, k, v, None, None, None, None, dmat=dmat_ref[h])
        k_dec = jnp.concatenate([k * kdec_ref[h], k * kdec_ref[RET_HEADS + h]], axis=1).astype(BF16)
        states = _dot(_dot_nt(eye, k_dec).astype(BF16), v)
        st_ref[h] = states[:RET_DK]
        st_ref[RET_HEADS + h] = states[RET_DK:]
        g = cur[:, o_gr + h * RET_DV:o_gr + (h + 1) * RET_DV]
        mix_ref[:, h * RET_DV:(h + 1) * RET_DV] = (_head_norm(o) * _silu(g)).astype(mix_ref.dtype)
    base = RET_HEADS * RET_DV
    for h in range(NA_HEADS):
        q = cur[:, o_qn + h * NA_DH:o_qn + (h + 1) * NA_DH].astype(BF16)
        k = cur[:, o_kn + h * NA_DH:o_kn + (h + 1) * NA_DH]
        v = cur[:, o_vn + h * NA_DH:o_vn + (h + 1) * NA_DH]
        nak_ref[:, h, :] = k
        nav_ref[:, h, :] = v
        s = _dot_nt(q, k.astype(BF16)) * (NA_DH ** -0.5)
        mix_ref[:, base + h * NA_DH:base + (h + 1) * NA_DH] = _softmax_pv([(s, v.astype(BF16))]).astype(mix_ref.dtype)
    y = _dot(mix_ref[...], wo_ref[...])
    y_ref[...] = _post_norm(xc_ref[...], y, mod_ref[2:3, :], lng_ref[...], lnb_ref[...])
    cur_ref[...] = nxt_ref[...]


def _ctx_layer_specs(n_proj, idx):
    def nxt(n):
        return pl.BlockSpec((SEQ, n), lambda i: (jnp.minimum(i + 1, BATCH - 1), 0))

    def cur(shape):
        return pl.BlockSpec(shape, lambda i: (i,) + (0,) * (len(shape) - 1))

    def full(shape):
        return pl.BlockSpec(shape, lambda i: (0,) * len(shape))

    def weight(n):
        return pl.BlockSpec((None, D_MODEL, n), lambda i: (idx, 0, 0), pipeline_mode=pl.Buffered(1))
    head = [nxt(D_MODEL), cur((SEQ, D_MODEL)), full((None, N_MOD, D_MODEL)), weight(n_proj)]
    tail = [weight(D_MODEL), full((1, D_MODEL)), full((1, D_MODEL))]
    w_specs = [full((D_MODEL, n_proj)), full((D_MODEL, D_MODEL))]
    w_shapes = [jax.ShapeDtypeStruct((D_MODEL, n_proj), BF16), jax.ShapeDtypeStruct((D_MODEL, D_MODEL), BF16)]
    return head, tail, cur, full, w_specs, w_shapes


def _even_ctx(x, mods, w_in, w_out, idx, decay_rows, ln_g, ln_b):
    head, tail, cur, full, w_specs, w_shapes = _ctx_layer_specs(sum(EVEN_SIZES), idx)
    return pl.pallas_call(
        _even_ctx_kernel,
        grid=(BATCH,),
        in_specs=head + [full((2 * RET_HEADS, LANES))] + tail,
        out_specs=[cur((SEQ, D_MODEL)),
                   cur((None, 2 * RET_HEADS, RET_DK, RET_DV)),
                   cur((SEQ, NA_HEADS, NA_DH)), cur((SEQ, NA_HEADS, NA_DH))] + w_specs,
        out_shape=[jax.ShapeDtypeStruct((BATCH * SEQ, D_MODEL), F32),
                   jax.ShapeDtypeStruct((BATCH, 2 * RET_HEADS, RET_DK, RET_DV), F32),
                   jax.ShapeDtypeStruct((BATCH * SEQ, NA_HEADS, NA_DH), F32),
                   jax.ShapeDtypeStruct((BATCH * SEQ, NA_HEADS, NA_DH), F32)] + w_shapes,
        scratch_shapes=[pltpu.VMEM((SEQ, sum(EVEN_SIZES)), F32),
                        pltpu.VMEM((SEQ, sum(EVEN_SIZES)), F32),
                        pltpu.VMEM((SEQ, D_MODEL), BF16),
                        pltpu.VMEM((RET_HEADS, SEQ, SEQ), F32),
                        pltpu.VMEM((2 * RET_HEADS, SEQ, RET_DK), F32)],
        compiler_params=_params(1),
        name="even_ctx",
    )(x, x, mods, w_in, decay_rows, w_out, ln_g.reshape(1, D_MODEL), ln_b.reshape(1, D_MODEL))


def _build_rpb_tiles(rpb_ref, tile_ref):
    qc = lax.broadcasted_iota(jnp.int32, (GRID_W, LANES), 0)
    kc = lax.broadcasted_iota(jnp.int32, (GRID_W, LANES), 1) % GRID_W
    diff = kc - qc + (NA_WIN_C - 1)
    start = jnp.clip(qc - NA_WIN_C // 2, 0, GRID_W - NA_WIN_C)
    win = (kc >= start) & (kc < start + NA_WIN_C)

    def body(idx, carry):
        t = jnp.zeros((GRID_W, LANES), F32)
        for d in range(RPB_COLS):
            t = jnp.where(diff == d, rpb_ref[idx * RPB_COLS + d], t)
        tile_ref[idx] = jnp.where(win, t, -jnp.inf)
        return carry

    lax.fori_loop(0, NA_HEADS * RPB_ROWS, body, 0)


RET_QBLK = 512
NA_Q_ROWS = 4
NA_KEY_ROWS = 12


def _band_start(qr):
    rows_w = min(NA_WIN_R, GRID_H)
    return min(max(qr - rows_w // 2, 0), GRID_H - rows_w)


def _key_window(qrow0):
    w0 = min(_band_start(qrow0), GRID_H - NA_KEY_ROWS)
    assert _band_start(qrow0 + NA_Q_ROWS - 1) + min(NA_WIN_R, GRID_H) <= w0 + NA_KEY_ROWS
    return w0


def _build_bias(tile_ref, bias_ref, head, qrow0):
    left = lax.broadcasted_iota(jnp.int32, (GRID_W, LANES), 1) < GRID_W
    neg = jnp.full((GRID_W, LANES), -jnp.inf, F32)
    rows_w = min(NA_WIN_R, GRID_H)
    w0 = _key_window(qrow0)
    for dq in range(NA_Q_ROWS):
        qr = qrow0 + dq
        rs = _band_start(qr)

        def tile(kr):
            if rs <= kr < rs + rows_w:
                return tile_ref[head * RPB_ROWS + kr - qr + NA_WIN_R - 1]
            return neg

        pieces = [jnp.where(left, tile(w0 + 2 * a), tile(w0 + 2 * a + 1)) for a in range(NA_KEY_ROWS // 2)]
        bias_ref[dq * GRID_W:(dq + 1) * GRID_W, :] = jnp.concatenate(pieces, axis=1)


def _even_lat_kernel(qr_ref, kr_ref, vr_ref, gr_ref, qn_ref, kn_ref, vn_ref, s0_ref, ck_ref, cv_ref,
                     dl_ref, rpb_ref, mix_ref, tile_ref, bias_ref, dmat_ref):
    length = qr_ref.shape[0]
    lg = _log_sigmoid(dl_ref[...])

    @pl.when(pl.program_id(0) == 0)
    def _():
        _build_rpb_tiles(rpb_ref, tile_ref)
        for h in range(RET_HEADS):
            lgf = lg[h:h + 1, 0:1]
            lgb = lg[RET_HEADS + h:RET_HEADS + h + 1, 0:1]

            def rows_body(r, carry, h=h, lgf=lgf, lgb=lgb):
                r0 = pl.multiple_of(r * LANES, LANES)
                dmat_ref[h, pl.ds(r0, LANES), :] = _decay_matrix(length, lgf, lgb, r0, LANES).astype(BF16)
                return carry

            lax.fori_loop(0, length // LANES, rows_body, 0)

    for h in range(RET_HEADS):
        lgf = lg[h:h + 1, 0:1]
        lgb = lg[RET_HEADS + h:RET_HEADS + h + 1, 0:1]
        k = kr_ref[:, h * RET_DK:(h + 1) * RET_DK].astype(F32) * (RET_DK ** -0.5)
        v = vr_ref[:, h * RET_DV:(h + 1) * RET_DV]
        for r0 in range(0, length, RET_QBLK):
            rows = slice(r0, r0 + RET_QBLK)
            q = qr_ref[rows, h * RET_DK:(h + 1) * RET_DK].astype(F32)
            o = _retention_head(q, k, v, lgf, lgb, s0_ref[0, h], s0_ref[1, h], dmat_ref[h, rows, :], r0)
            g = gr_ref[rows, h * RET_DV:(h + 1) * RET_DV]
            mix_ref[rows, h * RET_DV:(h + 1) * RET_DV] = (_head_norm(o) * _silu(g)).astype(mix_ref.dtype)
    base = RET_HEADS * RET_DV
    scale = NA_DH ** -0.5
    for h in range(NA_HEADS):
        sl = slice(h * NA_DH, (h + 1) * NA_DH)
        kc = ck_ref[:, sl].astype(BF16)
        vc = cv_ref[:, sl].astype(BF16)
        for qrow0 in range(0, GRID_H, NA_Q_ROWS):
            _build_bias(tile_ref, bias_ref, h, qrow0)
            rows = slice(qrow0 * GRID_W, (qrow0 + NA_Q_ROWS) * GRID_W)
            k0 = _key_window(qrow0) * GRID_W
            keys = slice(k0, k0 + NA_KEY_ROWS * GRID_W)
            q = qn_ref[rows, sl]
            s_band = _dot_nt(q, kn_ref[keys, sl]) * scale + bias_ref[...]
            s_ctx = _dot_nt(q, kc) * scale
            o = _softmax_pv([(s_band, vn_ref[keys, sl]), (s_ctx, vc)])
            mix_ref[rows, base + h * NA_DH:base + (h + 1) * NA_DH] = o.astype(mix_ref.dtype)


def _even_lat(qr, kr, vr, gr, qn, kn, vn, state, cache_k, cache_v, idx, decay_rows, rpb_flat):
    def seq_spec(n):
        return pl.BlockSpec((DEC_SEQ, n), lambda i: (i, 0))
    cache_spec = pl.BlockSpec((None, PAST_LEN, NA_HEADS * NA_DH), lambda i: (i, 0, 0))
    return pl.pallas_call(
        _even_lat_kernel,
        grid=(DEC_BATCH,),
        in_specs=[seq_spec(n) for n in EVEN_SIZES]
        + [pl.BlockSpec((None, None, 2, RET_HEADS, RET_DK, RET_DV), lambda i: (i, idx, 0, 0, 0, 0)),
           cache_spec, cache_spec,
           pl.BlockSpec((2 * RET_HEADS, LANES), lambda i: (0, 0)),
           pl.BlockSpec(memory_space=pltpu.SMEM)],
        out_specs=seq_spec(D_MODEL),
        out_shape=jax.ShapeDtypeStruct((DEC_BATCH * DEC_SEQ, D_MODEL), BF16),
        scratch_shapes=[pltpu.VMEM((NA_HEADS * RPB_ROWS, GRID_W, LANES), F32),
                        pltpu.VMEM((NA_Q_ROWS * GRID_W, NA_KEY_ROWS * GRID_W), F32),
                        pltpu.VMEM((RET_HEADS, DEC_SEQ, DEC_SEQ), BF16)],
        compiler_params=_params(1),
        name="even_lat",
    )(qr, kr, vr, gr, qn, kn, vn, state, cache_k, cache_v, decay_rows, rpb_flat)


def _gated_conv(bg, cg, u, w_ref):
    length, ch = u.shape
    z = cg * u
    row = lax.broadcasted_iota(jnp.int32, (length, ch), 0)
    z_prev = jnp.where(row == 0, 0.0, pltpu.roll(z, 1, 0))
    z_next = jnp.where(row == length - 1, 0.0, pltpu.roll(z, length - 1, 0))
    return bg * (z_prev * w_ref[0:1, :] + z * w_ref[1:2, :] + z_next * w_ref[2:3, :])


def _odd_ctx_kernel(xn_ref, xc_ref, mod_ref, wif_ref, w_ref, qg_ref, kg_ref, wof_ref, lng_ref, lnb_ref,
                    y_ref, ko_ref, vo_ref, wi_ref, wo_ref, nxt_ref, cur_ref, mix_ref):
    cur = _project_next(xn_ref, xc_ref, mod_ref, wif_ref, wof_ref, wi_ref, wo_ref, nxt_ref, cur_ref)
    o_bg, o_cg, o_u, o_q, o_k, o_v = _offsets(ODD_SIZES)
    conv = _gated_conv(cur[:, o_bg:o_bg + CONV_CH], cur[:, o_cg:o_cg + CONV_CH], cur[:, o_u:o_u + CONV_CH], w_ref)
    mix_ref[:, 0:CONV_CH] = conv.astype(mix_ref.dtype)
    group = ATT_HEADS // ATT_KV_HEADS
    scale = ATT_DH ** -0.5
    for kv in range(ATT_KV_HEADS):
        k = _rms_norm(cur[:, o_k + kv * ATT_DH:o_k + (kv + 1) * ATT_DH], kg_ref[...])
        v = cur[:, o_v + kv * ATT_DH:o_v + (kv + 1) * ATT_DH]
        ko_ref[:, kv, :] = k
        vo_ref[:, kv, :] = v
        k = k.astype(BF16)
        v = v.astype(BF16)
        for g in range(group):
            h = kv * group + g
            q = _rms_norm(cur[:, o_q + h * ATT_DH:o_q + (h + 1) * ATT_DH], qg_ref[...]).astype(BF16)
            o = _softmax_pv([(_dot_nt(q, k) * scale, v)])
            mix_ref[:, CONV_CH + h * ATT_DH:CONV_CH + (h + 1) * ATT_DH] = o.astype(mix_ref.dtype)
    y = _dot(mix_ref[...], wo_ref[...])
    y_ref[...] = _post_norm(xc_ref[...], y, mod_ref[2:3, :], lng_ref[...], lnb_ref[...])
    cur_ref[...] = nxt_ref[...]


def _odd_ctx(x, mods, w_in, w_out, idx, conv_w, qn_g, kn_g, ln_g, ln_b):
    head, tail, cur, full, w_specs, w_shapes = _ctx_layer_specs(sum(ODD_SIZES), idx)
    kv_shape = (SEQ, ATT_KV_HEADS, ATT_DH)
    return pl.pallas_call(
        _odd_ctx_kernel,
        grid=(BATCH,),
        in_specs=head + [full((3, CONV_CH)), full((1, ATT_DH)), full((1, ATT_DH))] + tail,
        out_specs=[cur((SEQ, D_MODEL)), cur(kv_shape), cur(kv_shape)] + w_specs,
        out_shape=[jax.ShapeDtypeStruct((BATCH * SEQ, D_MODEL), F32),
                   jax.ShapeDtypeStruct((BATCH * SEQ, ATT_KV_HEADS, ATT_DH), F32),
                   jax.ShapeDtypeStruct((BATCH * SEQ, ATT_KV_HEADS, ATT_DH), F32)] + w_shapes,
        scratch_shapes=[pltpu.VMEM((SEQ, sum(ODD_SIZES)), F32),
                        pltpu.VMEM((SEQ, sum(ODD_SIZES)), F32),
                        pltpu.VMEM((SEQ, D_MODEL), BF16)],
        compiler_params=_params(1),
        name="odd_ctx",
    )(x, x, mods, w_in, conv_w, qn_g, kn_g, w_out, ln_g.reshape(1, D_MODEL), ln_b.reshape(1, D_MODEL))


def _rope(x, cos, sin, even_lane):
    swapped = jnp.where(even_lane, pltpu.roll(x, LANES - 1, 1), pltpu.roll(x, 1, 1))
    return x * cos + swapped * sin


def _odd_lat_kernel(x_ref, mod_ref, wi_ref, ck_ref, cv_ref, w_ref, qg_ref, kg_ref, cos_ref, sin_ref,
                    wo_ref, lng_ref, lnb_ref, y_ref, proj_ref, qb_ref, kb_ref, vb_ref, mix_ref):
    length = x_ref.shape[0]
    h_in = (x_ref[...] * (1.0 + mod_ref[1:2, :]) + mod_ref[0:1, :]).astype(BF16)
    proj_ref[...] = _dot(h_in, wi_ref[...])
    o_bg, o_cg, o_u, o_q, o_k, o_v = _offsets(ODD_SIZES)
    conv = _gated_conv(proj_ref[:, o_bg:o_bg + CONV_CH], proj_ref[:, o_cg:o_cg + CONV_CH],
                       proj_ref[:, o_u:o_u + CONV_CH], w_ref)
    mix_ref[:, 0:CONV_CH] = conv.astype(mix_ref.dtype)
    group = ATT_HEADS // ATT_KV_HEADS
    scale = ATT_DH ** -0.5
    even_lane = lax.broadcasted_iota(jnp.int32, (length, ATT_DH), 1) % 2 == 0
    cos = cos_ref[...]
    sin = sin_ref[...]
    for kv in range(ATT_KV_HEADS):
        sl = slice(kv * ATT_DH, (kv + 1) * ATT_DH)
        k = _rms_norm(proj_ref[:, o_k + kv * ATT_DH:o_k + (kv + 1) * ATT_DH], kg_ref[...])
        kb_ref[0:length, sl] = _rope(k, cos, sin, even_lane).astype(BF16)
        vb_ref[0:length, sl] = proj_ref[:, o_v + kv * ATT_DH:o_v + (kv + 1) * ATT_DH].astype(BF16)
        kb_ref[length:, sl] = ck_ref[:, kv, :].astype(BF16)
        vb_ref[length:, sl] = cv_ref[:, kv, :].astype(BF16)
    for h in range(ATT_HEADS):
        q = _rms_norm(proj_ref[:, o_q + h * ATT_DH:o_q + (h + 1) * ATT_DH], qg_ref[...])
        qb_ref[:, h * ATT_DH:(h + 1) * ATT_DH] = _rope(q, cos, sin, even_lane).astype(BF16)

    def q_block(b, carry):
        rows = pl.ds(pl.multiple_of(b * LAT_QBLK, LAT_QBLK), LAT_QBLK)
        for h in range(ATT_HEADS):
            sl = slice((h // group) * ATT_DH, (h // group + 1) * ATT_DH)
            q = qb_ref[rows, h * ATT_DH:(h + 1) * ATT_DH]
            o = _softmax_pv([(_dot_nt(q, kb_ref[:, sl]) * scale, vb_ref[:, sl])])
            mix_ref[rows, CONV_CH + h * ATT_DH:CONV_CH + (h + 1) * ATT_DH] = o.astype(mix_ref.dtype)
        return carry

    lax.fori_loop(0, length // LAT_QBLK, q_block, 0)
    y = _dot(mix_ref[...], wo_ref[...])
    y_ref[...] = _post_norm(x_ref[...], y, mod_ref[2:3, :], lng_ref[...], lnb_ref[...])


def _odd_lat(x, mods, w_in, cache_k, cache_v, idx, conv_w, qn_g, kn_g, cos, sin, w_out, ln_g, ln_b):
    def full(shape):
        return pl.BlockSpec(shape, lambda i: (0,) * len(shape))

    def const(shape):
        return pl.BlockSpec(shape, lambda i: (0,) * len(shape), pipeline_mode=pl.Buffered(1))
    kv_w = ATT_KV_HEADS * ATT_DH
    cache_spec = pl.BlockSpec((None, None, PAST_LEN, ATT_KV_HEADS, ATT_DH), lambda i: (i, idx, 0, 0, 0))
    return pl.pallas_call(
        _odd_lat_kernel,
        grid=(DEC_BATCH,),
        in_specs=[pl.BlockSpec((DEC_SEQ, D_MODEL), lambda i: (i, 0)),
                  pl.BlockSpec((None, N_MOD, D_MODEL), lambda i: (i, 0, 0)),
                  const((D_MODEL, sum(ODD_SIZES))),
                  cache_spec, cache_spec, full((3, CONV_CH)), full((1, ATT_DH)), full((1, ATT_DH)),
                  const((DEC_SEQ, ATT_DH)), const((DEC_SEQ, ATT_DH)),
                  const((D_MODEL, D_MODEL)), full((1, D_MODEL)), full((1, D_MODEL))],
        out_specs=pl.BlockSpec((DEC_SEQ, D_MODEL), lambda i: (i, 0)),
        out_shape=jax.ShapeDtypeStruct((DEC_BATCH * DEC_SEQ, D_MODEL), F32),
        scratch_shapes=[pltpu.VMEM((DEC_SEQ, sum(ODD_SIZES)), F32),
                        pltpu.VMEM((DEC_SEQ, ATT_HEADS * ATT_DH), BF16),
                        pltpu.VMEM((DEC_SEQ + PAST_LEN, kv_w), BF16),
                        pltpu.VMEM((DEC_SEQ + PAST_LEN, kv_w), BF16),
                        pltpu.VMEM((DEC_SEQ, D_MODEL), BF16)],
        compiler_params=_params(1),
        name="odd_lat",
    )(x, mods, w_in, cache_k, cache_v, conv_w, qn_g, kn_g, cos, sin, w_out,
      ln_g.reshape(1, D_MODEL), ln_b.reshape(1, D_MODEL))


def _rope_tables():
    t = jnp.arange(DEC_SEQ)
    row = (t // GRID_W).astype(F32)
    col = (t % GRID_W).astype(F32)
    n_freq = ATT_DH // 4
    freqs = ROPE_THETA ** (-jnp.arange(n_freq, dtype=F32) / n_freq)
    ang = jnp.concatenate([row[:, None] * freqs, col[:, None] * freqs], axis=-1)
    cos = jnp.repeat(jnp.cos(ang), 2, axis=-1)
    sin = jnp.stack([-jnp.sin(ang), jnp.sin(ang)], axis=-1).reshape(DEC_SEQ, ATT_DH)
    return cos, sin


def _route(scores, sel):
    n_tok = sel.shape[1]
    neg = -jnp.inf
    sub = lax.broadcasted_iota(jnp.int32, (GROUP_SIZE, n_tok), 0).astype(F32)
    blocks = [sel[g * GROUP_SIZE:(g + 1) * GROUP_SIZE, :] for g in range(N_GROUPS)]
    grp = []
    for blk in blocks:
        m1 = jnp.max(blk, axis=0, keepdims=True)
        i1 = jnp.min(jnp.where(blk == m1, sub, float(GROUP_SIZE)), axis=0, keepdims=True)
        m2 = jnp.max(jnp.where(sub == i1, neg, blk), axis=0, keepdims=True)
        grp.append(m1 + m2)
    masked = []
    for g in range(N_GROUPS):
        ahead = jnp.zeros((1, n_tok), F32)
        for o in range(N_GROUPS):
            if o == g:
                continue
            wins = grp[o] >= grp[g] if o < g else grp[o] > grp[g]
            ahead = ahead + jnp.where(wins, 1.0, 0.0)
        masked.append(jnp.where(ahead < float(TOPK_GROUPS), blocks[g], neg))
    val = jnp.concatenate(masked, axis=0)
    row = lax.broadcasted_iota(jnp.int32, (N_EXPERTS, n_tok), 0).astype(F32)
    w = jnp.zeros((N_EXPERTS, n_tok), F32)
    for _ in range(TOP_K):
        m = jnp.max(val, axis=0, keepdims=True)
        idx = jnp.min(jnp.where(val == m, row, float(N_EXPERTS)), axis=0, keepdims=True)
        pick = row == idx
        w = jnp.where(pick, scores, w)
        val = jnp.where(pick, neg, val)
    return w / jnp.sum(w, axis=0, keepdims=True) * ROUTED_SCALE


def _moe_kernel(y_ref, mod_ref, rwt_ref, rb_ref, wg_hbm, wu_hbm, wd_hbm, sg_ref, su_ref, sd_ref,
                lng_ref, lnb_ref, o_ref, xb_ref, gate_ref, acc_ref, wg_buf, wu_buf, wd_buf, sem,
                *, layer, n_steps):
    n_tok = y_ref.shape[0]

    def weight_copies(step, slot):
        experts = pl.ds(step * MOE_EPS, MOE_EPS)
        return (pltpu.make_async_copy(wg_hbm.at[layer, experts], wg_buf.at[slot], sem.at[0, slot]),
                pltpu.make_async_copy(wu_hbm.at[layer, experts], wu_buf.at[slot], sem.at[1, slot]),
                pltpu.make_async_copy(wd_hbm.at[layer, experts], wd_buf.at[slot], sem.at[2, slot]))

    for cp in weight_copies(0, 0):
        cp.start()

    x = y_ref[...] * (1.0 + mod_ref[4:5, :]) + mod_ref[3:4, :]
    x_hi, x_lo = _split(x)
    xb_ref[...] = x_hi
    w_hi, w_lo = _split(rwt_ref[...])
    logits = _dot_nt(w_hi, x_hi) + _dot_nt(w_lo, x_hi) + _dot_nt(w_hi, x_lo)
    scores = 1.0 / (1.0 + jnp.exp(-logits))
    gate_t = _route(scores, scores + rb_ref[...])
    gate_ref[...] = jnp.concatenate([gate_t, jnp.zeros_like(gate_t)], axis=0).T
    sgu = jnp.concatenate([sg_ref[...].astype(BF16), su_ref[...].astype(BF16)], axis=1)
    hs = _dot(x_hi, sgu)
    h_sh = (_silu(hs[:, :EXPERT_HIDDEN]) * hs[:, EXPERT_HIDDEN:]).astype(BF16)
    acc_ref[...] = _dot(h_sh, sd_ref[...].astype(BF16))

    def expert_group(step, slot):
        xb = xb_ref[...]
        g_rot = pltpu.roll(gate_ref[...], jnp.bitwise_and(LANES - step * MOE_EPS, LANES - 1), 1)
        acc = acc_ref[...]
        for p in range(MOE_EPS // 2):
            wg2 = jnp.concatenate([wg_buf[slot, 2 * p].astype(BF16), wg_buf[slot, 2 * p + 1].astype(BF16)], axis=1)
            wu2 = jnp.concatenate([wu_buf[slot, 2 * p].astype(BF16), wu_buf[slot, 2 * p + 1].astype(BF16)], axis=1)
            hg = _dot(xb, wg2)
            hu = _dot(xb, wu2)
            g2 = jnp.concatenate(
                [jnp.broadcast_to(g_rot[:, 2 * p:2 * p + 1], (n_tok, EXPERT_HIDDEN)),
                 jnp.broadcast_to(g_rot[:, 2 * p + 1:2 * p + 2], (n_tok, EXPERT_HIDDEN))], axis=1)
            h = (_silu(hg) * hu * g2).astype(BF16)
            wd2 = wd_buf[slot, 2 * p:2 * p + 2].reshape(2 * EXPERT_HIDDEN, D_MODEL).astype(BF16)
            acc = acc + _dot(h, wd2)
        acc_ref[...] = acc

    def two_groups(k, carry):
        step = 2 * k
        for cp in weight_copies(step, 0):
            cp.wait()
        for cp in weight_copies(step + 1, 1):
            cp.start()
        expert_group(step, 0)

        for cp in weight_copies(step + 1, 1):
            cp.wait()

        @pl.when(step + 2 < n_steps)
        def _():
            for cp in weight_copies(step + 2, 0):
                cp.start()

        expert_group(step + 1, 1)
        return carry

    lax.fori_loop(0, n_steps // 2, two_groups, 0)
    o_ref[...] = _post_norm(y_ref[...], acc_ref[...], mod_ref[5:6, :], lng_ref[...], lnb_ref[...])


def _moe(y, mods, layer, rwt, rb, exp_w_gate, exp_w_up, exp_w_down, sg, su, sd, ln_g, ln_b, seq_len):
    t = y.shape[0]
    tile = MOE_TILE
    n_steps = N_EXPERTS // MOE_EPS
    assert n_steps % 2 == 0
    mod_idx = (lambda i: (i * tile // seq_len, 0, 0)) if mods.shape[0] > 1 else (lambda i: (0, 0, 0))

    def full(shape):
        return pl.BlockSpec(shape, lambda i: (0,) * len(shape))
    hbm = pl.BlockSpec(memory_space=pl.ANY)
    return pl.pallas_call(
        functools.partial(_moe_kernel, layer=layer, n_steps=n_steps),
        grid=(t // tile,),
        in_specs=[pl.BlockSpec((tile, D_MODEL), lambda i: (i, 0)),
                  pl.BlockSpec((None, N_MOD, D_MODEL), mod_idx),
                  full((N_EXPERTS, D_MODEL)),
                  full((N_EXPERTS, 1)),
                  hbm, hbm, hbm,
                  full((D_MODEL, EXPERT_HIDDEN)),
                  full((D_MODEL, EXPERT_HIDDEN)),
                  full((EXPERT_HIDDEN, D_MODEL)),
                  full((1, D_MODEL)),
                  full((1, D_MODEL))],
        out_specs=pl.BlockSpec((tile, D_MODEL), lambda i: (i, 0)),
        out_shape=jax.ShapeDtypeStruct((t, D_MODEL), F32),
        scratch_shapes=[pltpu.VMEM((tile, D_MODEL), BF16),
                        pltpu.VMEM((tile, LANES), F32),
                        pltpu.VMEM((tile, D_MODEL), F32),
                        pltpu.VMEM((2, MOE_EPS, D_MODEL, EXPERT_HIDDEN), F32),
                        pltpu.VMEM((2, MOE_EPS, D_MODEL, EXPERT_HIDDEN), F32),
                        pltpu.VMEM((2, MOE_EPS, EXPERT_HIDDEN, D_MODEL), F32),
                        pltpu.SemaphoreType.DMA((3, 2))],
        compiler_params=_params(1),
        name="moe",
    )(y, mods, rwt, rb, exp_w_gate, exp_w_up, exp_w_down, sg, su, sd,
      ln_g.reshape(1, D_MODEL), ln_b.reshape(1, D_MODEL))


def kernel(x_prompt, x_sample, c, state_ret, cache_na_k, cache_na_v, cache_att_k, cache_att_v, c_ctx,
           w_mod, b_mod, ln_g, ln_b, w_in_even, w_out_even, ret_decay_logit, na_rpb,
           w_in_odd, w_out_odd, conv_w, q_norm_g, k_norm_g, router_w, router_b,
           exp_w_gate, exp_w_up, exp_w_down, sh_w_gate, sh_w_up, sh_w_down):
    yp = x_prompt.reshape(BATCH * SEQ, D_MODEL)
    ys = x_sample.reshape(DEC_BATCH * DEC_SEQ, D_MODEL)

    cond = jnp.zeros((COND_ROWS, D_MODEL), F32).at[0].set(c_ctx).at[1:1 + DEC_BATCH].set(c)
    mods = _adaln(cond, w_mod, b_mod).reshape(DEPTH, COND_ROWS, N_MOD, D_MODEL)

    out = {}
    for l in range(DEPTH):
        i = l // 2
        mp = mods[l, 0:1]
        ms = mods[l, 1:1 + DEC_BATCH]
        if l % 2 == 0:
            decay_rows = jnp.broadcast_to(ret_decay_logit[i].reshape(2 * RET_HEADS, 1), (2 * RET_HEADS, LANES))
            dts = (BF16, BF16, BF16, F32, BF16, BF16, BF16)
            yp, st, na_k, na_v, w_in, w_out = _even_ctx(yp, mp, w_in_even, w_out_even, i, decay_rows,
                                                        ln_g[l, 0], ln_b[l, 0])
            sp = _inproj(ys, ms, w_in, EVEN_SIZES, dts, DEC_SEQ)
            mix_s = _even_lat(*sp, state_ret,
                              cache_na_k[:, i].reshape(DEC_BATCH, PAST_LEN, NA_HEADS * NA_DH),
                              cache_na_v[:, i].reshape(DEC_BATCH, PAST_LEN, NA_HEADS * NA_DH),
                              i, decay_rows, na_rpb[i].reshape(-1))
            ys = _outproj(mix_s, ys, ms, w_out, ln_g[l, 0], ln_b[l, 0], DEC_SEQ)
            out.setdefault("st", []).append(st.reshape(BATCH, 2, RET_HEADS, RET_DK, RET_DV))
            out.setdefault("na_k", []).append(na_k.reshape(BATCH, SEQ, NA_HEADS, NA_DH))
            out.setdefault("na_v", []).append(na_v.reshape(BATCH, SEQ, NA_HEADS, NA_DH))
        else:
            qg = q_norm_g[i].reshape(1, ATT_DH)
            kg = k_norm_g[i].reshape(1, ATT_DH)
            yp, k_new, att_v, w_in, w_out = _odd_ctx(yp, mp, w_in_odd, w_out_odd, i, conv_w[i], qg, kg,
                                                     ln_g[l, 0], ln_b[l, 0])
            cos, sin = _rope_tables()
            ys = _odd_lat(ys, ms, w_in, cache_att_k, cache_att_v, i,
                          conv_w[i], qg, kg, cos, sin, w_out, ln_g[l, 0], ln_b[l, 0])
            out.setdefault("att_k", []).append(k_new.reshape(BATCH, SEQ, ATT_KV_HEADS, ATT_DH))
            out.setdefault("att_v", []).append(att_v.reshape(BATCH, SEQ, ATT_KV_HEADS, ATT_DH))
        rwt = router_w[l].T
        rb = router_b[l].reshape(N_EXPERTS, 1)
        moe_w = (rwt, rb, exp_w_gate, exp_w_up, exp_w_down, sh_w_gate[l], sh_w_up[l], sh_w_down[l],
                 ln_g[l, 1], ln_b[l, 1])
        yp = _moe(yp, mp, l, *moe_w, SEQ)
        ys = _moe(ys, ms, l, *moe_w, DEC_SEQ)

    return (yp.reshape(BATCH, SEQ, D_MODEL),
            ys.reshape(DEC_BATCH, DEC_SEQ, D_MODEL),
            jnp.stack(out["st"], axis=1),
            jnp.stack(out["na_k"], axis=1),
            jnp.stack(out["na_v"], axis=1),
            jnp.stack(out["att_k"], axis=1),
            jnp.stack(out["att_v"], axis=1))
```

```python
import functools

import jax
import jax.numpy as jnp
from jax import lax
from jax.experimental import pallas as pl
from jax.experimental.pallas import tpu as pltpu

F32 = jnp.float32
BF16 = jnp.bfloat16

D_MODEL = 1024
BATCH = 32
SEQ = 256
DEPTH = 2
DEC_BATCH = 4
DEC_SEQ = 1024
PAST_LEN = 512
GRID_W = 64
GRID_H = DEC_SEQ // GRID_W
MIX_HALF = D_MODEL // 2
RET_HEADS = 4
RET_DV = MIX_HALF // RET_HEADS
RET_DK = RET_DV // 2
NA_HEADS = 4
NA_DH = MIX_HALF // NA_HEADS
NA_WIN_R = 8
NA_WIN_C = 16
CONV_CH = MIX_HALF
ATT_HEADS = 4
ATT_KV_HEADS = 2
ATT_DH = MIX_HALF // ATT_HEADS
ROPE_THETA = 10000.0
N_EXPERTS = 64
EXPERT_HIDDEN = D_MODEL // 8
TOP_K = 8
N_GROUPS = 8
GROUP_SIZE = N_EXPERTS // N_GROUPS
TOPK_GROUPS = 4
ROUTED_SCALE = 2.5
DEEPNORM_ALPHA = (2 * DEPTH) ** 0.25
LN_EPS = 1e-5
RMS_EPS = 1e-6
EVEN_SIZES = (RET_HEADS * RET_DK, RET_HEADS * RET_DK, RET_HEADS * RET_DV, RET_HEADS * RET_DV,
              NA_HEADS * NA_DH, NA_HEADS * NA_DH, NA_HEADS * NA_DH)
ODD_SIZES = (CONV_CH, CONV_CH, CONV_CH, ATT_HEADS * ATT_DH, ATT_KV_HEADS * ATT_DH, ATT_KV_HEADS * ATT_DH)
N_MOD = 6
RPB_ROWS = 2 * NA_WIN_R - 1
RPB_COLS = 2 * NA_WIN_C - 1

LANES = 128
COND_ROWS = 16
V7X_VMEM_LIMIT = 56 * 1024 * 1024

PROJ_TILE = 1024
LAT_QBLK = 512
MOE_TILE = 1024
MOE_EPS = 4
ADALN_TN = 1536


def _params(n_axes, vmem_limit=V7X_VMEM_LIMIT):
    return pltpu.CompilerParams(dimension_semantics=("arbitrary",) * n_axes,
                                vmem_limit_bytes=vmem_limit)


def _dot(a, b):
    return jnp.dot(a, b, preferred_element_type=F32)


def _dot_nt(a, b):
    return lax.dot_general(a, b, (((1,), (1,)), ((), ())), preferred_element_type=F32)


def _split(a):
    hi = a.astype(BF16)
    lo = (a - hi.astype(F32)).astype(BF16)
    return hi, lo


def _silu(x):
    return x / (1.0 + jnp.exp(-x))


def _log_sigmoid(x):
    return jnp.minimum(x, 0.0) - jnp.log(1.0 + jnp.exp(-jnp.abs(x)))


def _layer_norm(z, g, b):
    mu = jnp.mean(z, axis=-1, keepdims=True)
    zc = z - mu
    var = jnp.mean(zc * zc, axis=-1, keepdims=True)
    return zc * lax.rsqrt(var + LN_EPS) * g + b


def _post_norm(x, y, gate, g, b):
    return _layer_norm(DEEPNORM_ALPHA * x + (1.0 + gate) * y, g, b)


def _head_norm(x):
    mu = jnp.mean(x, axis=-1, keepdims=True)
    xc = x - mu
    var = jnp.mean(xc * xc, axis=-1, keepdims=True)
    return xc * lax.rsqrt(var + LN_EPS)


def _rms_norm(x, g):
    return x * lax.rsqrt(jnp.mean(x * x, axis=-1, keepdims=True) + RMS_EPS) * g


def _softmax_pv(parts):
    m = None
    for s, _ in parts:
        mi = jnp.max(s, axis=-1, keepdims=True)
        m = mi if m is None else jnp.maximum(m, mi)
    l = None
    o = None
    for s, v in parts:
        p = jnp.exp(s - m)
        li = jnp.sum(p, axis=-1, keepdims=True)
        oi = _dot(p.astype(BF16), v)
        l = li if l is None else l + li
        o = oi if o is None else o + oi
    return o / l


def _adaln_kernel(c_ref, w_ref, b_ref, o_ref):
    a_hi, a_lo = _split(_silu(c_ref[...]))
    w_hi, w_lo = _split(w_ref[...])
    o_ref[...] = _dot(a_hi, w_hi) + _dot(a_lo, w_hi) + _dot(a_hi, w_lo) + b_ref[...]


def _adaln(cond, w_mod, b_mod):
    n = w_mod.shape[-1]
    return pl.pallas_call(
        _adaln_kernel,
        grid=(DEPTH, n // ADALN_TN),
        in_specs=[pl.BlockSpec((COND_ROWS, D_MODEL), lambda l, j: (0, 0)),
                  pl.BlockSpec((None, D_MODEL, ADALN_TN), lambda l, j: (l, 0, j)),
                  pl.BlockSpec((None, 1, ADALN_TN), lambda l, j: (l, 0, j))],
        out_specs=pl.BlockSpec((None, COND_ROWS, ADALN_TN), lambda l, j: (l, 0, j)),
        out_shape=jax.ShapeDtypeStruct((DEPTH, COND_ROWS, n), F32),
        compiler_params=_params(2),
        name="adaln",
    )(cond, w_mod, b_mod.reshape(DEPTH, 1, n))


def _inproj_kernel(x_ref, mod_ref, w_ref, *out_refs, sizes):
    shift = mod_ref[0:1, :]
    scale = mod_ref[1:2, :]
    h = (x_ref[...] * (1.0 + scale) + shift).astype(BF16)
    y = _dot(h, w_ref[...])
    for o_ref, off, n in zip(out_refs, _offsets(sizes), sizes):
        o_ref[...] = y[:, off:off + n].astype(o_ref.dtype)


def _inproj(x, mods, w_bf16, sizes, dtypes, seq_len):
    t = x.shape[0]
    tile = PROJ_TILE
    mod_idx = (lambda i: (i * tile // seq_len, 0, 0)) if mods.shape[0] > 1 else (lambda i: (0, 0, 0))
    return pl.pallas_call(
        functools.partial(_inproj_kernel, sizes=sizes),
        grid=(t // tile,),
        in_specs=[pl.BlockSpec((tile, D_MODEL), lambda i: (i, 0)),
                  pl.BlockSpec((None, N_MOD, D_MODEL), mod_idx),
                  pl.BlockSpec(w_bf16.shape, lambda i: (0, 0))],
        out_specs=[pl.BlockSpec((tile, n), lambda i: (i, 0)) for n in sizes],
        out_shape=[jax.ShapeDtypeStruct((t, n), dt) for n, dt in zip(sizes, dtypes)],
        compiler_params=_params(1),
        name="inproj",
    )(x, mods, w_bf16)


def _outproj_kernel(mix_ref, x_ref, mod_ref, w_ref, g_ref, b_ref, o_ref):
    y = _dot(mix_ref[...], w_ref[...])
    o_ref[...] = _post_norm(x_ref[...], y, mod_ref[2:3, :], g_ref[...], b_ref[...])


def _outproj(mix, x, mods, w_bf16, ln_g, ln_b, seq_len):
    t = x.shape[0]
    tile = PROJ_TILE
    mod_idx = (lambda i: (i * tile // seq_len, 0, 0)) if mods.shape[0] > 1 else (lambda i: (0, 0, 0))
    return pl.pallas_call(
        _outproj_kernel,
        grid=(t // tile,),
        in_specs=[pl.BlockSpec((tile, D_MODEL), lambda i: (i, 0)),
                  pl.BlockSpec((tile, D_MODEL), lambda i: (i, 0)),
                  pl.BlockSpec((None, N_MOD, D_MODEL), mod_idx),
                  pl.BlockSpec((D_MODEL, D_MODEL), lambda i: (0, 0)),
                  pl.BlockSpec((1, D_MODEL), lambda i: (0, 0)),
                  pl.BlockSpec((1, D_MODEL), lambda i: (0, 0))],
        out_specs=pl.BlockSpec((tile, D_MODEL), lambda i: (i, 0)),
        out_shape=jax.ShapeDtypeStruct((t, D_MODEL), F32),
        compiler_params=_params(1),
        name="outproj",
    )(mix, x, mods, w_bf16, ln_g.reshape(1, D_MODEL), ln_b.reshape(1, D_MODEL))


def _decay_matrix(length, lgf, lgb, row0=0, rows=None):
    rows = length if rows is None else rows
    ii = lax.broadcasted_iota(jnp.int32, (rows, length), 0) + row0
    jj = lax.broadcasted_iota(jnp.int32, (rows, length), 1)
    rel = (ii - jj).astype(F32)
    fwd = jnp.where(rel >= 0.0, jnp.exp(lgf * jnp.maximum(rel, 0.0)), 0.0)
    bwd = jnp.where(rel <= 0.0, jnp.exp(lgb * jnp.maximum(-rel, 0.0)), 0.0)
    return fwd + bwd


def _retention_head(q, k, v, lgf, lgb, s0f, s0b, dmat=None):
    length = q.shape[0]
    s = _dot_nt(q.astype(BF16), k.astype(BF16))
    if dmat is None:
        dmat = _decay_matrix(length, lgf, lgb)
    if dmat.dtype == BF16:
        o = _dot(s.astype(BF16) * dmat, v)
    else:
        o = _dot((s * dmat).astype(BF16), v)
    if s0f is not None:
        pos = lax.broadcasted_iota(jnp.int32, (length, 1), 0).astype(F32)
        qf = q * jnp.exp(lgf * (pos + 1.0))
        qb = q * jnp.exp(lgb * (length - pos))
        o = o + _dot(qf.astype(BF16), s0f.astype(BF16)) + _dot(qb.astype(BF16), s0b.astype(BF16))
    return o


def _offsets(sizes):
    return [sum(sizes[:k]) for k in range(len(sizes))]


def _project(x_ref, mod_ref, wi_ref, dst_ref):
    h = (x_ref[...] * (1.0 + mod_ref[1:2, :]) + mod_ref[0:1, :]).astype(BF16)
    dst_ref[...] = _dot(h, wi_ref[...])


def _cast_rows(src_ref, dst_ref):
    def body(r, carry):
        rows = pl.ds(pl.multiple_of(r * LANES, LANES), LANES)
        dst_ref[rows, :] = src_ref[rows, :].astype(BF16)
        return carry
    lax.fori_loop(0, src_ref.shape[0] // LANES, body, 0)


def _project_next(xn_ref, xc_ref, mod_ref, wif_ref, wof_ref, wi_ref, wo_ref, nxt_ref, cur_ref):
    @pl.when(pl.program_id(0) == 0)
    def _():
        _cast_rows(wif_ref, wi_ref)
        _cast_rows(wof_ref, wo_ref)
        _project(xc_ref, mod_ref, wi_ref, cur_ref)

    _project(xn_ref, mod_ref, wi_ref, nxt_ref)
    return cur_ref


def _even_ctx_kernel(xn_ref, xc_ref, mod_ref, wif_ref, dl_ref, wof_ref, lng_ref, lnb_ref,
                     y_ref, st_ref, nak_ref, nav_ref, wi_ref, wo_ref,
                     nxt_ref, cur_ref, mix_ref, dmat_ref, kdec_ref):
    length = xc_ref.shape[0]

    @pl.when(pl.program_id(0) == 0)
    def _():
        lg = _log_sigmoid(dl_ref[...])
        pos = lax.broadcasted_iota(jnp.int32, (length, 1), 0).astype(F32)
        for h in range(RET_HEADS):
            lgf = lg[h:h + 1, 0:1]
            lgb = lg[RET_HEADS + h:RET_HEADS + h + 1, 0:1]
            dmat_ref[h] = _decay_matrix(length, lgf, lgb)
            kdec_ref[h] = jnp.broadcast_to(jnp.exp(lgf * (length - 1.0 - pos)), (length, RET_DK))
            kdec_ref[RET_HEADS + h] = jnp.broadcast_to(jnp.exp(lgb * pos), (length, RET_DK))

    cur = _project_next(xn_ref, xc_ref, mod_ref, wif_ref, wof_ref, wi_ref, wo_ref, nxt_ref, cur_ref)
    o_qr, o_kr, o_vr, o_gr, o_qn, o_kn, o_vn = _offsets(EVEN_SIZES)
    eye = jnp.where(lax.broadcasted_iota(jnp.int32, (2 * RET_DK, 2 * RET_DK), 0)
                    == lax.broadcasted_iota(jnp.int32, (2 * RET_DK, 2 * RET_DK), 1), 1.0, 0.0).astype(BF16)
    for h in range(RET_HEADS):
        q = cur[:, o_qr + h * RET_DK:o_qr + (h + 1) * RET_DK]
        k = cur[:, o_kr + h * RET_DK:o_kr + (h + 1) * RET_DK] * (RET_DK ** -0.5)
        v = cur[:, o_vr + h * RET_DV:o_vr + (h + 1) * RET_DV].astype(BF16)
        o = _retention_head(q, k, v, None, None, None, None, dmat=dmat_ref[h])
        k_dec = jnp.concatenate([k * kdec_ref[h], k * kdec_ref[RET_HEADS + h]], axis=1).astype(BF16)
        states = _dot(_dot_nt(eye, k_dec).astype(BF16), v)
        st_ref[h] = states[:RET_DK]
        st_ref[RET_HEADS + h] = states[RET_DK:]
        g = cur[:, o_gr + h * RET_DV:o_gr + (h + 1) * RET_DV]
        mix_ref[:, h * RET_DV:(h + 1) * RET_DV] = (_head_norm(o) * _silu(g)).astype(mix_ref.dtype)
    base = RET_HEADS * RET_DV
    for h in range(NA_HEADS):
        q = cur[:, o_qn + h * NA_DH:o_qn + (h + 1) * NA_DH].astype(BF16)
        k = cur[:, o_kn + h * NA_DH:o_kn + (h + 1) * NA_DH]
        v = cur[:, o_vn + h * NA_DH:o_vn + (h + 1) * NA_DH]
        nak_ref[:, h, :] = k
        nav_ref[:, h, :] = v
        s = _dot_nt(q, k.astype(BF16)) * (NA_DH ** -0.5)
        mix_ref[:, base + h * NA_DH:base + (h + 1) * NA_DH] = _softmax_pv([(s, v.astype(BF16))]).astype(mix_ref.dtype)
    y = _dot(mix_ref[...], wo_ref[...])
    y_ref[...] = _post_norm(xc_ref[...], y, mod_ref[2:3, :], lng_ref[...], lnb_ref[...])
    cur_ref[...] = nxt_ref[...]


def _ctx_layer_specs(n_proj, idx):
    def nxt(n):
        return pl.BlockSpec((SEQ, n), lambda i: (jnp.minimum(i + 1, BATCH - 1), 0))

    def cur(shape):
        return pl.BlockSpec(shape, lambda i: (i,) + (0,) * (len(shape) - 1))

    def full(shape):
        return pl.BlockSpec(shape, lambda i: (0,) * len(shape))

    def weight(n):
        return pl.BlockSpec((None, D_MODEL, n), lambda i: (idx, 0, 0), pipeline_mode=pl.Buffered(1))
    head = [nxt(D_MODEL), cur((SEQ, D_MODEL)), full((None, N_MOD, D_MODEL)), weight(n_proj)]
    tail = [weight(D_MODEL), full((1, D_MODEL)), full((1, D_MODEL))]
    w_specs = [full((D_MODEL, n_proj)), full((D_MODEL, D_MODEL))]
    w_shapes = [jax.ShapeDtypeStruct((D_MODEL, n_proj), BF16), jax.ShapeDtypeStruct((D_MODEL, D_MODEL), BF16)]
    return head, tail, cur, full, w_specs, w_shapes


def _even_ctx(x, mods, w_in, w_out, idx, decay_rows, ln_g, ln_b):
    head, tail, cur, full, w_specs, w_shapes = _ctx_layer_specs(sum(EVEN_SIZES), idx)
    return pl.pallas_call(
        _even_ctx_kernel,
        grid=(BATCH,),
        in_specs=head + [full((2 * RET_HEADS, LANES))] + tail,
        out_specs=[cur((SEQ, D_MODEL)),
                   cur((None, 2 * RET_HEADS, RET_DK, RET_DV)),
                   cur((SEQ, NA_HEADS, NA_DH)), cur((SEQ, NA_HEADS, NA_DH))] + w_specs,
        out_shape=[jax.ShapeDtypeStruct((BATCH * SEQ, D_MODEL), F32),
                   jax.ShapeDtypeStruct((BATCH, 2 * RET_HEADS, RET_DK, RET_DV), F32),
                   jax.ShapeDtypeStruct((BATCH * SEQ, NA_HEADS, NA_DH), F32),
                   jax.ShapeDtypeStruct((BATCH * SEQ, NA_HEADS, NA_DH), F32)] + w_shapes,
        scratch_shapes=[pltpu.VMEM((SEQ, sum(EVEN_SIZES)), F32),
                        pltpu.VMEM((SEQ, sum(EVEN_SIZES)), F32),
                        pltpu.VMEM((SEQ, D_MODEL), BF16),
                        pltpu.VMEM((RET_HEADS, SEQ, SEQ), F32),
                        pltpu.VMEM((2 * RET_HEADS, SEQ, RET_DK), F32)],
        compiler_params=_params(1),
        name="even_ctx",
    )(x, x, mods, w_in, decay_rows, w_out, ln_g.reshape(1, D_MODEL), ln_b.reshape(1, D_MODEL))


def _build_rpb_tiles(rpb_ref, tile_ref):
    qc = lax.broadcasted_iota(jnp.int32, (GRID_W, LANES), 0)
    kc = lax.broadcasted_iota(jnp.int32, (GRID_W, LANES), 1) % GRID_W
    diff = kc - qc + (NA_WIN_C - 1)
    start = jnp.clip(qc - NA_WIN_C // 2, 0, GRID_W - NA_WIN_C)
    win = (kc >= start) & (kc < start + NA_WIN_C)

    def body(idx, carry):
        t = jnp.zeros((GRID_W, LANES), F32)
        for d in range(RPB_COLS):
            t = jnp.where(diff == d, rpb_ref[idx * RPB_COLS + d], t)
        tile_ref[idx] = jnp.where(win, t, -jnp.inf)
        return carry

    lax.fori_loop(0, NA_HEADS * RPB_ROWS, body, 0)


NA_Q_ROWS = 8
NA_KEY_ROWS = 12


def _band_start(qr):
    rows_w = min(NA_WIN_R, GRID_H)
    return min(max(qr - rows_w // 2, 0), GRID_H - rows_w)


def _key_window(qrow0):
    w0 = min(_band_start(qrow0), GRID_H - NA_KEY_ROWS)
    assert _band_start(qrow0 + NA_Q_ROWS - 1) + min(NA_WIN_R, GRID_H) <= w0 + NA_KEY_ROWS
    return w0


def _build_bias(tile_ref, bias_ref, head, qrow0):
    left = lax.broadcasted_iota(jnp.int32, (GRID_W, LANES), 1) < GRID_W
    neg = jnp.full((GRID_W, LANES), -jnp.inf, F32)
    rows_w = min(NA_WIN_R, GRID_H)
    w0 = _key_window(qrow0)
    for dq in range(NA_Q_ROWS):
        qr = qrow0 + dq
        rs = _band_start(qr)

        def tile(kr):
            if rs <= kr < rs + rows_w:
                return tile_ref[head * RPB_ROWS + kr - qr + NA_WIN_R - 1]
            return neg

        pieces = [jnp.where(left, tile(w0 + 2 * a), tile(w0 + 2 * a + 1)) for a in range(NA_KEY_ROWS // 2)]
        bias_ref[dq * GRID_W:(dq + 1) * GRID_W, :] = jnp.concatenate(pieces, axis=1)


def _even_lat_kernel(qr_ref, kr_ref, vr_ref, gr_ref, qn_ref, kn_ref, vn_ref, s0_ref, ck_ref, cv_ref,
                     dl_ref, rpb_ref, mix_ref, tile_ref, bias_ref, dmat_ref):
    length = qr_ref.shape[0]
    lg = _log_sigmoid(dl_ref[...])

    @pl.when(pl.program_id(0) == 0)
    def _():
        _build_rpb_tiles(rpb_ref, tile_ref)
        for h in range(RET_HEADS):
            lgf = lg[h:h + 1, 0:1]
            lgb = lg[RET_HEADS + h:RET_HEADS + h + 1, 0:1]

            def rows_body(r, carry, h=h, lgf=lgf, lgb=lgb):
                r0 = pl.multiple_of(r * LANES, LANES)
                dmat_ref[h, pl.ds(r0, LANES), :] = _decay_matrix(length, lgf, lgb, r0, LANES).astype(BF16)
                return carry

            lax.fori_loop(0, length // LANES, rows_body, 0)

    for h in range(RET_HEADS):
        lgf = lg[h:h + 1, 0:1]
        lgb = lg[RET_HEADS + h:RET_HEADS + h + 1, 0:1]
        q = qr_ref[:, h * RET_DK:(h + 1) * RET_DK].astype(F32)
        k = kr_ref[:, h * RET_DK:(h + 1) * RET_DK].astype(F32) * (RET_DK ** -0.5)
        v = vr_ref[:, h * RET_DV:(h + 1) * RET_DV]
        o = _retention_head(q, k, v, lgf, lgb, s0_ref[0, h], s0_ref[1, h], dmat=dmat_ref[h])
        g = gr_ref[:, h * RET_DV:(h + 1) * RET_DV]
        mix_ref[:, h * RET_DV:(h + 1) * RET_DV] = (_head_norm(o) * _silu(g)).astype(mix_ref.dtype)
    base = RET_HEADS * RET_DV
    scale = NA_DH ** -0.5
    for h in range(NA_HEADS):
        sl = slice(h * NA_DH, (h + 1) * NA_DH)
        kc = ck_ref[:, sl].astype(BF16)
        vc = cv_ref[:, sl].astype(BF16)
        for qrow0 in range(0, GRID_H, NA_Q_ROWS):
            _build_bias(tile_ref, bias_ref, h, qrow0)
            rows = slice(qrow0 * GRID_W, (qrow0 + NA_Q_ROWS) * GRID_W)
            k0 = _key_window(qrow0) * GRID_W
            keys = slice(k0, k0 + NA_KEY_ROWS * GRID_W)
            q = qn_ref[rows, sl]
            s_band = _dot_nt(q, kn_ref[keys, sl]) * scale + bias_ref[...]
            s_ctx = _dot_nt(q, kc) * scale
            o = _softmax_pv([(s_band, vn_ref[keys, sl]), (s_ctx, vc)])
            mix_ref[rows, base + h * NA_DH:base + (h + 1) * NA_DH] = o.astype(mix_ref.dtype)


def _even_lat(qr, kr, vr, gr, qn, kn, vn, state, cache_k, cache_v, idx, decay_rows, rpb_flat):
    def seq_spec(n):
        return pl.BlockSpec((DEC_SEQ, n), lambda i: (i, 0))
    cache_spec = pl.BlockSpec((None, PAST_LEN, NA_HEADS * NA_DH), lambda i: (i, 0, 0))
    return pl.pallas_call(
        _even_lat_kernel,
        grid=(DEC_BATCH,),
        in_specs=[seq_spec(n) for n in EVEN_SIZES]
        + [pl.BlockSpec((None, None, 2, RET_HEADS, RET_DK, RET_DV), lambda i: (i, idx, 0, 0, 0, 0)),
           cache_spec, cache_spec,
           pl.BlockSpec((2 * RET_HEADS, LANES), lambda i: (0, 0)),
           pl.BlockSpec(memory_space=pltpu.SMEM)],
        out_specs=seq_spec(D_MODEL),
        out_shape=jax.ShapeDtypeStruct((DEC_BATCH * DEC_SEQ, D_MODEL), BF16),
        scratch_shapes=[pltpu.VMEM((NA_HEADS * RPB_ROWS, GRID_W, LANES), F32),
                        pltpu.VMEM((NA_Q_ROWS * GRID_W, NA_KEY_ROWS * GRID_W), F32),
                        pltpu.VMEM((RET_HEADS, DEC_SEQ, DEC_SEQ), BF16)],
        compiler_params=_params(1),
        name="even_lat",
    )(qr, kr, vr, gr, qn, kn, vn, state, cache_k, cache_v, decay_rows, rpb_flat)


def _gated_conv(bg, cg, u, w_ref):
    length, ch = u.shape
    z = cg * u
    row = lax.broadcasted_iota(jnp.int32, (length, ch), 0)
    z_prev = jnp.where(row == 0, 0.0, pltpu.roll(z, 1, 0))
    z_next = jnp.where(row == length - 1, 0.0, pltpu.roll(z, length - 1, 0))
    return bg * (z_prev * w_ref[0:1, :] + z * w_ref[1:2, :] + z_next * w_ref[2:3, :])


def _odd_ctx_kernel(xn_ref, xc_ref, mod_ref, wif_ref, w_ref, qg_ref, kg_ref, wof_ref, lng_ref, lnb_ref,
                    y_ref, ko_ref, vo_ref, wi_ref, wo_ref, nxt_ref, cur_ref, mix_ref):
    cur = _project_next(xn_ref, xc_ref, mod_ref, wif_ref, wof_ref, wi_ref, wo_ref, nxt_ref, cur_ref)
    o_bg, o_cg, o_u, o_q, o_k, o_v = _offsets(ODD_SIZES)
    conv = _gated_conv(cur[:, o_bg:o_bg + CONV_CH], cur[:, o_cg:o_cg + CONV_CH], cur[:, o_u:o_u + CONV_CH], w_ref)
    mix_ref[:, 0:CONV_CH] = conv.astype(mix_ref.dtype)
    group = ATT_HEADS // ATT_KV_HEADS
    scale = ATT_DH ** -0.5
    for kv in range(ATT_KV_HEADS):
        k = _rms_norm(cur[:, o_k + kv * ATT_DH:o_k + (kv + 1) * ATT_DH], kg_ref[...])
        v = cur[:, o_v + kv * ATT_DH:o_v + (kv + 1) * ATT_DH]
        ko_ref[:, kv, :] = k
        vo_ref[:, kv, :] = v
        k = k.astype(BF16)
        v = v.astype(BF16)
        for g in range(group):
            h = kv * group + g
            q = _rms_norm(cur[:, o_q + h * ATT_DH:o_q + (h + 1) * ATT_DH], qg_ref[...]).astype(BF16)
            o = _softmax_pv([(_dot_nt(q, k) * scale, v)])
            mix_ref[:, CONV_CH + h * ATT_DH:CONV_CH + (h + 1) * ATT_DH] = o.astype(mix_ref.dtype)
    y = _dot(mix_ref[...], wo_ref[...])
    y_ref[...] = _post_norm(xc_ref[...], y, mod_ref[2:3, :], lng_ref[...], lnb_ref[...])
    cur_ref[...] = nxt_ref[...]


def _odd_ctx(x, mods, w_in, w_out, idx, conv_w, qn_g, kn_g, ln_g, ln_b):
    head, tail, cur, full, w_specs, w_shapes = _ctx_layer_specs(sum(ODD_SIZES), idx)
    kv_shape = (SEQ, ATT_KV_HEADS, ATT_DH)
    return pl.pallas_call(
        _odd_ctx_kernel,
        grid=(BATCH,),
        in_specs=head + [full((3, CONV_CH)), full((1, ATT_DH)), full((1, ATT_DH))] + tail,
        out_specs=[cur((SEQ, D_MODEL)), cur(kv_shape), cur(kv_shape)] + w_specs,
        out_shape=[jax.ShapeDtypeStruct((BATCH * SEQ, D_MODEL), F32),
                   jax.ShapeDtypeStruct((BATCH * SEQ, ATT_KV_HEADS, ATT_DH), F32),
                   jax.ShapeDtypeStruct((BATCH * SEQ, ATT_KV_HEADS, ATT_DH), F32)] + w_shapes,
        scratch_shapes=[pltpu.VMEM((SEQ, sum(ODD_SIZES)), F32),
                        pltpu.VMEM((SEQ, sum(ODD_SIZES)), F32),
                        pltpu.VMEM((SEQ, D_MODEL), BF16)],
        compiler_params=_params(1),
        name="odd_ctx",
    )(x, x, mods, w_in, conv_w, qn_g, kn_g, w_out, ln_g.reshape(1, D_MODEL), ln_b.reshape(1, D_MODEL))


def _rope(x, cos, sin, even_lane):
    swapped = jnp.where(even_lane, pltpu.roll(x, LANES - 1, 1), pltpu.roll(x, 1, 1))
    return x * cos + swapped * sin


def _odd_lat_kernel(x_ref, mod_ref, wi_ref, ck_ref, cv_ref, w_ref, qg_ref, kg_ref, cos_ref, sin_ref,
                    wo_ref, lng_ref, lnb_ref, y_ref, proj_ref, qb_ref, kb_ref, vb_ref, mix_ref):
    length = x_ref.shape[0]
    h_in = (x_ref[...] * (1.0 + mod_ref[1:2, :]) + mod_ref[0:1, :]).astype(BF16)
    proj_ref[...] = _dot(h_in, wi_ref[...])
    o_bg, o_cg, o_u, o_q, o_k, o_v = _offsets(ODD_SIZES)
    conv = _gated_conv(proj_ref[:, o_bg:o_bg + CONV_CH], proj_ref[:, o_cg:o_cg + CONV_CH],
                       proj_ref[:, o_u:o_u + CONV_CH], w_ref)
    mix_ref[:, 0:CONV_CH] = conv.astype(mix_ref.dtype)
    group = ATT_HEADS // ATT_KV_HEADS
    scale = ATT_DH ** -0.5
    even_lane = lax.broadcasted_iota(jnp.int32, (length, ATT_DH), 1) % 2 == 0
    cos = cos_ref[...]
    sin = sin_ref[...]
    for kv in range(ATT_KV_HEADS):
        sl = slice(kv * ATT_DH, (kv + 1) * ATT_DH)
        k = _rms_norm(proj_ref[:, o_k + kv * ATT_DH:o_k + (kv + 1) * ATT_DH], kg_ref[...])
        kb_ref[0:length, sl] = _rope(k, cos, sin, even_lane).astype(BF16)
        vb_ref[0:length, sl] = proj_ref[:, o_v + kv * ATT_DH:o_v + (kv + 1) * ATT_DH].astype(BF16)
        kb_ref[length:, sl] = ck_ref[:, kv, :].astype(BF16)
        vb_ref[length:, sl] = cv_ref[:, kv, :].astype(BF16)
    for h in range(ATT_HEADS):
        q = _rms_norm(proj_ref[:, o_q + h * ATT_DH:o_q + (h + 1) * ATT_DH], qg_ref[...])
        qb_ref[:, h * ATT_DH:(h + 1) * ATT_DH] = _rope(q, cos, sin, even_lane).astype(BF16)

    def q_block(b, carry):
        rows = pl.ds(pl.multiple_of(b * LAT_QBLK, LAT_QBLK), LAT_QBLK)
        for h in range(ATT_HEADS):
            sl = slice((h // group) * ATT_DH, (h // group + 1) * ATT_DH)
            q = qb_ref[rows, h * ATT_DH:(h + 1) * ATT_DH]
            o = _softmax_pv([(_dot_nt(q, kb_ref[:, sl]) * scale, vb_ref[:, sl])])
            mix_ref[rows, CONV_CH + h * ATT_DH:CONV_CH + (h + 1) * ATT_DH] = o.astype(mix_ref.dtype)
        return carry

    lax.fori_loop(0, length // LAT_QBLK, q_block, 0)
    y = _dot(mix_ref[...], wo_ref[...])
    y_ref[...] = _post_norm(x_ref[...], y, mod_ref[2:3, :], lng_ref[...], lnb_ref[...])


def _odd_lat(x, mods, w_in, cache_k, cache_v, idx, conv_w, qn_g, kn_g, cos, sin, w_out, ln_g, ln_b):
    def full(shape):
        return pl.BlockSpec(shape, lambda i: (0,) * len(shape))

    def const(shape):
        return pl.BlockSpec(shape, lambda i: (0,) * len(shape), pipeline_mode=pl.Buffered(1))
    kv_w = ATT_KV_HEADS * ATT_DH
    cache_spec = pl.BlockSpec((None, None, PAST_LEN, ATT_KV_HEADS, ATT_DH), lambda i: (i, idx, 0, 0, 0))
    return pl.pallas_call(
        _odd_lat_kernel,
        grid=(DEC_BATCH,),
        in_specs=[pl.BlockSpec((DEC_SEQ, D_MODEL), lambda i: (i, 0)),
                  pl.BlockSpec((None, N_MOD, D_MODEL), lambda i: (i, 0, 0)),
                  const((D_MODEL, sum(ODD_SIZES))),
                  cache_spec, cache_spec, full((3, CONV_CH)), full((1, ATT_DH)), full((1, ATT_DH)),
                  const((DEC_SEQ, ATT_DH)), const((DEC_SEQ, ATT_DH)),
                  const((D_MODEL, D_MODEL)), full((1, D_MODEL)), full((1, D_MODEL))],
        out_specs=pl.BlockSpec((DEC_SEQ, D_MODEL), lambda i: (i, 0)),
        out_shape=jax.ShapeDtypeStruct((DEC_BATCH * DEC_SEQ, D_MODEL), F32),
        scratch_shapes=[pltpu.VMEM((DEC_SEQ, sum(ODD_SIZES)), F32),
                        pltpu.VMEM((DEC_SEQ, ATT_HEADS * ATT_DH), BF16),
                        pltpu.VMEM((DEC_SEQ + PAST_LEN, kv_w), BF16),
                        pltpu.VMEM((DEC_SEQ + PAST_LEN, kv_w), BF16),
                        pltpu.VMEM((DEC_SEQ, D_MODEL), BF16)],
        compiler_params=_params(1),
        name="odd_lat",
    )(x, mods, w_in, cache_k, cache_v, conv_w, qn_g, kn_g, cos, sin, w_out,
      ln_g.reshape(1, D_MODEL), ln_b.reshape(1, D_MODEL))


def _rope_tables():
    t = jnp.arange(DEC_SEQ)
    row = (t // GRID_W).astype(F32)
    col = (t % GRID_W).astype(F32)
    n_freq = ATT_DH // 4
    freqs = ROPE_THETA ** (-jnp.arange(n_freq, dtype=F32) / n_freq)
    ang = jnp.concatenate([row[:, None] * freqs, col[:, None] * freqs], axis=-1)
    cos = jnp.repeat(jnp.cos(ang), 2, axis=-1)
    sin = jnp.stack([-jnp.sin(ang), jnp.sin(ang)], axis=-1).reshape(DEC_SEQ, ATT_DH)
    return cos, sin


def _route(scores, sel):
    n_tok = sel.shape[1]
    neg = -jnp.inf
    sub = lax.broadcasted_iota(jnp.int32, (GROUP_SIZE, n_tok), 0).astype(F32)
    blocks = [sel[g * GROUP_SIZE:(g + 1) * GROUP_SIZE, :] for g in range(N_GROUPS)]
    grp = []
    for blk in blocks:
        m1 = jnp.max(blk, axis=0, keepdims=True)
        i1 = jnp.min(jnp.where(blk == m1, sub, float(GROUP_SIZE)), axis=0, keepdims=True)
        m2 = jnp.max(jnp.where(sub == i1, neg, blk), axis=0, keepdims=True)
        grp.append(m1 + m2)
    masked = []
    for g in range(N_GROUPS):
        ahead = jnp.zeros((1, n_tok), F32)
        for o in range(N_GROUPS):
            if o == g:
                continue
            wins = grp[o] >= grp[g] if o < g else grp[o] > grp[g]
            ahead = ahead + jnp.where(wins, 1.0, 0.0)
        masked.append(jnp.where(ahead < float(TOPK_GROUPS), blocks[g], neg))
    val = jnp.concatenate(masked, axis=0)
    row = lax.broadcasted_iota(jnp.int32, (N_EXPERTS, n_tok), 0).astype(F32)
    w = jnp.zeros((N_EXPERTS, n_tok), F32)
    for _ in range(TOP_K):
        m = jnp.max(val, axis=0, keepdims=True)
        idx = jnp.min(jnp.where(val == m, row, float(N_EXPERTS)), axis=0, keepdims=True)
        pick = row == idx
        w = jnp.where(pick, scores, w)
        val = jnp.where(pick, neg, val)
    return w / jnp.sum(w, axis=0, keepdims=True) * ROUTED_SCALE


def _moe_kernel(y_ref, mod_ref, rwt_ref, rb_ref, wg_hbm, wu_hbm, wd_hbm, sg_ref, su_ref, sd_ref,
                lng_ref, lnb_ref, o_ref, xb_ref, gate_ref, acc_ref, wg_buf, wu_buf, wd_buf, sem,
                *, layer, n_steps):
    n_tok = y_ref.shape[0]

    def weight_copies(step, slot):
        experts = pl.ds(step * MOE_EPS, MOE_EPS)
        return (pltpu.make_async_copy(wg_hbm.at[layer, experts], wg_buf.at[slot], sem.at[0, slot]),
                pltpu.make_async_copy(wu_hbm.at[layer, experts], wu_buf.at[slot], sem.at[1, slot]),
                pltpu.make_async_copy(wd_hbm.at[layer, experts], wd_buf.at[slot], sem.at[2, slot]))

    for cp in weight_copies(0, 0):
        cp.start()

    x = y_ref[...] * (1.0 + mod_ref[4:5, :]) + mod_ref[3:4, :]
    x_hi, x_lo = _split(x)
    xb_ref[...] = x_hi
    w_hi, w_lo = _split(rwt_ref[...])
    logits = _dot_nt(w_hi, x_hi) + _dot_nt(w_lo, x_hi) + _dot_nt(w_hi, x_lo)
    scores = 1.0 / (1.0 + jnp.exp(-logits))
    gate_t = _route(scores, scores + rb_ref[...])
    gate_ref[...] = jnp.concatenate([gate_t, jnp.zeros_like(gate_t)], axis=0).T
    sgu = jnp.concatenate([sg_ref[...].astype(BF16), su_ref[...].astype(BF16)], axis=1)
    hs = _dot(x_hi, sgu)
    h_sh = (_silu(hs[:, :EXPERT_HIDDEN]) * hs[:, EXPERT_HIDDEN:]).astype(BF16)
    acc_ref[...] = _dot(h_sh, sd_ref[...].astype(BF16))

    def expert_group(step, slot):
        xb = xb_ref[...]
        g_rot = pltpu.roll(gate_ref[...], jnp.bitwise_and(LANES - step * MOE_EPS, LANES - 1), 1)
        acc = acc_ref[...]
        for p in range(MOE_EPS // 2):
            wg2 = jnp.concatenate([wg_buf[slot, 2 * p].astype(BF16), wg_buf[slot, 2 * p + 1].astype(BF16)], axis=1)
            wu2 = jnp.concatenate([wu_buf[slot, 2 * p].astype(BF16), wu_buf[slot, 2 * p + 1].astype(BF16)], axis=1)
            hg = _dot(xb, wg2)
            hu = _dot(xb, wu2)
            g2 = jnp.concatenate(
                [jnp.broadcast_to(g_rot[:, 2 * p:2 * p + 1], (n_tok, EXPERT_HIDDEN)),
                 jnp.broadcast_to(g_rot[:, 2 * p + 1:2 * p + 2], (n_tok, EXPERT_HIDDEN))], axis=1)
            h = (_silu(hg) * hu * g2).astype(BF16)
            wd2 = wd_buf[slot, 2 * p:2 * p + 2].reshape(2 * EXPERT_HIDDEN, D_MODEL).astype(BF16)
            acc = acc + _dot(h, wd2)
        acc_ref[...] = acc

    def two_groups(k, carry):
        step = 2 * k
        for cp in weight_copies(step, 0):
            cp.wait()
        for cp in weight_copies(step + 1, 1):
            cp.start()
        expert_group(step, 0)

        for cp in weight_copies(step + 1, 1):
            cp.wait()

        @pl.when(step + 2 < n_steps)
        def _():
            for cp in weight_copies(step + 2, 0):
                cp.start()

        expert_group(step + 1, 1)
        return carry

    lax.fori_loop(0, n_steps // 2, two_groups, 0)
    o_ref[...] = _post_norm(y_ref[...], acc_ref[...], mod_ref[5:6, :], lng_ref[...], lnb_ref[...])


def _moe(y, mods, layer, rwt, rb, exp_w_gate, exp_w_up, exp_w_down, sg, su, sd, ln_g, ln_b, seq_len):
    t = y.shape[0]
    tile = MOE_TILE
    n_steps = N_EXPERTS // MOE_EPS
    assert n_steps % 2 == 0
    mod_idx = (lambda i: (i * tile // seq_len, 0, 0)) if mods.shape[0] > 1 else (lambda i: (0, 0, 0))

    def full(shape):
        return pl.BlockSpec(shape, lambda i: (0,) * len(shape))
    hbm = pl.BlockSpec(memory_space=pl.ANY)
    return pl.pallas_call(
        functools.partial(_moe_kernel, layer=layer, n_steps=n_steps),
        grid=(t // tile,),
        in_specs=[pl.BlockSpec((tile, D_MODEL), lambda i: (i, 0)),
                  pl.BlockSpec((None, N_MOD, D_MODEL), mod_idx),
                  full((N_EXPERTS, D_MODEL)),
                  full((N_EXPERTS, 1)),
                  hbm, hbm, hbm,
                  full((D_MODEL, EXPERT_HIDDEN)),
                  full((D_MODEL, EXPERT_HIDDEN)),
                  full((EXPERT_HIDDEN, D_MODEL)),
                  full((1, D_MODEL)),
                  full((1, D_MODEL))],
        out_specs=pl.BlockSpec((tile, D_MODEL), lambda i: (i, 0)),
        out_shape=jax.ShapeDtypeStruct((t, D_MODEL), F32),
        scratch_shapes=[pltpu.VMEM((tile, D_MODEL), BF16),
                        pltpu.VMEM((tile, LANES), F32),
                        pltpu.VMEM((tile, D_MODEL), F32),
                        pltpu.VMEM((2, MOE_EPS, D_MODEL, EXPERT_HIDDEN), F32),
                        pltpu.VMEM((2, MOE_EPS, D_MODEL, EXPERT_HIDDEN), F32),
                        pltpu.VMEM((2, MOE_EPS, EXPERT_HIDDEN, D_MODEL), F32),
                        pltpu.SemaphoreType.DMA((3, 2))],
        compiler_params=_params(1),
        name="moe",
    )(y, mods, rwt, rb, exp_w_gate, exp_w_up, exp_w_down, sg, su, sd,
      ln_g.reshape(1, D_MODEL), ln_b.reshape(1, D_MODEL))


def kernel(x_prompt, x_sample, c, state_ret, cache_na_k, cache_na_v, cache_att_k, cache_att_v, c_ctx,
           w_mod, b_mod, ln_g, ln_b, w_in_even, w_out_even, ret_decay_logit, na_rpb,
           w_in_odd, w_out_odd, conv_w, q_norm_g, k_norm_g, router_w, router_b,
           exp_w_gate, exp_w_up, exp_w_down, sh_w_gate, sh_w_up, sh_w_down):
    yp = x_prompt.reshape(BATCH * SEQ, D_MODEL)
    ys = x_sample.reshape(DEC_BATCH * DEC_SEQ, D_MODEL)

    cond = jnp.zeros((COND_ROWS, D_MODEL), F32).at[0].set(c_ctx).at[1:1 + DEC_BATCH].set(c)
    mods = _adaln(cond, w_mod, b_mod).reshape(DEPTH, COND_ROWS, N_MOD, D_MODEL)

    out = {}
    for l in range(DEPTH):
        i = l // 2
        mp = mods[l, 0:1]
        ms = mods[l, 1:1 + DEC_BATCH]
        if l % 2 == 0:
            decay_rows = jnp.broadcast_to(ret_decay_logit[i].reshape(2 * RET_HEADS, 1), (2 * RET_HEADS, LANES))
            dts = (BF16, BF16, BF16, F32, BF16, BF16, BF16)
            yp, st, na_k, na_v, w_in, w_out = _even_ctx(yp, mp, w_in_even, w_out_even, i, decay_rows,
                                                        ln_g[l, 0], ln_b[l, 0])
            sp = _inproj(ys, ms, w_in, EVEN_SIZES, dts, DEC_SEQ)
            mix_s = _even_lat(*sp, state_ret,
                              cache_na_k[:, i].reshape(DEC_BATCH, PAST_LEN, NA_HEADS * NA_DH),
                              cache_na_v[:, i].reshape(DEC_BATCH, PAST_LEN, NA_HEADS * NA_DH),
                              i, decay_rows, na_rpb[i].reshape(-1))
            ys = _outproj(mix_s, ys, ms, w_out, ln_g[l, 0], ln_b[l, 0], DEC_SEQ)
            out.setdefault("st", []).append(st.reshape(BATCH, 2, RET_HEADS, RET_DK, RET_DV))
            out.setdefault("na_k", []).append(na_k.reshape(BATCH, SEQ, NA_HEADS, NA_DH))
            out.setdefault("na_v", []).append(na_v.reshape(BATCH, SEQ, NA_HEADS, NA_DH))
        else:
            qg = q_norm_g[i].reshape(1, ATT_DH)
            kg = k_norm_g[i].reshape(1, ATT_DH)
            yp, k_new, att_v, w_in, w_out = _odd_ctx(yp, mp, w_in_odd, w_out_odd, i, conv_w[i], qg, kg,
                                                     ln_g[l, 0], ln_b[l, 0])
            cos, sin = _rope_tables()
            ys = _odd_lat(ys, ms, w_in, cache_att_k, cache_att_v, i,
                          conv_w[i], qg, kg, cos, sin, w_out, ln_g[l, 0], ln_b[l, 0])
            out.setdefault("att_k", []).append(k_new.reshape(BATCH, SEQ, ATT_KV_HEADS, ATT_DH))
            out.setdefault("att_v", []).append(att_v.reshape(BATCH, SEQ, ATT_KV_HEADS, ATT_DH))
        rwt = router_w[l].T
        rb = router_b[l].reshape(N_EXPERTS, 1)
        moe_w = (rwt, rb, exp_w_gate, exp_w_up, exp_w_down, sh_w_gate[l], sh_w_up[l], sh_w_down[l],
                 ln_g[l, 1], ln_b[l, 1])
        yp = _moe(yp, mp, l, *moe_w, SEQ)
        ys = _moe(ys, ms, l, *moe_w, DEC_SEQ)

    return (yp.reshape(BATCH, SEQ, D_MODEL),
            ys.reshape(DEC_BATCH, DEC_SEQ, D_MODEL),
            jnp.stack(out["st"], axis=1),
            jnp.stack(out["na_k"], axis=1),
            jnp.stack(out["na_v"], axis=1),
            jnp.stack(out["att_k"], axis=1),
            jnp.stack(out["att_v"], axis=1))
```

```python
import functools

import jax
import jax.numpy as jnp
from jax import lax
from jax.experimental import pallas as pl
from jax.experimental.pallas import tpu as pltpu

F32 = jnp.float32
BF16 = jnp.bfloat16

D_MODEL = 1024
BATCH = 32
SEQ = 256
DEPTH = 2
DEC_BATCH = 4
DEC_SEQ = 1024
PAST_LEN = 512
GRID_W = 64
GRID_H = DEC_SEQ // GRID_W
MIX_HALF = D_MODEL // 2
RET_HEADS = 4
RET_DV = MIX_HALF // RET_HEADS
RET_DK = RET_DV // 2
NA_HEADS = 4
NA_DH = MIX_HALF // NA_HEADS
NA_WIN_R = 8
NA_WIN_C = 16
CONV_CH = MIX_HALF
ATT_HEADS = 4
ATT_KV_HEADS = 2
ATT_DH = MIX_HALF // ATT_HEADS
ROPE_THETA = 10000.0
N_EXPERTS = 64
EXPERT_HIDDEN = D_MODEL // 8
TOP_K = 8
N_GROUPS = 8
GROUP_SIZE = N_EXPERTS // N_GROUPS
TOPK_GROUPS = 4
ROUTED_SCALE = 2.5
DEEPNORM_ALPHA = (2 * DEPTH) ** 0.25
LN_EPS = 1e-5
RMS_EPS = 1e-6
EVEN_SIZES = (RET_HEADS * RET_DK, RET_HEADS * RET_DK, RET_HEADS * RET_DV, RET_HEADS * RET_DV,
              NA_HEADS * NA_DH, NA_HEADS * NA_DH, NA_HEADS * NA_DH)
ODD_SIZES = (CONV_CH, CONV_CH, CONV_CH, ATT_HEADS * ATT_DH, ATT_KV_HEADS * ATT_DH, ATT_KV_HEADS * ATT_DH)
N_MOD = 6
RPB_ROWS = 2 * NA_WIN_R - 1
RPB_COLS = 2 * NA_WIN_C - 1

LANES = 128
COND_ROWS = 16
V7X_VMEM_LIMIT = 56 * 1024 * 1024

PROJ_TILE = 1024
LAT_QBLK = 256
MOE_TILE = 1024
MOE_EPS = 4
ADALN_TN = 1536


def _params(n_axes, vmem_limit=V7X_VMEM_LIMIT):
    return pltpu.CompilerParams(dimension_semantics=("arbitrary",) * n_axes,
                                vmem_limit_bytes=vmem_limit)


def _dot(a, b):
    return jnp.dot(a, b, preferred_element_type=F32)


def _dot_nt(a, b):
    return lax.dot_general(a, b, (((1,), (1,)), ((), ())), preferred_element_type=F32)


def _split(a):
    hi = a.astype(BF16)
    lo = (a - hi.astype(F32)).astype(BF16)
    return hi, lo


def _silu(x):
    return x / (1.0 + jnp.exp(-x))


def _log_sigmoid(x):
    return jnp.minimum(x, 0.0) - jnp.log(1.0 + jnp.exp(-jnp.abs(x)))


def _layer_norm(z, g, b):
    mu = jnp.mean(z, axis=-1, keepdims=True)
    zc = z - mu
    var = jnp.mean(zc * zc, axis=-1, keepdims=True)
    return zc * lax.rsqrt(var + LN_EPS) * g + b


def _post_norm(x, y, gate, g, b):
    return _layer_norm(DEEPNORM_ALPHA * x + (1.0 + gate) * y, g, b)


def _head_norm(x):
    mu = jnp.mean(x, axis=-1, keepdims=True)
    xc = x - mu
    var = jnp.mean(xc * xc, axis=-1, keepdims=True)
    return xc * lax.rsqrt(var + LN_EPS)


def _rms_norm(x, g):
    return x * lax.rsqrt(jnp.mean(x * x, axis=-1, keepdims=True) + RMS_EPS) * g


def _softmax_pv(parts):
    m = None
    for s, _ in parts:
        mi = jnp.max(s, axis=-1, keepdims=True)
        m = mi if m is None else jnp.maximum(m, mi)
    l = None
    o = None
    for s, v in parts:
        p = jnp.exp(s - m)
        li = jnp.sum(p, axis=-1, keepdims=True)
        oi = _dot(p.astype(BF16), v)
        l = li if l is None else l + li
        o = oi if o is None else o + oi
    return o / l


def _adaln_kernel(c_ref, w_ref, b_ref, o_ref):
    a_hi, a_lo = _split(_silu(c_ref[...]))
    w_hi, w_lo = _split(w_ref[...])
    o_ref[...] = _dot(a_hi, w_hi) + _dot(a_lo, w_hi) + _dot(a_hi, w_lo) + b_ref[...]


def _adaln(cond, w_mod, b_mod):
    n = w_mod.shape[-1]
    return pl.pallas_call(
        _adaln_kernel,
        grid=(DEPTH, n // ADALN_TN),
        in_specs=[pl.BlockSpec((COND_ROWS, D_MODEL), lambda l, j: (0, 0)),
                  pl.BlockSpec((None, D_MODEL, ADALN_TN), lambda l, j: (l, 0, j)),
                  pl.BlockSpec((None, 1, ADALN_TN), lambda l, j: (l, 0, j))],
        out_specs=pl.BlockSpec((None, COND_ROWS, ADALN_TN), lambda l, j: (l, 0, j)),
        out_shape=jax.ShapeDtypeStruct((DEPTH, COND_ROWS, n), F32),
        compiler_params=_params(2),
        name="adaln",
    )(cond, w_mod, b_mod.reshape(DEPTH, 1, n))


def _inproj_kernel(x_ref, mod_ref, w_ref, *out_refs, sizes):
    shift = mod_ref[0:1, :]
    scale = mod_ref[1:2, :]
    h = (x_ref[...] * (1.0 + scale) + shift).astype(BF16)
    y = _dot(h, w_ref[...])
    for o_ref, off, n in zip(out_refs, _offsets(sizes), sizes):
        o_ref[...] = y[:, off:off + n].astype(o_ref.dtype)


def _inproj(x, mods, w_bf16, sizes, dtypes, seq_len):
    t = x.shape[0]
    tile = PROJ_TILE
    mod_idx = (lambda i: (i * tile // seq_len, 0, 0)) if mods.shape[0] > 1 else (lambda i: (0, 0, 0))
    return pl.pallas_call(
        functools.partial(_inproj_kernel, sizes=sizes),
        grid=(t // tile,),
        in_specs=[pl.BlockSpec((tile, D_MODEL), lambda i: (i, 0)),
                  pl.BlockSpec((None, N_MOD, D_MODEL), mod_idx),
                  pl.BlockSpec(w_bf16.shape, lambda i: (0, 0))],
        out_specs=[pl.BlockSpec((tile, n), lambda i: (i, 0)) for n in sizes],
        out_shape=[jax.ShapeDtypeStruct((t, n), dt) for n, dt in zip(sizes, dtypes)],
        compiler_params=_params(1),
        name="inproj",
    )(x, mods, w_bf16)


def _outproj_kernel(mix_ref, x_ref, mod_ref, w_ref, g_ref, b_ref, o_ref):
    y = _dot(mix_ref[...], w_ref[...])
    o_ref[...] = _post_norm(x_ref[...], y, mod_ref[2:3, :], g_ref[...], b_ref[...])


def _outproj(mix, x, mods, w_bf16, ln_g, ln_b, seq_len):
    t = x.shape[0]
    tile = PROJ_TILE
    mod_idx = (lambda i: (i * tile // seq_len, 0, 0)) if mods.shape[0] > 1 else (lambda i: (0, 0, 0))
    return pl.pallas_call(
        _outproj_kernel,
        grid=(t // tile,),
        in_specs=[pl.BlockSpec((tile, D_MODEL), lambda i: (i, 0)),
                  pl.BlockSpec((tile, D_MODEL), lambda i: (i, 0)),
                  pl.BlockSpec((None, N_MOD, D_MODEL), mod_idx),
                  pl.BlockSpec((D_MODEL, D_MODEL), lambda i: (0, 0)),
                  pl.BlockSpec((1, D_MODEL), lambda i: (0, 0)),
                  pl.BlockSpec((1, D_MODEL), lambda i: (0, 0))],
        out_specs=pl.BlockSpec((tile, D_MODEL), lambda i: (i, 0)),
        out_shape=jax.ShapeDtypeStruct((t, D_MODEL), F32),
        compiler_params=_params(1),
        name="outproj",
    )(mix, x, mods, w_bf16, ln_g.reshape(1, D_MODEL), ln_b.reshape(1, D_MODEL))


def _decay_matrix(length, lgf, lgb, row0=0, rows=None):
    rows = length if rows is None else rows
    ii = lax.broadcasted_iota(jnp.int32, (rows, length), 0) + row0
    jj = lax.broadcasted_iota(jnp.int32, (rows, length), 1)
    rel = (ii - jj).astype(F32)
    fwd = jnp.where(rel >= 0.0, jnp.exp(lgf * jnp.maximum(rel, 0.0)), 0.0)
    bwd = jnp.where(rel <= 0.0, jnp.exp(lgb * jnp.maximum(-rel, 0.0)), 0.0)
    return fwd + bwd


def _retention_head(q, k, v, lgf, lgb, s0f, s0b, dmat=None):
    length = q.shape[0]
    s = _dot_nt(q.astype(BF16), k.astype(BF16))
    if dmat is None:
        dmat = _decay_matrix(length, lgf, lgb)
    if dmat.dtype == BF16:
        o = _dot(s.astype(BF16) * dmat, v)
    else:
        o = _dot((s * dmat).astype(BF16), v)
    if s0f is not None:
        pos = lax.broadcasted_iota(jnp.int32, (length, 1), 0).astype(F32)
        qf = q * jnp.exp(lgf * (pos + 1.0))
        qb = q * jnp.exp(lgb * (length - pos))
        o = o + _dot(qf.astype(BF16), s0f.astype(BF16)) + _dot(qb.astype(BF16), s0b.astype(BF16))
    return o


def _offsets(sizes):
    return [sum(sizes[:k]) for k in range(len(sizes))]


def _project(x_ref, mod_ref, wi_ref, dst_ref):
    h = (x_ref[...] * (1.0 + mod_ref[1:2, :]) + mod_ref[0:1, :]).astype(BF16)
    dst_ref[...] = _dot(h, wi_ref[...])


def _cast_rows(src_ref, dst_ref):
    def body(r, carry):
        rows = pl.ds(pl.multiple_of(r * LANES, LANES), LANES)
        dst_ref[rows, :] = src_ref[rows, :].astype(BF16)
        return carry
    lax.fori_loop(0, src_ref.shape[0] // LANES, body, 0)


def _project_next(xn_ref, xc_ref, mod_ref, wif_ref, wof_ref, wi_ref, wo_ref, nxt_ref, cur_ref):
    @pl.when(pl.program_id(0) == 0)
    def _():
        _cast_rows(wif_ref, wi_ref)
        _cast_rows(wof_ref, wo_ref)
        _project(xc_ref, mod_ref, wi_ref, cur_ref)

    _project(xn_ref, mod_ref, wi_ref, nxt_ref)
    return cur_ref


def _even_ctx_kernel(xn_ref, xc_ref, mod_ref, wif_ref, dl_ref, wof_ref, lng_ref, lnb_ref,
                     y_ref, st_ref, nak_ref, nav_ref, wi_ref, wo_ref,
                     nxt_ref, cur_ref, mix_ref, dmat_ref, kdec_ref):
    length = xc_ref.shape[0]

    @pl.when(pl.program_id(0) == 0)
    def _():
        lg = _log_sigmoid(dl_ref[...])
        pos = lax.broadcasted_iota(jnp.int32, (length, 1), 0).astype(F32)
        for h in range(RET_HEADS):
            lgf = lg[h:h + 1, 0:1]
            lgb = lg[RET_HEADS + h:RET_HEADS + h + 1, 0:1]
            dmat_ref[h] = _decay_matrix(length, lgf, lgb)
            kdec_ref[h] = jnp.broadcast_to(jnp.exp(lgf * (length - 1.0 - pos)), (length, RET_DK))
            kdec_ref[RET_HEADS + h] = jnp.broadcast_to(jnp.exp(lgb * pos), (length, RET_DK))

    cur = _project_next(xn_ref, xc_ref, mod_ref, wif_ref, wof_ref, wi_ref, wo_ref, nxt_ref, cur_ref)
    o_qr, o_kr, o_vr, o_gr, o_qn, o_kn, o_vn = _offsets(EVEN_SIZES)
    eye = jnp.where(lax.broadcasted_iota(jnp.int32, (2 * RET_DK, 2 * RET_DK), 0)
                    == lax.broadcasted_iota(jnp.int32, (2 * RET_DK, 2 * RET_DK), 1), 1.0, 0.0).astype(BF16)
    for h in range(RET_HEADS):
        q = cur[:, o_qr + h * RET_DK:o_qr + (h + 1) * RET_DK]
        k = cur[:, o_kr + h * RET_DK:o_kr + (h + 1) * RET_DK] * (RET_DK ** -0.5)
        v = cur[:, o_vr + h * RET_DV:o_vr + (h + 1) * RET_DV].astype(BF16)
        o = _retention_head(q, k, v, None, None, None, None, dmat=dmat_ref[h])
        k_dec = jnp.concatenate([k * kdec_ref[h], k * kdec_ref[RET_HEADS + h]], axis=1).astype(BF16)
        states = _dot(_dot_nt(eye, k_dec).astype(BF16), v)
        st_ref[h] = states[:RET_DK]
        st_ref[RET_HEADS + h] = states[RET_DK:]
        g = cur[:, o_gr + h * RET_DV:o_gr + (h + 1) * RET_DV]
        mix_ref[:, h * RET_DV:(h + 1) * RET_DV] = (_head_norm(o) * _silu(g)).astype(mix_ref.dtype)
    base = RET_HEADS * RET_DV
    for h in range(NA_HEADS):
        q = cur[:, o_qn + h * NA_DH:o_qn + (h + 1) * NA_DH].astype(BF16)
        k = cur[:, o_kn + h * NA_DH:o_kn + (h + 1) * NA_DH]
        v = cur[:, o_vn + h * NA_DH:o_vn + (h + 1) * NA_DH]
        nak_ref[:, h, :] = k
        nav_ref[:, h, :] = v
        s = _dot_nt(q, k.astype(BF16)) * (NA_DH ** -0.5)
        mix_ref[:, base + h * NA_DH:base + (h + 1) * NA_DH] = _softmax_pv([(s, v.astype(BF16))]).astype(mix_ref.dtype)
    y = _dot(mix_ref[...], wo_ref[...])
    y_ref[...] = _post_norm(xc_ref[...], y, mod_ref[2:3, :], lng_ref[...], lnb_ref[...])
    cur_ref[...] = nxt_ref[...]


def _ctx_layer_specs(n_proj, idx):
    def nxt(n):
        return pl.BlockSpec((SEQ, n), lambda i: (jnp.minimum(i + 1, BATCH - 1), 0))

    def cur(shape):
        return pl.BlockSpec(shape, lambda i: (i,) + (0,) * (len(shape) - 1))

    def full(shape):
        return pl.BlockSpec(shape, lambda i: (0,) * len(shape))

    def weight(n):
        return pl.BlockSpec((None, D_MODEL, n), lambda i: (idx, 0, 0), pipeline_mode=pl.Buffered(1))
    head = [nxt(D_MODEL), cur((SEQ, D_MODEL)), full((None, N_MOD, D_MODEL)), weight(n_proj)]
    tail = [weight(D_MODEL), full((1, D_MODEL)), full((1, D_MODEL))]
    w_specs = [full((D_MODEL, n_proj)), full((D_MODEL, D_MODEL))]
    w_shapes = [jax.ShapeDtypeStruct((D_MODEL, n_proj), BF16), jax.ShapeDtypeStruct((D_MODEL, D_MODEL), BF16)]
    return head, tail, cur, full, w_specs, w_shapes


def _even_ctx(x, mods, w_in, w_out, idx, decay_rows, ln_g, ln_b):
    head, tail, cur, full, w_specs, w_shapes = _ctx_layer_specs(sum(EVEN_SIZES), idx)
    return pl.pallas_call(
        _even_ctx_kernel,
        grid=(BATCH,),
        in_specs=head + [full((2 * RET_HEADS, LANES))] + tail,
        out_specs=[cur((SEQ, D_MODEL)),
                   cur((None, 2 * RET_HEADS, RET_DK, RET_DV)),
                   cur((SEQ, NA_HEADS, NA_DH)), cur((SEQ, NA_HEADS, NA_DH))] + w_specs,
        out_shape=[jax.ShapeDtypeStruct((BATCH * SEQ, D_MODEL), F32),
                   jax.ShapeDtypeStruct((BATCH, 2 * RET_HEADS, RET_DK, RET_DV), F32),
                   jax.ShapeDtypeStruct((BATCH * SEQ, NA_HEADS, NA_DH), F32),
                   jax.ShapeDtypeStruct((BATCH * SEQ, NA_HEADS, NA_DH), F32)] + w_shapes,
        scratch_shapes=[pltpu.VMEM((SEQ, sum(EVEN_SIZES)), F32),
                        pltpu.VMEM((SEQ, sum(EVEN_SIZES)), F32),
                        pltpu.VMEM((SEQ, D_MODEL), BF16),
                        pltpu.VMEM((RET_HEADS, SEQ, SEQ), F32),
                        pltpu.VMEM((2 * RET_HEADS, SEQ, RET_DK), F32)],
        compiler_params=_params(1),
        name="even_ctx",
    )(x, x, mods, w_in, decay_rows, w_out, ln_g.reshape(1, D_MODEL), ln_b.reshape(1, D_MODEL))


def _build_rpb_tiles(rpb_ref, tile_ref):
    qc = lax.broadcasted_iota(jnp.int32, (GRID_W, LANES), 0)
    kc = lax.broadcasted_iota(jnp.int32, (GRID_W, LANES), 1) % GRID_W
    diff = kc - qc + (NA_WIN_C - 1)
    start = jnp.clip(qc - NA_WIN_C // 2, 0, GRID_W - NA_WIN_C)
    win = (kc >= start) & (kc < start + NA_WIN_C)

    def body(idx, carry):
        t = jnp.zeros((GRID_W, LANES), F32)
        for d in range(RPB_COLS):
            t = jnp.where(diff == d, rpb_ref[idx * RPB_COLS + d], t)
        tile_ref[idx] = jnp.where(win, t, -jnp.inf)
        return carry

    lax.fori_loop(0, NA_HEADS * RPB_ROWS, body, 0)


NA_Q_ROWS = 8
NA_KEY_ROWS = 12


def _band_start(qr):
    rows_w = min(NA_WIN_R, GRID_H)
    return min(max(qr - rows_w // 2, 0), GRID_H - rows_w)


def _key_window(qrow0):
    w0 = min(_band_start(qrow0), GRID_H - NA_KEY_ROWS)
    assert _band_start(qrow0 + NA_Q_ROWS - 1) + min(NA_WIN_R, GRID_H) <= w0 + NA_KEY_ROWS
    return w0


def _build_bias(tile_ref, bias_ref, head, qrow0):
    left = lax.broadcasted_iota(jnp.int32, (GRID_W, LANES), 1) < GRID_W
    neg = jnp.full((GRID_W, LANES), -jnp.inf, F32)
    rows_w = min(NA_WIN_R, GRID_H)
    w0 = _key_window(qrow0)
    for dq in range(NA_Q_ROWS):
        qr = qrow0 + dq
        rs = _band_start(qr)

        def tile(kr):
            if rs <= kr < rs + rows_w:
                return tile_ref[head * RPB_ROWS + kr - qr + NA_WIN_R - 1]
            return neg

        pieces = [jnp.where(left, tile(w0 + 2 * a), tile(w0 + 2 * a + 1)) for a in range(NA_KEY_ROWS // 2)]
        bias_ref[dq * GRID_W:(dq + 1) * GRID_W, :] = jnp.concatenate(pieces, axis=1)


def _even_lat_kernel(qr_ref, kr_ref, vr_ref, gr_ref, qn_ref, kn_ref, vn_ref, s0_ref, ck_hbm, cv_hbm,
                     dl_ref, rpb_ref, mix_ref, tile_ref, bias_ref, dmat_ref, cache_buf, sem, *, idx):
    length = qr_ref.shape[0]
    lg = _log_sigmoid(dl_ref[...])

    def cache_copies():
        b = pl.program_id(0)
        return [pltpu.make_async_copy(src.at[b, idx, :, h, :], cache_buf.at[a, h], sem.at[a, h])
                for a, src in enumerate((ck_hbm, cv_hbm)) for h in range(NA_HEADS)]

    for cp in cache_copies():
        cp.start()

    @pl.when(pl.program_id(0) == 0)
    def _():
        _build_rpb_tiles(rpb_ref, tile_ref)
        for h in range(RET_HEADS):
            lgf = lg[h:h + 1, 0:1]
            lgb = lg[RET_HEADS + h:RET_HEADS + h + 1, 0:1]

            def rows_body(r, carry, h=h, lgf=lgf, lgb=lgb):
                r0 = pl.multiple_of(r * LANES, LANES)
                dmat_ref[h, pl.ds(r0, LANES), :] = _decay_matrix(length, lgf, lgb, r0, LANES).astype(BF16)
                return carry

            lax.fori_loop(0, length // LANES, rows_body, 0)

    for h in range(RET_HEADS):
        lgf = lg[h:h + 1, 0:1]
        lgb = lg[RET_HEADS + h:RET_HEADS + h + 1, 0:1]
        q = qr_ref[:, h * RET_DK:(h + 1) * RET_DK].astype(F32)
        k = kr_ref[:, h * RET_DK:(h + 1) * RET_DK].astype(F32) * (RET_DK ** -0.5)
        v = vr_ref[:, h * RET_DV:(h + 1) * RET_DV]
        o = _retention_head(q, k, v, lgf, lgb, s0_ref[0, h], s0_ref[1, h], dmat=dmat_ref[h])
        g = gr_ref[:, h * RET_DV:(h + 1) * RET_DV]
        mix_ref[:, h * RET_DV:(h + 1) * RET_DV] = (_head_norm(o) * _silu(g)).astype(mix_ref.dtype)
    base = RET_HEADS * RET_DV
    scale = NA_DH ** -0.5
    for cp in cache_copies():
        cp.wait()
    for h in range(NA_HEADS):
        sl = slice(h * NA_DH, (h + 1) * NA_DH)
        kc = cache_buf[0, h].astype(BF16)
        vc = cache_buf[1, h].astype(BF16)
        for qrow0 in range(0, GRID_H, NA_Q_ROWS):
            _build_bias(tile_ref, bias_ref, h, qrow0)
            rows = slice(qrow0 * GRID_W, (qrow0 + NA_Q_ROWS) * GRID_W)
            k0 = _key_window(qrow0) * GRID_W
            keys = slice(k0, k0 + NA_KEY_ROWS * GRID_W)
            q = qn_ref[rows, sl]
            s_band = _dot_nt(q, kn_ref[keys, sl]) * scale + bias_ref[...]
            s_ctx = _dot_nt(q, kc) * scale
            o = _softmax_pv([(s_band, vn_ref[keys, sl]), (s_ctx, vc)])
            mix_ref[rows, base + h * NA_DH:base + (h + 1) * NA_DH] = o.astype(mix_ref.dtype)


def _even_lat(qr, kr, vr, gr, qn, kn, vn, state, cache_k, cache_v, idx, decay_rows, rpb_flat):
    def seq_spec(n):
        return pl.BlockSpec((DEC_SEQ, n), lambda i: (i, 0))
    cache_spec = pl.BlockSpec(memory_space=pl.ANY)
    return pl.pallas_call(
        functools.partial(_even_lat_kernel, idx=idx),
        grid=(DEC_BATCH,),
        in_specs=[seq_spec(n) for n in EVEN_SIZES]
        + [pl.BlockSpec((None, None, 2, RET_HEADS, RET_DK, RET_DV), lambda i: (i, idx, 0, 0, 0, 0)),
           cache_spec, cache_spec,
           pl.BlockSpec((2 * RET_HEADS, LANES), lambda i: (0, 0)),
           pl.BlockSpec(memory_space=pltpu.SMEM)],
        out_specs=seq_spec(D_MODEL),
        out_shape=jax.ShapeDtypeStruct((DEC_BATCH * DEC_SEQ, D_MODEL), BF16),
        scratch_shapes=[pltpu.VMEM((NA_HEADS * RPB_ROWS, GRID_W, LANES), F32),
                        pltpu.VMEM((NA_Q_ROWS * GRID_W, NA_KEY_ROWS * GRID_W), F32),
                        pltpu.VMEM((RET_HEADS, DEC_SEQ, DEC_SEQ), BF16),
                        pltpu.VMEM((2, NA_HEADS, PAST_LEN, NA_DH), F32),
                        pltpu.SemaphoreType.DMA((2, NA_HEADS))],
        compiler_params=_params(1),
        name="even_lat",
    )(qr, kr, vr, gr, qn, kn, vn, state, cache_k, cache_v, decay_rows, rpb_flat)


def _gated_conv(bg, cg, u, w_ref):
    length, ch = u.shape
    z = cg * u
    row = lax.broadcasted_iota(jnp.int32, (length, ch), 0)
    z_prev = jnp.where(row == 0, 0.0, pltpu.roll(z, 1, 0))
    z_next = jnp.where(row == length - 1, 0.0, pltpu.roll(z, length - 1, 0))
    return bg * (z_prev * w_ref[0:1, :] + z * w_ref[1:2, :] + z_next * w_ref[2:3, :])


def _odd_ctx_kernel(xn_ref, xc_ref, mod_ref, wif_ref, w_ref, qg_ref, kg_ref, wof_ref, lng_ref, lnb_ref,
                    y_ref, ko_ref, vo_ref, wi_ref, wo_ref, nxt_ref, cur_ref, mix_ref):
    cur = _project_next(xn_ref, xc_ref, mod_ref, wif_ref, wof_ref, wi_ref, wo_ref, nxt_ref, cur_ref)
    o_bg, o_cg, o_u, o_q, o_k, o_v = _offsets(ODD_SIZES)
    conv = _gated_conv(cur[:, o_bg:o_bg + CONV_CH], cur[:, o_cg:o_cg + CONV_CH], cur[:, o_u:o_u + CONV_CH], w_ref)
    mix_ref[:, 0:CONV_CH] = conv.astype(mix_ref.dtype)
    group = ATT_HEADS // ATT_KV_HEADS
    scale = ATT_DH ** -0.5
    for kv in range(ATT_KV_HEADS):
        k = _rms_norm(cur[:, o_k + kv * ATT_DH:o_k + (kv + 1) * ATT_DH], kg_ref[...])
        v = cur[:, o_v + kv * ATT_DH:o_v + (kv + 1) * ATT_DH]
        ko_ref[:, kv, :] = k
        vo_ref[:, kv, :] = v
        k = k.astype(BF16)
        v = v.astype(BF16)
        for g in range(group):
            h = kv * group + g
            q = _rms_norm(cur[:, o_q + h * ATT_DH:o_q + (h + 1) * ATT_DH], qg_ref[...]).astype(BF16)
            o = _softmax_pv([(_dot_nt(q, k) * scale, v)])
            mix_ref[:, CONV_CH + h * ATT_DH:CONV_CH + (h + 1) * ATT_DH] = o.astype(mix_ref.dtype)
    y = _dot(mix_ref[...], wo_ref[...])
    y_ref[...] = _post_norm(xc_ref[...], y, mod_ref[2:3, :], lng_ref[...], lnb_ref[...])
    cur_ref[...] = nxt_ref[...]


def _odd_ctx(x, mods, w_in, w_out, idx, conv_w, qn_g, kn_g, ln_g, ln_b):
    head, tail, cur, full, w_specs, w_shapes = _ctx_layer_specs(sum(ODD_SIZES), idx)
    kv_shape = (SEQ, ATT_KV_HEADS, ATT_DH)
    return pl.pallas_call(
        _odd_ctx_kernel,
        grid=(BATCH,),
        in_specs=head + [full((3, CONV_CH)), full((1, ATT_DH)), full((1, ATT_DH))] + tail,
        out_specs=[cur((SEQ, D_MODEL)), cur(kv_shape), cur(kv_shape)] + w_specs,
        out_shape=[jax.ShapeDtypeStruct((BATCH * SEQ, D_MODEL), F32),
                   jax.ShapeDtypeStruct((BATCH * SEQ, ATT_KV_HEADS, ATT_DH), F32),
                   jax.ShapeDtypeStruct((BATCH * SEQ, ATT_KV_HEADS, ATT_DH), F32)] + w_shapes,
        scratch_shapes=[pltpu.VMEM((SEQ, sum(ODD_SIZES)), F32),
                        pltpu.VMEM((SEQ, sum(ODD_SIZES)), F32),
                        pltpu.VMEM((SEQ, D_MODEL), BF16)],
        compiler_params=_params(1),
        name="odd_ctx",
    )(x, x, mods, w_in, conv_w, qn_g, kn_g, w_out, ln_g.reshape(1, D_MODEL), ln_b.reshape(1, D_MODEL))


def _rope(x, cos, sin, even_lane):
    swapped = jnp.where(even_lane, pltpu.roll(x, LANES - 1, 1), pltpu.roll(x, 1, 1))
    return x * cos + swapped * sin


def _odd_lat_kernel(x_ref, mod_ref, wi_ref, ck_ref, cv_ref, w_ref, qg_ref, kg_ref, cos_ref, sin_ref,
                    wo_ref, lng_ref, lnb_ref, y_ref, proj_ref, qb_ref, kb_ref, vb_ref, mix_ref):
    length = x_ref.shape[0]
    h_in = (x_ref[...] * (1.0 + mod_ref[1:2, :]) + mod_ref[0:1, :]).astype(BF16)
    proj_ref[...] = _dot(h_in, wi_ref[...])
    o_bg, o_cg, o_u, o_q, o_k, o_v = _offsets(ODD_SIZES)
    conv = _gated_conv(proj_ref[:, o_bg:o_bg + CONV_CH], proj_ref[:, o_cg:o_cg + CONV_CH],
                       proj_ref[:, o_u:o_u + CONV_CH], w_ref)
    mix_ref[:, 0:CONV_CH] = conv.astype(mix_ref.dtype)
    group = ATT_HEADS // ATT_KV_HEADS
    scale = ATT_DH ** -0.5
    even_lane = lax.broadcasted_iota(jnp.int32, (length, ATT_DH), 1) % 2 == 0
    cos = cos_ref[...]
    sin = sin_ref[...]
    for kv in range(ATT_KV_HEADS):
        sl = slice(kv * ATT_DH, (kv + 1) * ATT_DH)
        k = _rms_norm(proj_ref[:, o_k + kv * ATT_DH:o_k + (kv + 1) * ATT_DH], kg_ref[...])
        kb_ref[0:length, sl] = _rope(k, cos, sin, even_lane).astype(BF16)
        vb_ref[0:length, sl] = proj_ref[:, o_v + kv * ATT_DH:o_v + (kv + 1) * ATT_DH].astype(BF16)
        kb_ref[length:, sl] = ck_ref[:, kv, :].astype(BF16)
        vb_ref[length:, sl] = cv_ref[:, kv, :].astype(BF16)
    for h in range(ATT_HEADS):
        q = _rms_norm(proj_ref[:, o_q + h * ATT_DH:o_q + (h + 1) * ATT_DH], qg_ref[...])
        qb_ref[:, h * ATT_DH:(h + 1) * ATT_DH] = _rope(q, cos, sin, even_lane).astype(BF16)

    def q_block(b, carry):
        rows = pl.ds(pl.multiple_of(b * LAT_QBLK, LAT_QBLK), LAT_QBLK)
        for h in range(ATT_HEADS):
            sl = slice((h // group) * ATT_DH, (h // group + 1) * ATT_DH)
            q = qb_ref[rows, h * ATT_DH:(h + 1) * ATT_DH]
            o = _softmax_pv([(_dot_nt(q, kb_ref[:, sl]) * scale, vb_ref[:, sl])])
            mix_ref[rows, CONV_CH + h * ATT_DH:CONV_CH + (h + 1) * ATT_DH] = o.astype(mix_ref.dtype)
        return carry

    lax.fori_loop(0, length // LAT_QBLK, q_block, 0)
    y = _dot(mix_ref[...], wo_ref[...])
    y_ref[...] = _post_norm(x_ref[...], y, mod_ref[2:3, :], lng_ref[...], lnb_ref[...])


def _odd_lat(x, mods, w_in, cache_k, cache_v, idx, conv_w, qn_g, kn_g, cos, sin, w_out, ln_g, ln_b):
    def full(shape):
        return pl.BlockSpec(shape, lambda i: (0,) * len(shape))

    def const(shape):
        return pl.BlockSpec(shape, lambda i: (0,) * len(shape), pipeline_mode=pl.Buffered(1))
    kv_w = ATT_KV_HEADS * ATT_DH
    cache_spec = pl.BlockSpec((None, None, PAST_LEN, ATT_KV_HEADS, ATT_DH), lambda i: (i, idx, 0, 0, 0))
    return pl.pallas_call(
        _odd_lat_kernel,
        grid=(DEC_BATCH,),
        in_specs=[pl.BlockSpec((DEC_SEQ, D_MODEL), lambda i: (i, 0)),
                  pl.BlockSpec((None, N_MOD, D_MODEL), lambda i: (i, 0, 0)),
                  const((D_MODEL, sum(ODD_SIZES))),
                  cache_spec, cache_spec, full((3, CONV_CH)), full((1, ATT_DH)), full((1, ATT_DH)),
                  const((DEC_SEQ, ATT_DH)), const((DEC_SEQ, ATT_DH)),
                  const((D_MODEL, D_MODEL)), full((1, D_MODEL)), full((1, D_MODEL))],
        out_specs=pl.BlockSpec((DEC_SEQ, D_MODEL), lambda i: (i, 0)),
        out_shape=jax.ShapeDtypeStruct((DEC_BATCH * DEC_SEQ, D_MODEL), F32),
        scratch_shapes=[pltpu.VMEM((DEC_SEQ, sum(ODD_SIZES)), F32),
                        pltpu.VMEM((DEC_SEQ, ATT_HEADS * ATT_DH), BF16),
                        pltpu.VMEM((DEC_SEQ + PAST_LEN, kv_w), BF16),
                        pltpu.VMEM((DEC_SEQ + PAST_LEN, kv_w), BF16),
                        pltpu.VMEM((DEC_SEQ, D_MODEL), BF16)],
        compiler_params=_params(1),
        name="odd_lat",
    )(x, mods, w_in, cache_k, cache_v, conv_w, qn_g, kn_g, cos, sin, w_out,
      ln_g.reshape(1, D_MODEL), ln_b.reshape(1, D_MODEL))


def _rope_tables():
    t = jnp.arange(DEC_SEQ)
    row = (t // GRID_W).astype(F32)
    col = (t % GRID_W).astype(F32)
    n_freq = ATT_DH // 4
    freqs = ROPE_THETA ** (-jnp.arange(n_freq, dtype=F32) / n_freq)
    ang = jnp.concatenate([row[:, None] * freqs, col[:, None] * freqs], axis=-1)
    cos = jnp.repeat(jnp.cos(ang), 2, axis=-1)
    sin = jnp.stack([-jnp.sin(ang), jnp.sin(ang)], axis=-1).reshape(DEC_SEQ, ATT_DH)
    return cos, sin


def _route(scores, sel):
    n_tok = sel.shape[1]
    neg = -jnp.inf
    sub = lax.broadcasted_iota(jnp.int32, (GROUP_SIZE, n_tok), 0).astype(F32)
    blocks = [sel[g * GROUP_SIZE:(g + 1) * GROUP_SIZE, :] for g in range(N_GROUPS)]
    grp = []
    for blk in blocks:
        m1 = jnp.max(blk, axis=0, keepdims=True)
        i1 = jnp.min(jnp.where(blk == m1, sub, float(GROUP_SIZE)), axis=0, keepdims=True)
        m2 = jnp.max(jnp.where(sub == i1, neg, blk), axis=0, keepdims=True)
        grp.append(m1 + m2)
    masked = []
    for g in range(N_GROUPS):
        ahead = jnp.zeros((1, n_tok), F32)
        for o in range(N_GROUPS):
            if o == g:
                continue
            wins = grp[o] >= grp[g] if o < g else grp[o] > grp[g]
            ahead = ahead + jnp.where(wins, 1.0, 0.0)
        masked.append(jnp.where(ahead < float(TOPK_GROUPS), blocks[g], neg))
    val = jnp.concatenate(masked, axis=0)
    row = lax.broadcasted_iota(jnp.int32, (N_EXPERTS, n_tok), 0).astype(F32)
    w = jnp.zeros((N_EXPERTS, n_tok), F32)
    for _ in range(TOP_K):
        m = jnp.max(val, axis=0, keepdims=True)
        idx = jnp.min(jnp.where(val == m, row, float(N_EXPERTS)), axis=0, keepdims=True)
        pick = row == idx
        w = jnp.where(pick, scores, w)
        val = jnp.where(pick, neg, val)
    return w / jnp.sum(w, axis=0, keepdims=True) * ROUTED_SCALE


def _moe_kernel(y_ref, mod_ref, rwt_ref, rb_ref, wg_hbm, wu_hbm, wd_hbm, sg_ref, su_ref, sd_ref,
                lng_ref, lnb_ref, o_ref, xb_ref, gate_ref, acc_ref, wg_buf, wu_buf, wd_buf, sem,
                *, layer, n_steps):
    n_tok = y_ref.shape[0]

    def weight_copies(step, slot):
        experts = pl.ds(step * MOE_EPS, MOE_EPS)
        return (pltpu.make_async_copy(wg_hbm.at[layer, experts], wg_buf.at[slot], sem.at[0, slot]),
                pltpu.make_async_copy(wu_hbm.at[layer, experts], wu_buf.at[slot], sem.at[1, slot]),
                pltpu.make_async_copy(wd_hbm.at[layer, experts], wd_buf.at[slot], sem.at[2, slot]))

    for cp in weight_copies(0, 0):
        cp.start()

    x = y_ref[...] * (1.0 + mod_ref[4:5, :]) + mod_ref[3:4, :]
    x_hi, x_lo = _split(x)
    xb_ref[...] = x_hi
    w_hi, w_lo = _split(rwt_ref[...])
    logits = _dot_nt(w_hi, x_hi) + _dot_nt(w_lo, x_hi) + _dot_nt(w_hi, x_lo)
    scores = 1.0 / (1.0 + jnp.exp(-logits))
    gate_t = _route(scores, scores + rb_ref[...])
    gate_ref[...] = jnp.concatenate([gate_t, jnp.zeros_like(gate_t)], axis=0).T
    sgu = jnp.concatenate([sg_ref[...].astype(BF16), su_ref[...].astype(BF16)], axis=1)
    hs = _dot(x_hi, sgu)
    h_sh = (_silu(hs[:, :EXPERT_HIDDEN]) * hs[:, EXPERT_HIDDEN:]).astype(BF16)
    acc_ref[...] = _dot(h_sh, sd_ref[...].astype(BF16))

    def expert_group(step, slot):
        xb = xb_ref[...]
        g_rot = pltpu.roll(gate_ref[...], jnp.bitwise_and(LANES - step * MOE_EPS, LANES - 1), 1)
        acc = acc_ref[...]
        for p in range(MOE_EPS // 2):
            wg2 = jnp.concatenate([wg_buf[slot, 2 * p].astype(BF16), wg_buf[slot, 2 * p + 1].astype(BF16)], axis=1)
            wu2 = jnp.concatenate([wu_buf[slot, 2 * p].astype(BF16), wu_buf[slot, 2 * p + 1].astype(BF16)], axis=1)
            hg = _dot(xb, wg2)
            hu = _dot(xb, wu2)
            g2 = jnp.concatenate(
                [jnp.broadcast_to(g_rot[:, 2 * p:2 * p + 1], (n_tok, EXPERT_HIDDEN)),
                 jnp.broadcast_to(g_rot[:, 2 * p + 1:2 * p + 2], (n_tok, EXPERT_HIDDEN))], axis=1)
            h = (_silu(hg) * hu * g2).astype(BF16)
            wd2 = wd_buf[slot, 2 * p:2 * p + 2].reshape(2 * EXPERT_HIDDEN, D_MODEL).astype(BF16)
            acc = acc + _dot(h, wd2)
        acc_ref[...] = acc

    def two_groups(k, carry):
        step = 2 * k
        for cp in weight_copies(step, 0):
            cp.wait()
        for cp in weight_copies(step + 1, 1):
            cp.start()
        expert_group(step, 0)

        for cp in weight_copies(step + 1, 1):
            cp.wait()

        @pl.when(step + 2 < n_steps)
        def _():
            for cp in weight_copies(step + 2, 0):
                cp.start()

        expert_group(step + 1, 1)
        return carry

    lax.fori_loop(0, n_steps // 2, two_groups, 0)
    o_ref[...] = _post_norm(y_ref[...], acc_ref[...], mod_ref[5:6, :], lng_ref[...], lnb_ref[...])


def _moe(y, mods, layer, rwt, rb, exp_w_gate, exp_w_up, exp_w_down, sg, su, sd, ln_g, ln_b, seq_len):
    t = y.shape[0]
    tile = MOE_TILE
    n_steps = N_EXPERTS // MOE_EPS
    assert n_steps % 2 == 0
    mod_idx = (lambda i: (i * tile // seq_len, 0, 0)) if mods.shape[0] > 1 else (lambda i: (0, 0, 0))

    def full(shape):
        return pl.BlockSpec(shape, lambda i: (0,) * len(shape))
    hbm = pl.BlockSpec(memory_space=pl.ANY)
    return pl.pallas_call(
        functools.partial(_moe_kernel, layer=layer, n_steps=n_steps),
        grid=(t // tile,),
        in_specs=[pl.BlockSpec((tile, D_MODEL), lambda i: (i, 0)),
                  pl.BlockSpec((None, N_MOD, D_MODEL), mod_idx),
                  full((N_EXPERTS, D_MODEL)),
                  full((N_EXPERTS, 1)),
                  hbm, hbm, hbm,
                  full((D_MODEL, EXPERT_HIDDEN)),
                  full((D_MODEL, EXPERT_HIDDEN)),
                  full((EXPERT_HIDDEN, D_MODEL)),
                  full((1, D_MODEL)),
                  full((1, D_MODEL))],
        out_specs=pl.BlockSpec((tile, D_MODEL), lambda i: (i, 0)),
        out_shape=jax.ShapeDtypeStruct((t, D_MODEL), F32),
        scratch_shapes=[pltpu.VMEM((tile, D_MODEL), BF16),
                        pltpu.VMEM((tile, LANES), F32),
                        pltpu.VMEM((tile, D_MODEL), F32),
                        pltpu.VMEM((2, MOE_EPS, D_MODEL, EXPERT_HIDDEN), F32),
                        pltpu.VMEM((2, MOE_EPS, D_MODEL, EXPERT_HIDDEN), F32),
                        pltpu.VMEM((2, MOE_EPS, EXPERT_HIDDEN, D_MODEL), F32),
                        pltpu.SemaphoreType.DMA((3, 2))],
        compiler_params=_params(1),
        name="moe",
    )(y, mods, rwt, rb, exp_w_gate, exp_w_up, exp_w_down, sg, su, sd,
      ln_g.reshape(1, D_MODEL), ln_b.reshape(1, D_MODEL))


def kernel(x_prompt, x_sample, c, state_ret, cache_na_k, cache_na_v, cache_att_k, cache_att_v, c_ctx,
           w_mod, b_mod, ln_g, ln_b, w_in_even, w_out_even, ret_decay_logit, na_rpb,
           w_in_odd, w_out_odd, conv_w, q_norm_g, k_norm_g, router_w, router_b,
           exp_w_gate, exp_w_up, exp_w_down, sh_w_gate, sh_w_up, sh_w_down):
    yp = x_prompt.reshape(BATCH * SEQ, D_MODEL)
    ys = x_sample.reshape(DEC_BATCH * DEC_SEQ, D_MODEL)

    cond = jnp.zeros((COND_ROWS, D_MODEL), F32).at[0].set(c_ctx).at[1:1 + DEC_BATCH].set(c)
    mods = _adaln(cond, w_mod, b_mod).reshape(DEPTH, COND_ROWS, N_MOD, D_MODEL)

    out = {}
    for l in range(DEPTH):
        i = l // 2
        mp = mods[l, 0:1]
        ms = mods[l, 1:1 + DEC_BATCH]
        if l % 2 == 0:
            decay_rows = jnp.broadcast_to(ret_decay_logit[i].reshape(2 * RET_HEADS, 1), (2 * RET_HEADS, LANES))
            dts = (BF16, BF16, BF16, F32, BF16, BF16, BF16)
            yp, st, na_k, na_v, w_in, w_out = _even_ctx(yp, mp, w_in_even, w_out_even, i, decay_rows,
                                                        ln_g[l, 0], ln_b[l, 0])
            sp = _inproj(ys, ms, w_in, EVEN_SIZES, dts, DEC_SEQ)
            mix_s = _even_lat(*sp, state_ret, cache_na_k, cache_na_v, i, decay_rows, na_rpb[i].reshape(-1))
            ys = _outproj(mix_s, ys, ms, w_out, ln_g[l, 0], ln_b[l, 0], DEC_SEQ)
            out.setdefault("st", []).append(st.reshape(BATCH, 2, RET_HEADS, RET_DK, RET_DV))
            out.setdefault("na_k", []).append(na_k.reshape(BATCH, SEQ, NA_HEADS, NA_DH))
            out.setdefault("na_v", []).append(na_v.reshape(BATCH, SEQ, NA_HEADS, NA_DH))
        else:
            qg = q_norm_g[i].reshape(1, ATT_DH)
            kg = k_norm_g[i].reshape(1, ATT_DH)
            yp, k_new, att_v, w_in, w_out = _odd_ctx(yp, mp, w_in_odd, w_out_odd, i, conv_w[i], qg, kg,
                                                     ln_g[l, 0], ln_b[l, 0])
            cos, sin = _rope_tables()
            ys = _odd_lat(ys, ms, w_in, cache_att_k, cache_att_v, i,
                          conv_w[i], qg, kg, cos, sin, w_out, ln_g[l, 0], ln_b[l, 0])
            out.setdefault("att_k", []).append(k_new.reshape(BATCH, SEQ, ATT_KV_HEADS, ATT_DH))
            out.setdefault("att_v", []).append(att_v.reshape(BATCH, SEQ, ATT_KV_HEADS, ATT_DH))
        rwt = router_w[l].T
        rb = router_b[l].reshape(N_EXPERTS, 1)
        moe_w = (rwt, rb, exp_w_gate, exp_w_up, exp_w_down, sh_w_gate[l], sh_w_up[l], sh_w_down[l],
                 ln_g[l, 1], ln_b[l, 1])
        yp = _moe(yp, mp, l, *moe_w, SEQ)
        ys = _moe(ys, ms, l, *moe_w, DEC_SEQ)

    return (yp.reshape(BATCH, SEQ, D_MODEL),
            ys.reshape(DEC_BATCH, DEC_SEQ, D_MODEL),
            jnp.stack(out["st"], axis=1),
            jnp.stack(out["na_k"], axis=1),
            jnp.stack(out["na_v"], axis=1),
            jnp.stack(out["att_k"], axis=1),
            jnp.stack(out["att_v"], axis=1))
```

```python
import functools

import jax
import jax.numpy as jnp
from jax import lax
from jax.experimental import pallas as pl
from jax.experimental.pallas import tpu as pltpu

F32 = jnp.float32
BF16 = jnp.bfloat16

D_MODEL = 1024
BATCH = 32
SEQ = 256
DEPTH = 2
DEC_BATCH = 4
DEC_SEQ = 1024
PAST_LEN = 512
GRID_W = 64
GRID_H = DEC_SEQ // GRID_W
MIX_HALF = D_MODEL // 2
RET_HEADS = 4
RET_DV = MIX_HALF // RET_HEADS
RET_DK = RET_DV // 2
NA_HEADS = 4
NA_DH = MIX_HALF // NA_HEADS
NA_WIN_R = 8
NA_WIN_C = 16
CONV_CH = MIX_HALF
ATT_HEADS = 4
ATT_KV_HEADS = 2
ATT_DH = MIX_HALF // ATT_HEADS
ROPE_THETA = 10000.0
N_EXPERTS = 64
EXPERT_HIDDEN = D_MODEL // 8
TOP_K = 8
N_GROUPS = 8
GROUP_SIZE = N_EXPERTS // N_GROUPS
TOPK_GROUPS = 4
ROUTED_SCALE = 2.5
DEEPNORM_ALPHA = (2 * DEPTH) ** 0.25
LN_EPS = 1e-5
RMS_EPS = 1e-6
EVEN_SIZES = (RET_HEADS * RET_DK, RET_HEADS * RET_DK, RET_HEADS * RET_DV, RET_HEADS * RET_DV,
              NA_HEADS * NA_DH, NA_HEADS * NA_DH, NA_HEADS * NA_DH)
ODD_SIZES = (CONV_CH, CONV_CH, CONV_CH, ATT_HEADS * ATT_DH, ATT_KV_HEADS * ATT_DH, ATT_KV_HEADS * ATT_DH)
N_MOD = 6
RPB_ROWS = 2 * NA_WIN_R - 1
RPB_COLS = 2 * NA_WIN_C - 1

LANES = 128
COND_ROWS = 16
V7X_VMEM_LIMIT = 56 * 1024 * 1024

PROJ_TILE = 1024
LAT_QBLK = 256
MOE_TILE = 1024
MOE_EPS = 4
ADALN_TN = 2048


def _params(n_axes, vmem_limit=V7X_VMEM_LIMIT):
    return pltpu.CompilerParams(dimension_semantics=("arbitrary",) * n_axes,
                                vmem_limit_bytes=vmem_limit)


def _dot(a, b):
    return jnp.dot(a, b, preferred_element_type=F32)


def _dot_nt(a, b):
    return lax.dot_general(a, b, (((1,), (1,)), ((), ())), preferred_element_type=F32)


def _split(a):
    hi = a.astype(BF16)
    lo = (a - hi.astype(F32)).astype(BF16)
    return hi, lo


def _silu(x):
    return x / (1.0 + jnp.exp(-x))


def _log_sigmoid(x):
    return jnp.minimum(x, 0.0) - jnp.log(1.0 + jnp.exp(-jnp.abs(x)))


def _layer_norm(z, g, b):
    mu = jnp.mean(z, axis=-1, keepdims=True)
    zc = z - mu
    var = jnp.mean(zc * zc, axis=-1, keepdims=True)
    return zc * lax.rsqrt(var + LN_EPS) * g + b


def _post_norm(x, y, gate, g, b):
    return _layer_norm(DEEPNORM_ALPHA * x + (1.0 + gate) * y, g, b)


def _head_norm(x):
    mu = jnp.mean(x, axis=-1, keepdims=True)
    xc = x - mu
    var = jnp.mean(xc * xc, axis=-1, keepdims=True)
    return xc * lax.rsqrt(var + LN_EPS)


def _rms_norm(x, g):
    return x * lax.rsqrt(jnp.mean(x * x, axis=-1, keepdims=True) + RMS_EPS) * g


def _softmax_pv(parts):
    m = None
    for s, _ in parts:
        mi = jnp.max(s, axis=-1, keepdims=True)
        m = mi if m is None else jnp.maximum(m, mi)
    l = None
    o = None
    for s, v in parts:
        p = jnp.exp(s - m)
        li = jnp.sum(p, axis=-1, keepdims=True)
        oi = _dot(p.astype(BF16), v)
        l = li if l is None else l + li
        o = oi if o is None else o + oi
    return o / l


def _adaln_kernel(c_ref, w_ref, b_ref, o_ref):
    a_hi, a_lo = _split(_silu(c_ref[...]))
    w_hi, w_lo = _split(w_ref[...])
    o_ref[...] = _dot(a_hi, w_hi) + _dot(a_lo, w_hi) + _dot(a_hi, w_lo) + b_ref[...]


def _adaln(cond, w_mod, b_mod):
    n = w_mod.shape[-1]
    return pl.pallas_call(
        _adaln_kernel,
        grid=(DEPTH, n // ADALN_TN),
        in_specs=[pl.BlockSpec((COND_ROWS, D_MODEL), lambda l, j: (0, 0)),
                  pl.BlockSpec((None, D_MODEL, ADALN_TN), lambda l, j: (l, 0, j)),
                  pl.BlockSpec((None, 1, ADALN_TN), lambda l, j: (l, 0, j))],
        out_specs=pl.BlockSpec((None, COND_ROWS, ADALN_TN), lambda l, j: (l, 0, j)),
        out_shape=jax.ShapeDtypeStruct((DEPTH, COND_ROWS, n), F32),
        compiler_params=_params(2),
        name="adaln",
    )(cond, w_mod, b_mod.reshape(DEPTH, 1, n))


def _inproj_kernel(x_ref, mod_ref, w_ref, *out_refs, sizes):
    shift = mod_ref[0:1, :]
    scale = mod_ref[1:2, :]
    h = (x_ref[...] * (1.0 + scale) + shift).astype(BF16)
    y = _dot(h, w_ref[...])
    for o_ref, off, n in zip(out_refs, _offsets(sizes), sizes):
        o_ref[...] = y[:, off:off + n].astype(o_ref.dtype)


def _inproj(x, mods, w_bf16, sizes, dtypes, seq_len):
    t = x.shape[0]
    tile = PROJ_TILE
    mod_idx = (lambda i: (i * tile // seq_len, 0, 0)) if mods.shape[0] > 1 else (lambda i: (0, 0, 0))
    return pl.pallas_call(
        functools.partial(_inproj_kernel, sizes=sizes),
        grid=(t // tile,),
        in_specs=[pl.BlockSpec((tile, D_MODEL), lambda i: (i, 0)),
                  pl.BlockSpec((None, N_MOD, D_MODEL), mod_idx),
                  pl.BlockSpec(w_bf16.shape, lambda i: (0, 0))],
        out_specs=[pl.BlockSpec((tile, n), lambda i: (i, 0)) for n in sizes],
        out_shape=[jax.ShapeDtypeStruct((t, n), dt) for n, dt in zip(sizes, dtypes)],
        compiler_params=_params(1),
        name="inproj",
    )(x, mods, w_bf16)


def _outproj_kernel(mix_ref, x_ref, mod_ref, w_ref, g_ref, b_ref, o_ref):
    y = _dot(mix_ref[...], w_ref[...])
    o_ref[...] = _post_norm(x_ref[...], y, mod_ref[2:3, :], g_ref[...], b_ref[...])


def _outproj(mix, x, mods, w_bf16, ln_g, ln_b, seq_len):
    t = x.shape[0]
    tile = PROJ_TILE
    mod_idx = (lambda i: (i * tile // seq_len, 0, 0)) if mods.shape[0] > 1 else (lambda i: (0, 0, 0))
    return pl.pallas_call(
        _outproj_kernel,
        grid=(t // tile,),
        in_specs=[pl.BlockSpec((tile, D_MODEL), lambda i: (i, 0)),
                  pl.BlockSpec((tile, D_MODEL), lambda i: (i, 0)),
                  pl.BlockSpec((None, N_MOD, D_MODEL), mod_idx),
                  pl.BlockSpec((D_MODEL, D_MODEL), lambda i: (0, 0)),
                  pl.BlockSpec((1, D_MODEL), lambda i: (0, 0)),
                  pl.BlockSpec((1, D_MODEL), lambda i: (0, 0))],
        out_specs=pl.BlockSpec((tile, D_MODEL), lambda i: (i, 0)),
        out_shape=jax.ShapeDtypeStruct((t, D_MODEL), F32),
        compiler_params=_params(1),
        name="outproj",
    )(mix, x, mods, w_bf16, ln_g.reshape(1, D_MODEL), ln_b.reshape(1, D_MODEL))


def _decay_matrix(length, lgf, lgb, row0=0, rows=None):
    rows = length if rows is None else rows
    ii = lax.broadcasted_iota(jnp.int32, (rows, length), 0) + row0
    jj = lax.broadcasted_iota(jnp.int32, (rows, length), 1)
    rel = (ii - jj).astype(F32)
    fwd = jnp.where(rel >= 0.0, jnp.exp(lgf * jnp.maximum(rel, 0.0)), 0.0)
    bwd = jnp.where(rel <= 0.0, jnp.exp(lgb * jnp.maximum(-rel, 0.0)), 0.0)
    return fwd + bwd


def _retention_head(q, k, v, lgf, lgb, s0f, s0b, dmat=None):
    length = q.shape[0]
    s = _dot_nt(q.astype(BF16), k.astype(BF16))
    if dmat is None:
        dmat = _decay_matrix(length, lgf, lgb)
    if dmat.dtype == BF16:
        o = _dot(s.astype(BF16) * dmat, v)
    else:
        o = _dot((s * dmat).astype(BF16), v)
    if s0f is not None:
        pos = lax.broadcasted_iota(jnp.int32, (length, 1), 0).astype(F32)
        qf = q * jnp.exp(lgf * (pos + 1.0))
        qb = q * jnp.exp(lgb * (length - pos))
        o = o + _dot(qf.astype(BF16), s0f.astype(BF16)) + _dot(qb.astype(BF16), s0b.astype(BF16))
    return o


def _offsets(sizes):
    return [sum(sizes[:k]) for k in range(len(sizes))]


def _project(x_ref, mod_ref, wi_ref, dst_ref):
    h = (x_ref[...] * (1.0 + mod_ref[1:2, :]) + mod_ref[0:1, :]).astype(BF16)
    dst_ref[...] = _dot(h, wi_ref[...])


def _cast_rows(src_ref, dst_ref):
    def body(r, carry):
        rows = pl.ds(pl.multiple_of(r * LANES, LANES), LANES)
        dst_ref[rows, :] = src_ref[rows, :].astype(BF16)
        return carry
    lax.fori_loop(0, src_ref.shape[0] // LANES, body, 0)


def _project_next(xn_ref, xc_ref, mod_ref, wif_ref, wof_ref, wi_ref, wo_ref, nxt_ref, cur_ref):
    @pl.when(pl.program_id(0) == 0)
    def _():
        _cast_rows(wif_ref, wi_ref)
        _cast_rows(wof_ref, wo_ref)
        _project(xc_ref, mod_ref, wi_ref, cur_ref)

    _project(xn_ref, mod_ref, wi_ref, nxt_ref)
    return cur_ref


def _even_ctx_kernel(xn_ref, xc_ref, mod_ref, wif_ref, dl_ref, wof_ref, lng_ref, lnb_ref,
                     y_ref, st_ref, nak_ref, nav_ref, wi_ref, wo_ref,
                     nxt_ref, cur_ref, mix_ref, dmat_ref, kdec_ref):
    length = xc_ref.shape[0]

    @pl.when(pl.program_id(0) == 0)
    def _():
        lg = _log_sigmoid(dl_ref[...])
        pos = lax.broadcasted_iota(jnp.int32, (length, 1), 0).astype(F32)
        for h in range(RET_HEADS):
            lgf = lg[h:h + 1, 0:1]
            lgb = lg[RET_HEADS + h:RET_HEADS + h + 1, 0:1]
            dmat_ref[h] = _decay_matrix(length, lgf, lgb)
            kdec_ref[h] = jnp.broadcast_to(jnp.exp(lgf * (length - 1.0 - pos)), (length, RET_DK))
            kdec_ref[RET_HEADS + h] = jnp.broadcast_to(jnp.exp(lgb * pos), (length, RET_DK))

    cur = _project_next(xn_ref, xc_ref, mod_ref, wif_ref, wof_ref, wi_ref, wo_ref, nxt_ref, cur_ref)
    o_qr, o_kr, o_vr, o_gr, o_qn, o_kn, o_vn = _offsets(EVEN_SIZES)
    eye = jnp.where(lax.broadcasted_iota(jnp.int32, (2 * RET_DK, 2 * RET_DK), 0)
                    == lax.broadcasted_iota(jnp.int32, (2 * RET_DK, 2 * RET_DK), 1), 1.0, 0.0).astype(BF16)
    for h in range(RET_HEADS):
        q = cur[:, o_qr + h * RET_DK:o_qr + (h + 1) * RET_DK]
        k = cur[:, o_kr + h * RET_DK:o_kr + (h + 1) * RET_DK] * (RET_DK ** -0.5)
        v = cur[:, o_vr + h * RET_DV:o_vr + (h + 1) * RET_DV].astype(BF16)
        o = _retention_head(q, k, v, None, None, None, None, dmat=dmat_ref[h])
        k_dec = jnp.concatenate([k * kdec_ref[h], k * kdec_ref[RET_HEADS + h]], axis=1).astype(BF16)
        states = _dot(_dot_nt(eye, k_dec).astype(BF16), v)
        st_ref[h] = states[:RET_DK]
        st_ref[RET_HEADS + h] = states[RET_DK:]
        g = cur[:, o_gr + h * RET_DV:o_gr + (h + 1) * RET_DV]
        mix_ref[:, h * RET_DV:(h + 1) * RET_DV] = (_head_norm(o) * _silu(g)).astype(mix_ref.dtype)
    base = RET_HEADS * RET_DV
    for h in range(NA_HEADS):
        q = cur[:, o_qn + h * NA_DH:o_qn + (h + 1) * NA_DH].astype(BF16)
        k = cur[:, o_kn + h * NA_DH:o_kn + (h + 1) * NA_DH]
        v = cur[:, o_vn + h * NA_DH:o_vn + (h + 1) * NA_DH]
        nak_ref[:, h, :] = k
        nav_ref[:, h, :] = v
        s = _dot_nt(q, k.astype(BF16)) * (NA_DH ** -0.5)
        mix_ref[:, base + h * NA_DH:base + (h + 1) * NA_DH] = _softmax_pv([(s, v.astype(BF16))]).astype(mix_ref.dtype)
    y = _dot(mix_ref[...], wo_ref[...])
    y_ref[...] = _post_norm(xc_ref[...], y, mod_ref[2:3, :], lng_ref[...], lnb_ref[...])
    cur_ref[...] = nxt_ref[...]


def _ctx_layer_specs(n_proj, idx):
    def nxt(n):
        return pl.BlockSpec((SEQ, n), lambda i: (jnp.minimum(i + 1, BATCH - 1), 0))

    def cur(shape):
        return pl.BlockSpec(shape, lambda i: (i,) + (0,) * (len(shape) - 1))

    def full(shape):
        return pl.BlockSpec(shape, lambda i: (0,) * len(shape))

    def weight(n):
        return pl.BlockSpec((None, D_MODEL, n), lambda i: (idx, 0, 0), pipeline_mode=pl.Buffered(1))
    head = [nxt(D_MODEL), cur((SEQ, D_MODEL)), full((None, N_MOD, D_MODEL)), weight(n_proj)]
    tail = [weight(D_MODEL), full((1, D_MODEL)), full((1, D_MODEL))]
    w_specs = [full((D_MODEL, n_proj)), full((D_MODEL, D_MODEL))]
    w_shapes = [jax.ShapeDtypeStruct((D_MODEL, n_proj), BF16), jax.ShapeDtypeStruct((D_MODEL, D_MODEL), BF16)]
    return head, tail, cur, full, w_specs, w_shapes


def _even_ctx(x, mods, w_in, w_out, idx, decay_rows, ln_g, ln_b):
    head, tail, cur, full, w_specs, w_shapes = _ctx_layer_specs(sum(EVEN_SIZES), idx)
    return pl.pallas_call(
        _even_ctx_kernel,
        grid=(BATCH,),
        in_specs=head + [full((2 * RET_HEADS, LANES))] + tail,
        out_specs=[cur((SEQ, D_MODEL)),
                   cur((None, 2 * RET_HEADS, RET_DK, RET_DV)),
                   cur((SEQ, NA_HEADS, NA_DH)), cur((SEQ, NA_HEADS, NA_DH))] + w_specs,
        out_shape=[jax.ShapeDtypeStruct((BATCH * SEQ, D_MODEL), F32),
                   jax.ShapeDtypeStruct((BATCH, 2 * RET_HEADS, RET_DK, RET_DV), F32),
                   jax.ShapeDtypeStruct((BATCH * SEQ, NA_HEADS, NA_DH), F32),
                   jax.ShapeDtypeStruct((BATCH * SEQ, NA_HEADS, NA_DH), F32)] + w_shapes,
        scratch_shapes=[pltpu.VMEM((SEQ, sum(EVEN_SIZES)), F32),
                        pltpu.VMEM((SEQ, sum(EVEN_SIZES)), F32),
                        pltpu.VMEM((SEQ, D_MODEL), BF16),
                        pltpu.VMEM((RET_HEADS, SEQ, SEQ), F32),
                        pltpu.VMEM((2 * RET_HEADS, SEQ, RET_DK), F32)],
        compiler_params=_params(1),
        name="even_ctx",
    )(x, x, mods, w_in, decay_rows, w_out, ln_g.reshape(1, D_MODEL), ln_b.reshape(1, D_MODEL))


def _build_rpb_tiles(rpb_ref, tile_ref):
    qc = lax.broadcasted_iota(jnp.int32, (GRID_W, LANES), 0)
    kc = lax.broadcasted_iota(jnp.int32, (GRID_W, LANES), 1) % GRID_W
    diff = kc - qc + (NA_WIN_C - 1)
    start = jnp.clip(qc - NA_WIN_C // 2, 0, GRID_W - NA_WIN_C)
    win = (kc >= start) & (kc < start + NA_WIN_C)

    def body(idx, carry):
        t = jnp.zeros((GRID_W, LANES), F32)
        for d in range(RPB_COLS):
            t = jnp.where(diff == d, rpb_ref[idx * RPB_COLS + d], t)
        tile_ref[idx] = jnp.where(win, t, -jnp.inf)
        return carry

    lax.fori_loop(0, NA_HEADS * RPB_ROWS, body, 0)


NA_Q_ROWS = 8
NA_KEY_ROWS = 12


def _band_start(qr):
    rows_w = min(NA_WIN_R, GRID_H)
    return min(max(qr - rows_w // 2, 0), GRID_H - rows_w)


def _key_window(qrow0):
    w0 = min(_band_start(qrow0), GRID_H - NA_KEY_ROWS)
    assert _band_start(qrow0 + NA_Q_ROWS - 1) + min(NA_WIN_R, GRID_H) <= w0 + NA_KEY_ROWS
    return w0


def _build_bias(tile_ref, bias_ref, head, qrow0):
    left = lax.broadcasted_iota(jnp.int32, (GRID_W, LANES), 1) < GRID_W
    neg = jnp.full((GRID_W, LANES), -jnp.inf, F32)
    rows_w = min(NA_WIN_R, GRID_H)
    w0 = _key_window(qrow0)
    for dq in range(NA_Q_ROWS):
        qr = qrow0 + dq
        rs = _band_start(qr)

        def tile(kr):
            if rs <= kr < rs + rows_w:
                return tile_ref[head * RPB_ROWS + kr - qr + NA_WIN_R - 1]
            return neg

        pieces = [jnp.where(left, tile(w0 + 2 * a), tile(w0 + 2 * a + 1)) for a in range(NA_KEY_ROWS // 2)]
        bias_ref[dq * GRID_W:(dq + 1) * GRID_W, :] = jnp.concatenate(pieces, axis=1)


def _even_lat_kernel(qr_ref, kr_ref, vr_ref, gr_ref, qn_ref, kn_ref, vn_ref, s0_ref, ck_hbm, cv_hbm,
                     dl_ref, rpb_ref, mix_ref, tile_ref, bias_ref, dmat_ref, cache_buf, sem, *, idx):
    length = qr_ref.shape[0]
    lg = _log_sigmoid(dl_ref[...])

    def cache_copies():
        b = pl.program_id(0)
        return [pltpu.make_async_copy(src.at[b, idx, :, h, :], cache_buf.at[a, h], sem.at[a, h])
                for a, src in enumerate((ck_hbm, cv_hbm)) for h in range(NA_HEADS)]

    for cp in cache_copies():
        cp.start()

    @pl.when(pl.program_id(0) == 0)
    def _():
        _build_rpb_tiles(rpb_ref, tile_ref)
        for h in range(RET_HEADS):
            lgf = lg[h:h + 1, 0:1]
            lgb = lg[RET_HEADS + h:RET_HEADS + h + 1, 0:1]

            def rows_body(r, carry, h=h, lgf=lgf, lgb=lgb):
                r0 = pl.multiple_of(r * LANES, LANES)
                dmat_ref[h, pl.ds(r0, LANES), :] = _decay_matrix(length, lgf, lgb, r0, LANES).astype(BF16)
                return carry

            lax.fori_loop(0, length // LANES, rows_body, 0)

    for h in range(RET_HEADS):
        lgf = lg[h:h + 1, 0:1]
        lgb = lg[RET_HEADS + h:RET_HEADS + h + 1, 0:1]
        q = qr_ref[:, h * RET_DK:(h + 1) * RET_DK].astype(F32)
        k = kr_ref[:, h * RET_DK:(h + 1) * RET_DK].astype(F32) * (RET_DK ** -0.5)
        v = vr_ref[:, h * RET_DV:(h + 1) * RET_DV]
        o = _retention_head(q, k, v, lgf, lgb, s0_ref[0, h], s0_ref[1, h], dmat=dmat_ref[h])
        g = gr_ref[:, h * RET_DV:(h + 1) * RET_DV]
        mix_ref[:, h * RET_DV:(h + 1) * RET_DV] = (_head_norm(o) * _silu(g)).astype(mix_ref.dtype)
    base = RET_HEADS * RET_DV
    scale = NA_DH ** -0.5
    for cp in cache_copies():
        cp.wait()
    for h in range(NA_HEADS):
        sl = slice(h * NA_DH, (h + 1) * NA_DH)
        kc = cache_buf[0, h].astype(BF16)
        vc = cache_buf[1, h].astype(BF16)
        for qrow0 in range(0, GRID_H, NA_Q_ROWS):
            _build_bias(tile_ref, bias_ref, h, qrow0)
            rows = slice(qrow0 * GRID_W, (qrow0 + NA_Q_ROWS) * GRID_W)
            k0 = _key_window(qrow0) * GRID_W
            keys = slice(k0, k0 + NA_KEY_ROWS * GRID_W)
            q = qn_ref[rows, sl]
            s_band = _dot_nt(q, kn_ref[keys, sl]) * scale + bias_ref[...]
            s_ctx = _dot_nt(q, kc) * scale
            o = _softmax_pv([(s_band, vn_ref[keys, sl]), (s_ctx, vc)])
            mix_ref[rows, base + h * NA_DH:base + (h + 1) * NA_DH] = o.astype(mix_ref.dtype)


def _even_lat(qr, kr, vr, gr, qn, kn, vn, state, cache_k, cache_v, idx, decay_rows, rpb_flat):
    def seq_spec(n):
        return pl.BlockSpec((DEC_SEQ, n), lambda i: (i, 0))
    cache_spec = pl.BlockSpec(memory_space=pl.ANY)
    return pl.pallas_call(
        functools.partial(_even_lat_kernel, idx=idx),
        grid=(DEC_BATCH,),
        in_specs=[seq_spec(n) for n in EVEN_SIZES]
        + [pl.BlockSpec((None, None, 2, RET_HEADS, RET_DK, RET_DV), lambda i: (i, idx, 0, 0, 0, 0)),
           cache_spec, cache_spec,
           pl.BlockSpec((2 * RET_HEADS, LANES), lambda i: (0, 0)),
           pl.BlockSpec(memory_space=pltpu.SMEM)],
        out_specs=seq_spec(D_MODEL),
        out_shape=jax.ShapeDtypeStruct((DEC_BATCH * DEC_SEQ, D_MODEL), BF16),
        scratch_shapes=[pltpu.VMEM((NA_HEADS * RPB_ROWS, GRID_W, LANES), F32),
                        pltpu.VMEM((NA_Q_ROWS * GRID_W, NA_KEY_ROWS * GRID_W), F32),
                        pltpu.VMEM((RET_HEADS, DEC_SEQ, DEC_SEQ), BF16),
                        pltpu.VMEM((2, NA_HEADS, PAST_LEN, NA_DH), F32),
                        pltpu.SemaphoreType.DMA((2, NA_HEADS))],
        compiler_params=_params(1),
        name="even_lat",
    )(qr, kr, vr, gr, qn, kn, vn, state, cache_k, cache_v, decay_rows, rpb_flat)


def _gated_conv(bg, cg, u, w_ref):
    length, ch = u.shape
    z = cg * u
    row = lax.broadcasted_iota(jnp.int32, (length, ch), 0)
    z_prev = jnp.where(row == 0, 0.0, pltpu.roll(z, 1, 0))
    z_next = jnp.where(row == length - 1, 0.0, pltpu.roll(z, length - 1, 0))
    return bg * (z_prev * w_ref[0:1, :] + z * w_ref[1:2, :] + z_next * w_ref[2:3, :])


def _odd_ctx_kernel(xn_ref, xc_ref, mod_ref, wif_ref, w_ref, qg_ref, kg_ref, wof_ref, lng_ref, lnb_ref,
                    y_ref, ko_ref, vo_ref, wi_ref, wo_ref, nxt_ref, cur_ref, mix_ref):
    cur = _project_next(xn_ref, xc_ref, mod_ref, wif_ref, wof_ref, wi_ref, wo_ref, nxt_ref, cur_ref)
    o_bg, o_cg, o_u, o_q, o_k, o_v = _offsets(ODD_SIZES)
    conv = _gated_conv(cur[:, o_bg:o_bg + CONV_CH], cur[:, o_cg:o_cg + CONV_CH], cur[:, o_u:o_u + CONV_CH], w_ref)
    mix_ref[:, 0:CONV_CH] = conv.astype(mix_ref.dtype)
    group = ATT_HEADS // ATT_KV_HEADS
    scale = ATT_DH ** -0.5
    for kv in range(ATT_KV_HEADS):
        k = _rms_norm(cur[:, o_k + kv * ATT_DH:o_k + (kv + 1) * ATT_DH], kg_ref[...])
        v = cur[:, o_v + kv * ATT_DH:o_v + (kv + 1) * ATT_DH]
        ko_ref[:, kv, :] = k
        vo_ref[:, kv, :] = v
        k = k.astype(BF16)
        v = v.astype(BF16)
        for g in range(group):
            h = kv * group + g
            q = _rms_norm(cur[:, o_q + h * ATT_DH:o_q + (h + 1) * ATT_DH], qg_ref[...]).astype(BF16)
            o = _softmax_pv([(_dot_nt(q, k) * scale, v)])
            mix_ref[:, CONV_CH + h * ATT_DH:CONV_CH + (h + 1) * ATT_DH] = o.astype(mix_ref.dtype)
    y = _dot(mix_ref[...], wo_ref[...])
    y_ref[...] = _post_norm(xc_ref[...], y, mod_ref[2:3, :], lng_ref[...], lnb_ref[...])
    cur_ref[...] = nxt_ref[...]


def _odd_ctx(x, mods, w_in, w_out, idx, conv_w, qn_g, kn_g, ln_g, ln_b):
    head, tail, cur, full, w_specs, w_shapes = _ctx_layer_specs(sum(ODD_SIZES), idx)
    kv_shape = (SEQ, ATT_KV_HEADS, ATT_DH)
    return pl.pallas_call(
        _odd_ctx_kernel,
        grid=(BATCH,),
        in_specs=head + [full((3, CONV_CH)), full((1, ATT_DH)), full((1, ATT_DH))] + tail,
        out_specs=[cur((SEQ, D_MODEL)), cur(kv_shape), cur(kv_shape)] + w_specs,
        out_shape=[jax.ShapeDtypeStruct((BATCH * SEQ, D_MODEL), F32),
                   jax.ShapeDtypeStruct((BATCH * SEQ, ATT_KV_HEADS, ATT_DH), F32),
                   jax.ShapeDtypeStruct((BATCH * SEQ, ATT_KV_HEADS, ATT_DH), F32)] + w_shapes,
        scratch_shapes=[pltpu.VMEM((SEQ, sum(ODD_SIZES)), F32),
                        pltpu.VMEM((SEQ, sum(ODD_SIZES)), F32),
                        pltpu.VMEM((SEQ, D_MODEL), BF16)],
        compiler_params=_params(1),
        name="odd_ctx",
    )(x, x, mods, w_in, conv_w, qn_g, kn_g, w_out, ln_g.reshape(1, D_MODEL), ln_b.reshape(1, D_MODEL))


def _rope(x, cos, sin, even_lane):
    swapped = jnp.where(even_lane, pltpu.roll(x, LANES - 1, 1), pltpu.roll(x, 1, 1))
    return x * cos + swapped * sin


def _odd_lat_kernel(x_ref, mod_ref, wi_ref, ck_ref, cv_ref, w_ref, qg_ref, kg_ref, cos_ref, sin_ref,
                    wo_ref, lng_ref, lnb_ref, y_ref, proj_ref, qb_ref, kb_ref, vb_ref, mix_ref):
    length = x_ref.shape[0]
    h_in = (x_ref[...] * (1.0 + mod_ref[1:2, :]) + mod_ref[0:1, :]).astype(BF16)
    proj_ref[...] = _dot(h_in, wi_ref[...])
    o_bg, o_cg, o_u, o_q, o_k, o_v = _offsets(ODD_SIZES)
    conv = _gated_conv(proj_ref[:, o_bg:o_bg + CONV_CH], proj_ref[:, o_cg:o_cg + CONV_CH],
                       proj_ref[:, o_u:o_u + CONV_CH], w_ref)
    mix_ref[:, 0:CONV_CH] = conv.astype(mix_ref.dtype)
    group = ATT_HEADS // ATT_KV_HEADS
    scale = ATT_DH ** -0.5
    even_lane = lax.broadcasted_iota(jnp.int32, (length, ATT_DH), 1) % 2 == 0
    cos = cos_ref[...]
    sin = sin_ref[...]
    for kv in range(ATT_KV_HEADS):
        sl = slice(kv * ATT_DH, (kv + 1) * ATT_DH)
        k = _rms_norm(proj_ref[:, o_k + kv * ATT_DH:o_k + (kv + 1) * ATT_DH], kg_ref[...])
        kb_ref[0:length, sl] = _rope(k, cos, sin, even_lane).astype(BF16)
        vb_ref[0:length, sl] = proj_ref[:, o_v + kv * ATT_DH:o_v + (kv + 1) * ATT_DH].astype(BF16)
        kb_ref[length:, sl] = ck_ref[:, kv, :].astype(BF16)
        vb_ref[length:, sl] = cv_ref[:, kv, :].astype(BF16)
    for h in range(ATT_HEADS):
        q = _rms_norm(proj_ref[:, o_q + h * ATT_DH:o_q + (h + 1) * ATT_DH], qg_ref[...])
        qb_ref[:, h * ATT_DH:(h + 1) * ATT_DH] = _rope(q, cos, sin, even_lane).astype(BF16)

    def q_block(b, carry):
        rows = pl.ds(pl.multiple_of(b * LAT_QBLK, LAT_QBLK), LAT_QBLK)
        for h in range(ATT_HEADS):
            sl = slice((h // group) * ATT_DH, (h // group + 1) * ATT_DH)
            q = qb_ref[rows, h * ATT_DH:(h + 1) * ATT_DH]
            o = _softmax_pv([(_dot_nt(q, kb_ref[:, sl]) * scale, vb_ref[:, sl])])
            mix_ref[rows, CONV_CH + h * ATT_DH:CONV_CH + (h + 1) * ATT_DH] = o.astype(mix_ref.dtype)
        return carry

    lax.fori_loop(0, length // LAT_QBLK, q_block, 0)
    y = _dot(mix_ref[...], wo_ref[...])
    y_ref[...] = _post_norm(x_ref[...], y, mod_ref[2:3, :], lng_ref[...], lnb_ref[...])


def _odd_lat(x, mods, w_in, cache_k, cache_v, idx, conv_w, qn_g, kn_g, cos, sin, w_out, ln_g, ln_b):
    def full(shape):
        return pl.BlockSpec(shape, lambda i: (0,) * len(shape))

    def const(shape):
        return pl.BlockSpec(shape, lambda i: (0,) * len(shape), pipeline_mode=pl.Buffered(1))
    kv_w = ATT_KV_HEADS * ATT_DH
    cache_spec = pl.BlockSpec((None, None, PAST_LEN, ATT_KV_HEADS, ATT_DH), lambda i: (i, idx, 0, 0, 0))
    return pl.pallas_call(
        _odd_lat_kernel,
        grid=(DEC_BATCH,),
        in_specs=[pl.BlockSpec((DEC_SEQ, D_MODEL), lambda i: (i, 0)),
                  pl.BlockSpec((None, N_MOD, D_MODEL), lambda i: (i, 0, 0)),
                  const((D_MODEL, sum(ODD_SIZES))),
                  cache_spec, cache_spec, full((3, CONV_CH)), full((1, ATT_DH)), full((1, ATT_DH)),
                  const((DEC_SEQ, ATT_DH)), const((DEC_SEQ, ATT_DH)),
                  const((D_MODEL, D_MODEL)), full((1, D_MODEL)), full((1, D_MODEL))],
        out_specs=pl.BlockSpec((DEC_SEQ, D_MODEL), lambda i: (i, 0)),
        out_shape=jax.ShapeDtypeStruct((DEC_BATCH * DEC_SEQ, D_MODEL), F32),
        scratch_shapes=[pltpu.VMEM((DEC_SEQ, sum(ODD_SIZES)), F32),
                        pltpu.VMEM((DEC_SEQ, ATT_HEADS * ATT_DH), BF16),
                        pltpu.VMEM((DEC_SEQ + PAST_LEN, kv_w), BF16),
                        pltpu.VMEM((DEC_SEQ + PAST_LEN, kv_w), BF16),
                        pltpu.VMEM((DEC_SEQ, D_MODEL), BF16)],
        compiler_params=_params(1),
        name="odd_lat",
    )(x, mods, w_in, cache_k, cache_v, conv_w, qn_g, kn_g, cos, sin, w_out,
      ln_g.reshape(1, D_MODEL), ln_b.reshape(1, D_MODEL))


def _rope_tables():
    t = jnp.arange(DEC_SEQ)
    row = (t // GRID_W).astype(F32)
    col = (t % GRID_W).astype(F32)
    n_freq = ATT_DH // 4
    freqs = ROPE_THETA ** (-jnp.arange(n_freq, dtype=F32) / n_freq)
    ang = jnp.concatenate([row[:, None] * freqs, col[:, None] * freqs], axis=-1)
    cos = jnp.repeat(jnp.cos(ang), 2, axis=-1)
    sin = jnp.stack([-jnp.sin(ang), jnp.sin(ang)], axis=-1).reshape(DEC_SEQ, ATT_DH)
    return cos, sin


def _route(scores, sel):
    n_tok = sel.shape[1]
    neg = -jnp.inf
    sub = lax.broadcasted_iota(jnp.int32, (GROUP_SIZE, n_tok), 0).astype(F32)
    blocks = [sel[g * GROUP_SIZE:(g + 1) * GROUP_SIZE, :] for g in range(N_GROUPS)]
    grp = []
    for blk in blocks:
        m1 = jnp.max(blk, axis=0, keepdims=True)
        i1 = jnp.min(jnp.where(blk == m1, sub, float(GROUP_SIZE)), axis=0, keepdims=True)
        m2 = jnp.max(jnp.where(sub == i1, neg, blk), axis=0, keepdims=True)
        grp.append(m1 + m2)
    masked = []
    for g in range(N_GROUPS):
        ahead = jnp.zeros((1, n_tok), F32)
        for o in range(N_GROUPS):
            if o == g:
                continue
            wins = grp[o] >= grp[g] if o < g else grp[o] > grp[g]
            ahead = ahead + jnp.where(wins, 1.0, 0.0)
        masked.append(jnp.where(ahead < float(TOPK_GROUPS), blocks[g], neg))
    val = jnp.concatenate(masked, axis=0)
    row = lax.broadcasted_iota(jnp.int32, (N_EXPERTS, n_tok), 0).astype(F32)
    w = jnp.zeros((N_EXPERTS, n_tok), F32)
    for _ in range(TOP_K):
        m = jnp.max(val, axis=0, keepdims=True)
        idx = jnp.min(jnp.where(val == m, row, float(N_EXPERTS)), axis=0, keepdims=True)
        pick = row == idx
        w = jnp.where(pick, scores, w)
        val = jnp.where(pick, neg, val)
    return w / jnp.sum(w, axis=0, keepdims=True) * ROUTED_SCALE


def _moe_kernel(y_ref, mod_ref, rwt_ref, rb_ref, wg_hbm, wu_hbm, wd_hbm, sg_ref, su_ref, sd_ref,
                lng_ref, lnb_ref, o_ref, xb_ref, gate_ref, acc_ref, wg_buf, wu_buf, wd_buf, sem,
                *, layer, n_steps):
    n_tok = y_ref.shape[0]

    def weight_copies(step, slot):
        experts = pl.ds(step * MOE_EPS, MOE_EPS)
        return (pltpu.make_async_copy(wg_hbm.at[layer, experts], wg_buf.at[slot], sem.at[0, slot]),
                pltpu.make_async_copy(wu_hbm.at[layer, experts], wu_buf.at[slot], sem.at[1, slot]),
                pltpu.make_async_copy(wd_hbm.at[layer, experts], wd_buf.at[slot], sem.at[2, slot]))

    for cp in weight_copies(0, 0):
        cp.start()

    x = y_ref[...] * (1.0 + mod_ref[4:5, :]) + mod_ref[3:4, :]
    x_hi, x_lo = _split(x)
    xb_ref[...] = x_hi
    w_hi, w_lo = _split(rwt_ref[...])
    logits = _dot_nt(w_hi, x_hi) + _dot_nt(w_lo, x_hi) + _dot_nt(w_hi, x_lo)
    scores = 1.0 / (1.0 + jnp.exp(-logits))
    gate_t = _route(scores, scores + rb_ref[...])
    gate_ref[...] = jnp.concatenate([gate_t, jnp.zeros_like(gate_t)], axis=0).T
    sgu = jnp.concatenate([sg_ref[...].astype(BF16), su_ref[...].astype(BF16)], axis=1)
    hs = _dot(x_hi, sgu)
    h_sh = (_silu(hs[:, :EXPERT_HIDDEN]) * hs[:, EXPERT_HIDDEN:]).astype(BF16)
    acc_ref[...] = _dot(h_sh, sd_ref[...].astype(BF16))

    def expert_group(step, slot):
        xb = xb_ref[...]
        g_rot = pltpu.roll(gate_ref[...], jnp.bitwise_and(LANES - step * MOE_EPS, LANES - 1), 1)
        acc = acc_ref[...]
        for p in range(MOE_EPS // 2):
            wg2 = jnp.concatenate([wg_buf[slot, 2 * p].astype(BF16), wg_buf[slot, 2 * p + 1].astype(BF16)], axis=1)
            wu2 = jnp.concatenate([wu_buf[slot, 2 * p].astype(BF16), wu_buf[slot, 2 * p + 1].astype(BF16)], axis=1)
            hg = _dot(xb, wg2)
            hu = _dot(xb, wu2)
            g2 = jnp.concatenate(
                [jnp.broadcast_to(g_rot[:, 2 * p:2 * p + 1], (n_tok, EXPERT_HIDDEN)),
                 jnp.broadcast_to(g_rot[:, 2 * p + 1:2 * p + 2], (n_tok, EXPERT_HIDDEN))], axis=1)
            h = (_silu(hg) * hu * g2).astype(BF16)
            wd2 = wd_buf[slot, 2 * p:2 * p + 2].reshape(2 * EXPERT_HIDDEN, D_MODEL).astype(BF16)
            acc = acc + _dot(h, wd2)
        acc_ref[...] = acc

    def two_groups(k, carry):
        step = 2 * k
        for cp in weight_copies(step, 0):
            cp.wait()
        for cp in weight_copies(step + 1, 1):
            cp.start()
        expert_group(step, 0)

        for cp in weight_copies(step + 1, 1):
            cp.wait()

        @pl.when(step + 2 < n_steps)
        def _():
            for cp in weight_copies(step + 2, 0):
                cp.start()

        expert_group(step + 1, 1)
        return carry

    lax.fori_loop(0, n_steps // 2, two_groups, 0)
    o_ref[...] = _post_norm(y_ref[...], acc_ref[...], mod_ref[5:6, :], lng_ref[...], lnb_ref[...])


def _moe(y, mods, layer, rwt, rb, exp_w_gate, exp_w_up, exp_w_down, sg, su, sd, ln_g, ln_b, seq_len):
    t = y.shape[0]
    tile = MOE_TILE
    n_steps = N_EXPERTS // MOE_EPS
    assert n_steps % 2 == 0
    mod_idx = (lambda i: (i * tile // seq_len, 0, 0)) if mods.shape[0] > 1 else (lambda i: (0, 0, 0))

    def full(shape):
        return pl.BlockSpec(shape, lambda i: (0,) * len(shape))
    hbm = pl.BlockSpec(memory_space=pl.ANY)
    return pl.pallas_call(
        functools.partial(_moe_kernel, layer=layer, n_steps=n_steps),
        grid=(t // tile,),
        in_specs=[pl.BlockSpec((tile, D_MODEL), lambda i: (i, 0)),
                  pl.BlockSpec((None, N_MOD, D_MODEL), mod_idx),
                  full((N_EXPERTS, D_MODEL)),
                  full((N_EXPERTS, 1)),
                  hbm, hbm, hbm,
                  full((D_MODEL, EXPERT_HIDDEN)),
                  full((D_MODEL, EXPERT_HIDDEN)),
                  full((EXPERT_HIDDEN, D_MODEL)),
                  full((1, D_MODEL)),
                  full((1, D_MODEL))],
        out_specs=pl.BlockSpec((tile, D_MODEL), lambda i: (i, 0)),
        out_shape=jax.ShapeDtypeStruct((t, D_MODEL), F32),
        scratch_shapes=[pltpu.VMEM((tile, D_MODEL), BF16),
                        pltpu.VMEM((tile, LANES), F32),
                        pltpu.VMEM((tile, D_MODEL), F32),
                        pltpu.VMEM((2, MOE_EPS, D_MODEL, EXPERT_HIDDEN), F32),
                        pltpu.VMEM((2, MOE_EPS, D_MODEL, EXPERT_HIDDEN), F32),
                        pltpu.VMEM((2, MOE_EPS, EXPERT_HIDDEN, D_MODEL), F32),
                        pltpu.SemaphoreType.DMA((3, 2))],
        compiler_params=_params(1),
        name="moe",
    )(y, mods, rwt, rb, exp_w_gate, exp_w_up, exp_w_down, sg, su, sd,
      ln_g.reshape(1, D_MODEL), ln_b.reshape(1, D_MODEL))


def kernel(x_prompt, x_sample, c, state_ret, cache_na_k, cache_na_v, cache_att_k, cache_att_v, c_ctx,
           w_mod, b_mod, ln_g, ln_b, w_in_even, w_out_even, ret_decay_logit, na_rpb,
           w_in_odd, w_out_odd, conv_w, q_norm_g, k_norm_g, router_w, router_b,
           exp_w_gate, exp_w_up, exp_w_down, sh_w_gate, sh_w_up, sh_w_down):
    yp = x_prompt.reshape(BATCH * SEQ, D_MODEL)
    ys = x_sample.reshape(DEC_BATCH * DEC_SEQ, D_MODEL)

    cond = jnp.zeros((COND_ROWS, D_MODEL), F32).at[0].set(c_ctx).at[1:1 + DEC_BATCH].set(c)
    mods = _adaln(cond, w_mod, b_mod).reshape(DEPTH, COND_ROWS, N_MOD, D_MODEL)

    out = {}
    for l in range(DEPTH):
        i = l // 2
        mp = mods[l, 0:1]
        ms = mods[l, 1:1 + DEC_BATCH]
        if l % 2 == 0:
            decay_rows = jnp.broadcast_to(ret_decay_logit[i].reshape(2 * RET_HEADS, 1), (2 * RET_HEADS, LANES))
            dts = (BF16, BF16, BF16, F32, BF16, BF16, BF16)
            yp, st, na_k, na_v, w_in, w_out = _even_ctx(yp, mp, w_in_even, w_out_even, i, decay_rows,
                                                        ln_g[l, 0], ln_b[l, 0])
            sp = _inproj(ys, ms, w_in, EVEN_SIZES, dts, DEC_SEQ)
            mix_s = _even_lat(*sp, state_ret, cache_na_k, cache_na_v, i, decay_rows, na_rpb[i].reshape(-1))
            ys = _outproj(mix_s, ys, ms, w_out, ln_g[l, 0], ln_b[l, 0], DEC_SEQ)
            out.setdefault("st", []).append(st.reshape(BATCH, 2, RET_HEADS, RET_DK, RET_DV))
            out.setdefault("na_k", []).append(na_k.reshape(BATCH, SEQ, NA_HEADS, NA_DH))
            out.setdefault("na_v", []).append(na_v.reshape(BATCH, SEQ, NA_HEADS, NA_DH))
        else:
            qg = q_norm_g[i].reshape(1, ATT_DH)
            kg = k_norm_g[i].reshape(1, ATT_DH)
            yp, k_new, att_v, w_in, w_out = _odd_ctx(yp, mp, w_in_odd, w_out_odd, i, conv_w[i], qg, kg,
                                                     ln_g[l, 0], ln_b[l, 0])
            cos, sin = _rope_tables()
            ys = _odd_lat(ys, ms, w_in, cache_att_k, cache_att_v, i,
                          conv_w[i], qg, kg, cos, sin, w_out, ln_g[l, 0], ln_b[l, 0])
            out.setdefault("att_k", []).append(k_new.reshape(BATCH, SEQ, ATT_KV_HEADS, ATT_DH))
            out.setdefault("att_v", []).append(att_v.reshape(BATCH, SEQ, ATT_KV_HEADS, ATT_DH))
        rwt = router_w[l].T
        rb = router_b[l].reshape(N_EXPERTS, 1)
        moe_w = (rwt, rb, exp_w_gate, exp_w_up, exp_w_down, sh_w_gate[l], sh_w_up[l], sh_w_down[l],
                 ln_g[l, 1], ln_b[l, 1])
        yp = _moe(yp, mp, l, *moe_w, SEQ)
        ys = _moe(ys, ms, l, *moe_w, DEC_SEQ)

    return (yp.reshape(BATCH, SEQ, D_MODEL),
            ys.reshape(DEC_BATCH, DEC_SEQ, D_MODEL),
            jnp.stack(out["st"], axis=1),
            jnp.stack(out["na_k"], axis=1),
            jnp.stack(out["na_v"], axis=1),
            jnp.stack(out["att_k"], axis=1),
            jnp.stack(out["att_v"], axis=1))
```

```python
import functools

import jax
import jax.numpy as jnp
from jax import lax
from jax.experimental import pallas as pl
from jax.experimental.pallas import tpu as pltpu

F32 = jnp.float32
BF16 = jnp.bfloat16

D_MODEL = 1024
BATCH = 32
SEQ = 256
DEPTH = 2
DEC_BATCH = 4
DEC_SEQ = 1024
PAST_LEN = 512
GRID_W = 64
GRID_H = DEC_SEQ // GRID_W
MIX_HALF = D_MODEL // 2
RET_HEADS = 4
RET_DV = MIX_HALF // RET_HEADS
RET_DK = RET_DV // 2
NA_HEADS = 4
NA_DH = MIX_HALF // NA_HEADS
NA_WIN_R = 8
NA_WIN_C = 16
CONV_CH = MIX_HALF
ATT_HEADS = 4
ATT_KV_HEADS = 2
ATT_DH = MIX_HALF // ATT_HEADS
ROPE_THETA = 10000.0
N_EXPERTS = 64
EXPERT_HIDDEN = D_MODEL // 8
TOP_K = 8
N_GROUPS = 8
GROUP_SIZE = N_EXPERTS // N_GROUPS
TOPK_GROUPS = 4
ROUTED_SCALE = 2.5
DEEPNORM_ALPHA = (2 * DEPTH) ** 0.25
LN_EPS = 1e-5
RMS_EPS = 1e-6
EVEN_SIZES = (RET_HEADS * RET_DK, RET_HEADS * RET_DK, RET_HEADS * RET_DV, RET_HEADS * RET_DV,
              NA_HEADS * NA_DH, NA_HEADS * NA_DH, NA_HEADS * NA_DH)
ODD_SIZES = (CONV_CH, CONV_CH, CONV_CH, ATT_HEADS * ATT_DH, ATT_KV_HEADS * ATT_DH, ATT_KV_HEADS * ATT_DH)
N_MOD = 6
RPB_ROWS = 2 * NA_WIN_R - 1
RPB_COLS = 2 * NA_WIN_C - 1

LANES = 128
COND_ROWS = 16
V7X_VMEM_LIMIT = 56 * 1024 * 1024

PROJ_TILE = 1024
LAT_QBLK = 256
MOE_TILE = 1024
MOE_EPS = 4
ADALN_TN = 2048


def _params(n_axes, vmem_limit=V7X_VMEM_LIMIT):
    return pltpu.CompilerParams(dimension_semantics=("arbitrary",) * n_axes,
                                vmem_limit_bytes=vmem_limit)


def _dot(a, b):
    return jnp.dot(a, b, preferred_element_type=F32)


def _dot_nt(a, b):
    return lax.dot_general(a, b, (((1,), (1,)), ((), ())), preferred_element_type=F32)


def _split(a):
    hi = a.astype(BF16)
    lo = (a - hi.astype(F32)).astype(BF16)
    return hi, lo


def _silu(x):
    return x / (1.0 + jnp.exp(-x))


def _log_sigmoid(x):
    return jnp.minimum(x, 0.0) - jnp.log(1.0 + jnp.exp(-jnp.abs(x)))


def _layer_norm(z, g, b):
    mu = jnp.mean(z, axis=-1, keepdims=True)
    zc = z - mu
    var = jnp.mean(zc * zc, axis=-1, keepdims=True)
    return zc * lax.rsqrt(var + LN_EPS) * g + b


def _post_norm(x, y, gate, g, b):
    return _layer_norm(DEEPNORM_ALPHA * x + (1.0 + gate) * y, g, b)


def _head_norm(x):
    mu = jnp.mean(x, axis=-1, keepdims=True)
    xc = x - mu
    var = jnp.mean(xc * xc, axis=-1, keepdims=True)
    return xc * lax.rsqrt(var + LN_EPS)


def _rms_norm(x, g):
    return x * lax.rsqrt(jnp.mean(x * x, axis=-1, keepdims=True) + RMS_EPS) * g


def _softmax_pv(parts):
    m = None
    for s, _ in parts:
        mi = jnp.max(s, axis=-1, keepdims=True)
        m = mi if m is None else jnp.maximum(m, mi)
    l = None
    o = None
    for s, v in parts:
        p = jnp.exp(s - m)
        li = jnp.sum(p, axis=-1, keepdims=True)
        oi = _dot(p.astype(BF16), v)
        l = li if l is None else l + li
        o = oi if o is None else o + oi
    return o / l


def _adaln_kernel(c_ref, w_ref, b_ref, o_ref):
    a_hi, a_lo = _split(_silu(c_ref[...]))
    w_hi, w_lo = _split(w_ref[...])
    o_ref[...] = _dot(a_hi, w_hi) + _dot(a_lo, w_hi) + _dot(a_hi, w_lo) + b_ref[...]


def _adaln(cond, w_mod, b_mod):
    n = w_mod.shape[-1]
    return pl.pallas_call(
        _adaln_kernel,
        grid=(DEPTH, n // ADALN_TN),
        in_specs=[pl.BlockSpec((COND_ROWS, D_MODEL), lambda l, j: (0, 0)),
                  pl.BlockSpec((None, D_MODEL, ADALN_TN), lambda l, j: (l, 0, j)),
                  pl.BlockSpec((None, 1, ADALN_TN), lambda l, j: (l, 0, j))],
        out_specs=pl.BlockSpec((None, COND_ROWS, ADALN_TN), lambda l, j: (l, 0, j)),
        out_shape=jax.ShapeDtypeStruct((DEPTH, COND_ROWS, n), F32),
        compiler_params=_params(2),
        name="adaln",
    )(cond, w_mod, b_mod.reshape(DEPTH, 1, n))


def _inproj_kernel(x_ref, mod_ref, w_ref, *out_refs, sizes):
    shift = mod_ref[0:1, :]
    scale = mod_ref[1:2, :]
    h = (x_ref[...] * (1.0 + scale) + shift).astype(BF16)
    y = _dot(h, w_ref[...])
    for o_ref, off, n in zip(out_refs, _offsets(sizes), sizes):
        o_ref[...] = y[:, off:off + n].astype(o_ref.dtype)


def _inproj(x, mods, w_bf16, sizes, dtypes, seq_len):
    t = x.shape[0]
    tile = PROJ_TILE
    mod_idx = (lambda i: (i * tile // seq_len, 0, 0)) if mods.shape[0] > 1 else (lambda i: (0, 0, 0))
    return pl.pallas_call(
        functools.partial(_inproj_kernel, sizes=sizes),
        grid=(t // tile,),
        in_specs=[pl.BlockSpec((tile, D_MODEL), lambda i: (i, 0)),
                  pl.BlockSpec((None, N_MOD, D_MODEL), mod_idx),
                  pl.BlockSpec(w_bf16.shape, lambda i: (0, 0))],
        out_specs=[pl.BlockSpec((tile, n), lambda i: (i, 0)) for n in sizes],
        out_shape=[jax.ShapeDtypeStruct((t, n), dt) for n, dt in zip(sizes, dtypes)],
        compiler_params=_params(1),
        name="inproj",
    )(x, mods, w_bf16)


def _outproj_kernel(mix_ref, x_ref, mod_ref, w_ref, g_ref, b_ref, o_ref):
    y = _dot(mix_ref[...], w_ref[...])
    o_ref[...] = _post_norm(x_ref[...], y, mod_ref[2:3, :], g_ref[...], b_ref[...])


def _outproj(mix, x, mods, w_bf16, ln_g, ln_b, seq_len):
    t = x.shape[0]
    tile = PROJ_TILE
    mod_idx = (lambda i: (i * tile // seq_len, 0, 0)) if mods.shape[0] > 1 else (lambda i: (0, 0, 0))
    return pl.pallas_call(
        _outproj_kernel,
        grid=(t // tile,),
        in_specs=[pl.BlockSpec((tile, D_MODEL), lambda i: (i, 0)),
                  pl.BlockSpec((tile, D_MODEL), lambda i: (i, 0)),
                  pl.BlockSpec((None, N_MOD, D_MODEL), mod_idx),
                  pl.BlockSpec((D_MODEL, D_MODEL), lambda i: (0, 0)),
                  pl.BlockSpec((1, D_MODEL), lambda i: (0, 0)),
                  pl.BlockSpec((1, D_MODEL), lambda i: (0, 0))],
        out_specs=pl.BlockSpec((tile, D_MODEL), lambda i: (i, 0)),
        out_shape=jax.ShapeDtypeStruct((t, D_MODEL), F32),
        compiler_params=_params(1),
        name="outproj",
    )(mix, x, mods, w_bf16, ln_g.reshape(1, D_MODEL), ln_b.reshape(1, D_MODEL))


def _decay_matrix(length, lgf, lgb, row0=0, rows=None):
    rows = length if rows is None else rows
    ii = lax.broadcasted_iota(jnp.int32, (rows, length), 0) + row0
    jj = lax.broadcasted_iota(jnp.int32, (rows, length), 1)
    rel = (ii - jj).astype(F32)
    fwd = jnp.where(rel >= 0.0, jnp.exp(lgf * jnp.maximum(rel, 0.0)), 0.0)
    bwd = jnp.where(rel <= 0.0, jnp.exp(lgb * jnp.maximum(-rel, 0.0)), 0.0)
    return fwd + bwd


def _retention_head(q, k, v, lgf, lgb, s0f, s0b, dmat=None):
    length = q.shape[0]
    s = _dot_nt(q.astype(BF16), k.astype(BF16))
    if dmat is None:
        dmat = _decay_matrix(length, lgf, lgb)
    if dmat.dtype == BF16:
        o = _dot(s.astype(BF16) * dmat, v)
    else:
        o = _dot((s * dmat).astype(BF16), v)
    if s0f is not None:
        pos = lax.broadcasted_iota(jnp.int32, (length, 1), 0).astype(F32)
        qf = q * jnp.exp(lgf * (pos + 1.0))
        qb = q * jnp.exp(lgb * (length - pos))
        o = o + _dot(qf.astype(BF16), s0f.astype(BF16)) + _dot(qb.astype(BF16), s0b.astype(BF16))
    return o


def _offsets(sizes):
    return [sum(sizes[:k]) for k in range(len(sizes))]


def _project(x_ref, mod_ref, wi_ref, dst_ref):
    h = (x_ref[...] * (1.0 + mod_ref[1:2, :]) + mod_ref[0:1, :]).astype(BF16)
    dst_ref[...] = _dot(h, wi_ref[...])


def _cast_rows(src_ref, dst_ref):
    def body(r, carry):
        rows = pl.ds(pl.multiple_of(r * LANES, LANES), LANES)
        dst_ref[rows, :] = src_ref[rows, :].astype(BF16)
        return carry
    lax.fori_loop(0, src_ref.shape[0] // LANES, body, 0)


def _project_next(xn_ref, xc_ref, mod_ref, wif_ref, wof_ref, wi_ref, wo_ref, nxt_ref, cur_ref):
    @pl.when(pl.program_id(0) == 0)
    def _():
        _cast_rows(wif_ref, wi_ref)
        _cast_rows(wof_ref, wo_ref)
        _project(xc_ref, mod_ref, wi_ref, cur_ref)

    _project(xn_ref, mod_ref, wi_ref, nxt_ref)
    return cur_ref


def _even_ctx_kernel(xn_ref, xc_ref, mod_ref, wif_ref, dl_ref, wof_ref, lng_ref, lnb_ref,
                     y_ref, st_ref, nak_ref, nav_ref, wi_ref, wo_ref,
                     nxt_ref, cur_ref, mix_ref, dmat_ref, kdec_ref):
    length = xc_ref.shape[0]

    @pl.when(pl.program_id(0) == 0)
    def _():
        lg = _log_sigmoid(dl_ref[...])
        pos = lax.broadcasted_iota(jnp.int32, (length, 1), 0).astype(F32)
        for h in range(RET_HEADS):
            lgf = lg[h:h + 1, 0:1]
            lgb = lg[RET_HEADS + h:RET_HEADS + h + 1, 0:1]
            dmat_ref[h] = _decay_matrix(length, lgf, lgb)
            kdec_ref[h] = jnp.broadcast_to(jnp.exp(lgf * (length - 1.0 - pos)), (length, RET_DK))
            kdec_ref[RET_HEADS + h] = jnp.broadcast_to(jnp.exp(lgb * pos), (length, RET_DK))

    cur = _project_next(xn_ref, xc_ref, mod_ref, wif_ref, wof_ref, wi_ref, wo_ref, nxt_ref, cur_ref)
    o_qr, o_kr, o_vr, o_gr, o_qn, o_kn, o_vn = _offsets(EVEN_SIZES)
    eye = jnp.where(lax.broadcasted_iota(jnp.int32, (2 * RET_DK, 2 * RET_DK), 0)
                    == lax.broadcasted_iota(jnp.int32, (2 * RET_DK, 2 * RET_DK), 1), 1.0, 0.0).astype(BF16)
    for h in range(RET_HEADS):
        q = cur[:, o_qr + h * RET_DK:o_qr + (h + 1) * RET_DK]
        k = cur[:, o_kr + h * RET_DK:o_kr + (h + 1) * RET_DK] * (RET_DK ** -0.5)
        v = cur[:, o_vr + h * RET_DV:o_vr + (h + 1) * RET_DV].astype(BF16)
        o = _retention_head(q, k, v, None, None, None, None, dmat=dmat_ref[h])
        k_dec = jnp.concatenate([k * kdec_ref[h], k * kdec_ref[RET_HEADS + h]], axis=1).astype(BF16)
        states = _dot(_dot_nt(eye, k_dec).astype(BF16), v)
        st_ref[h] = states[:RET_DK]
        st_ref[RET_HEADS + h] = states[RET_DK:]
        g = cur[:, o_gr + h * RET_DV:o_gr + (h + 1) * RET_DV]
        mix_ref[:, h * RET_DV:(h + 1) * RET_DV] = (_head_norm(o) * _silu(g)).astype(mix_ref.dtype)
    base = RET_HEADS * RET_DV
    for h in range(NA_HEADS):
        q = cur[:, o_qn + h * NA_DH:o_qn + (h + 1) * NA_DH].astype(BF16)
        k = cur[:, o_kn + h * NA_DH:o_kn + (h + 1) * NA_DH]
        v = cur[:, o_vn + h * NA_DH:o_vn + (h + 1) * NA_DH]
        nak_ref[:, h, :] = k
        nav_ref[:, h, :] = v
        s = _dot_nt(q, k.astype(BF16)) * (NA_DH ** -0.5)
        mix_ref[:, base + h * NA_DH:base + (h + 1) * NA_DH] = _softmax_pv([(s, v.astype(BF16))]).astype(mix_ref.dtype)
    y = _dot(mix_ref[...], wo_ref[...])
    y_ref[...] = _post_norm(xc_ref[...], y, mod_ref[2:3, :], lng_ref[...], lnb_ref[...])
    cur_ref[...] = nxt_ref[...]


def _ctx_layer_specs(n_proj, idx):
    def nxt(n):
        return pl.BlockSpec((SEQ, n), lambda i: (jnp.minimum(i + 1, BATCH - 1), 0))

    def cur(shape):
        return pl.BlockSpec(shape, lambda i: (i,) + (0,) * (len(shape) - 1))

    def full(shape):
        return pl.BlockSpec(shape, lambda i: (0,) * len(shape))

    def weight(n):
        return pl.BlockSpec((None, D_MODEL, n), lambda i: (idx, 0, 0), pipeline_mode=pl.Buffered(1))
    head = [nxt(D_MODEL), cur((SEQ, D_MODEL)), full((None, N_MOD, D_MODEL)), weight(n_proj)]
    tail = [weight(D_MODEL), full((1, D_MODEL)), full((1, D_MODEL))]
    w_specs = [full((D_MODEL, n_proj)), full((D_MODEL, D_MODEL))]
    w_shapes = [jax.ShapeDtypeStruct((D_MODEL, n_proj), BF16), jax.ShapeDtypeStruct((D_MODEL, D_MODEL), BF16)]
    return head, tail, cur, full, w_specs, w_shapes


def _even_ctx(x, mods, w_in, w_out, idx, decay_rows, ln_g, ln_b):
    head, tail, cur, full, w_specs, w_shapes = _ctx_layer_specs(sum(EVEN_SIZES), idx)
    return pl.pallas_call(
        _even_ctx_kernel,
        grid=(BATCH,),
        in_specs=head + [full((2 * RET_HEADS, LANES))] + tail,
        out_specs=[cur((SEQ, D_MODEL)),
                   cur((None, 2 * RET_HEADS, RET_DK, RET_DV)),
                   cur((SEQ, NA_HEADS, NA_DH)), cur((SEQ, NA_HEADS, NA_DH))] + w_specs,
        out_shape=[jax.ShapeDtypeStruct((BATCH * SEQ, D_MODEL), F32),
                   jax.ShapeDtypeStruct((BATCH, 2 * RET_HEADS, RET_DK, RET_DV), F32),
                   jax.ShapeDtypeStruct((BATCH * SEQ, NA_HEADS, NA_DH), F32),
                   jax.ShapeDtypeStruct((BATCH * SEQ, NA_HEADS, NA_DH), F32)] + w_shapes,
        scratch_shapes=[pltpu.VMEM((SEQ, sum(EVEN_SIZES)), F32),
                        pltpu.VMEM((SEQ, sum(EVEN_SIZES)), F32),
                        pltpu.VMEM((SEQ, D_MODEL), BF16),
                        pltpu.VMEM((RET_HEADS, SEQ, SEQ), F32),
                        pltpu.VMEM((2 * RET_HEADS, SEQ, RET_DK), F32)],
        compiler_params=_params(1),
        name="even_ctx",
    )(x, x, mods, w_in, decay_rows, w_out, ln_g.reshape(1, D_MODEL), ln_b.reshape(1, D_MODEL))


def _build_rpb_tiles(rpb_ref, tile_ref):
    qc = lax.broadcasted_iota(jnp.int32, (GRID_W, LANES), 0)
    kc = lax.broadcasted_iota(jnp.int32, (GRID_W, LANES), 1) % GRID_W
    diff = kc - qc + (NA_WIN_C - 1)
    start = jnp.clip(qc - NA_WIN_C // 2, 0, GRID_W - NA_WIN_C)
    win = (kc >= start) & (kc < start + NA_WIN_C)

    def body(idx, carry):
        t = jnp.zeros((GRID_W, LANES), F32)
        for d in range(RPB_COLS):
            t = jnp.where(diff == d, rpb_ref[idx * RPB_COLS + d], t)
        tile_ref[idx] = jnp.where(win, t, -jnp.inf)
        return carry

    lax.fori_loop(0, NA_HEADS * RPB_ROWS, body, 0)


NA_Q_ROWS = 8
NA_KEY_ROWS = 12


def _band_start(qr):
    rows_w = min(NA_WIN_R, GRID_H)
    return min(max(qr - rows_w // 2, 0), GRID_H - rows_w)


def _key_window(qrow0):
    w0 = min(_band_start(qrow0), GRID_H - NA_KEY_ROWS)
    assert _band_start(qrow0 + NA_Q_ROWS - 1) + min(NA_WIN_R, GRID_H) <= w0 + NA_KEY_ROWS
    return w0


def _build_bias(tile_ref, bias_ref, head, qrow0):
    left = lax.broadcasted_iota(jnp.int32, (GRID_W, LANES), 1) < GRID_W
    neg = jnp.full((GRID_W, LANES), -jnp.inf, F32)
    rows_w = min(NA_WIN_R, GRID_H)
    w0 = _key_window(qrow0)
    for dq in range(NA_Q_ROWS):
        qr = qrow0 + dq
        rs = _band_start(qr)

        def tile(kr):
            if rs <= kr < rs + rows_w:
                return tile_ref[head * RPB_ROWS + kr - qr + NA_WIN_R - 1]
            return neg

        pieces = [jnp.where(left, tile(w0 + 2 * a), tile(w0 + 2 * a + 1)) for a in range(NA_KEY_ROWS // 2)]
        bias_ref[dq * GRID_W:(dq + 1) * GRID_W, :] = jnp.concatenate(pieces, axis=1)


def _even_lat_kernel(qr_ref, kr_ref, vr_ref, gr_ref, qn_ref, kn_ref, vn_ref, s0_ref, ck_hbm, cv_hbm,
                     dl_ref, rpb_ref, mix_ref, tile_ref, bias_ref, dmat_ref, cache_buf, sem, *, idx):
    length = qr_ref.shape[0]
    lg = _log_sigmoid(dl_ref[...])

    def cache_copies():
        b = pl.program_id(0)
        return [pltpu.make_async_copy(src.at[b, idx, :, h, :], cache_buf.at[a, h], sem.at[a, h])
                for a, src in enumerate((ck_hbm, cv_hbm)) for h in range(NA_HEADS)]

    for n, cp in enumerate(cache_copies()):
        cp.start(priority=n % 2)

    @pl.when(pl.program_id(0) == 0)
    def _():
        _build_rpb_tiles(rpb_ref, tile_ref)
        for h in range(RET_HEADS):
            lgf = lg[h:h + 1, 0:1]
            lgb = lg[RET_HEADS + h:RET_HEADS + h + 1, 0:1]

            def rows_body(r, carry, h=h, lgf=lgf, lgb=lgb):
                r0 = pl.multiple_of(r * LANES, LANES)
                dmat_ref[h, pl.ds(r0, LANES), :] = _decay_matrix(length, lgf, lgb, r0, LANES).astype(BF16)
                return carry

            lax.fori_loop(0, length // LANES, rows_body, 0)

    for h in range(RET_HEADS):
        lgf = lg[h:h + 1, 0:1]
        lgb = lg[RET_HEADS + h:RET_HEADS + h + 1, 0:1]
        q = qr_ref[:, h * RET_DK:(h + 1) * RET_DK].astype(F32)
        k = kr_ref[:, h * RET_DK:(h + 1) * RET_DK].astype(F32) * (RET_DK ** -0.5)
        v = vr_ref[:, h * RET_DV:(h + 1) * RET_DV]
        o = _retention_head(q, k, v, lgf, lgb, s0_ref[0, h], s0_ref[1, h], dmat=dmat_ref[h])
        g = gr_ref[:, h * RET_DV:(h + 1) * RET_DV]
        mix_ref[:, h * RET_DV:(h + 1) * RET_DV] = (_head_norm(o) * _silu(g)).astype(mix_ref.dtype)
    base = RET_HEADS * RET_DV
    scale = NA_DH ** -0.5
    for cp in cache_copies():
        cp.wait()
    for h in range(NA_HEADS):
        sl = slice(h * NA_DH, (h + 1) * NA_DH)
        kc = cache_buf[0, h].astype(BF16)
        vc = cache_buf[1, h].astype(BF16)
        for qrow0 in range(0, GRID_H, NA_Q_ROWS):
            _build_bias(tile_ref, bias_ref, h, qrow0)
            rows = slice(qrow0 * GRID_W, (qrow0 + NA_Q_ROWS) * GRID_W)
            k0 = _key_window(qrow0) * GRID_W
            keys = slice(k0, k0 + NA_KEY_ROWS * GRID_W)
            q = qn_ref[rows, sl]
            s_band = _dot_nt(q, kn_ref[keys, sl]) * scale + bias_ref[...]
            s_ctx = _dot_nt(q, kc) * scale
            o = _softmax_pv([(s_band, vn_ref[keys, sl]), (s_ctx, vc)])
            mix_ref[rows, base + h * NA_DH:base + (h + 1) * NA_DH] = o.astype(mix_ref.dtype)


def _even_lat(qr, kr, vr, gr, qn, kn, vn, state, cache_k, cache_v, idx, decay_rows, rpb_flat):
    def seq_spec(n):
        return pl.BlockSpec((DEC_SEQ, n), lambda i: (i, 0))
    cache_spec = pl.BlockSpec(memory_space=pl.ANY)
    return pl.pallas_call(
        functools.partial(_even_lat_kernel, idx=idx),
        grid=(DEC_BATCH,),
        in_specs=[seq_spec(n) for n in EVEN_SIZES]
        + [pl.BlockSpec((None, None, 2, RET_HEADS, RET_DK, RET_DV), lambda i: (i, idx, 0, 0, 0, 0)),
           cache_spec, cache_spec,
           pl.BlockSpec((2 * RET_HEADS, LANES), lambda i: (0, 0)),
           pl.BlockSpec(memory_space=pltpu.SMEM)],
        out_specs=seq_spec(D_MODEL),
        out_shape=jax.ShapeDtypeStruct((DEC_BATCH * DEC_SEQ, D_MODEL), BF16),
        scratch_shapes=[pltpu.VMEM((NA_HEADS * RPB_ROWS, GRID_W, LANES), F32),
                        pltpu.VMEM((NA_Q_ROWS * GRID_W, NA_KEY_ROWS * GRID_W), F32),
                        pltpu.VMEM((RET_HEADS, DEC_SEQ, DEC_SEQ), BF16),
                        pltpu.VMEM((2, NA_HEADS, PAST_LEN, NA_DH), F32),
                        pltpu.SemaphoreType.DMA((2, NA_HEADS))],
        compiler_params=_params(1),
        name="even_lat",
    )(qr, kr, vr, gr, qn, kn, vn, state, cache_k, cache_v, decay_rows, rpb_flat)


def _gated_conv(bg, cg, u, w_ref):
    length, ch = u.shape
    z = cg * u
    row = lax.broadcasted_iota(jnp.int32, (length, ch), 0)
    z_prev = jnp.where(row == 0, 0.0, pltpu.roll(z, 1, 0))
    z_next = jnp.where(row == length - 1, 0.0, pltpu.roll(z, length - 1, 0))
    return bg * (z_prev * w_ref[0:1, :] + z * w_ref[1:2, :] + z_next * w_ref[2:3, :])


def _odd_ctx_kernel(xn_ref, xc_ref, mod_ref, wif_ref, w_ref, qg_ref, kg_ref, wof_ref, lng_ref, lnb_ref,
                    y_ref, ko_ref, vo_ref, wi_ref, wo_ref, nxt_ref, cur_ref, mix_ref):
    cur = _project_next(xn_ref, xc_ref, mod_ref, wif_ref, wof_ref, wi_ref, wo_ref, nxt_ref, cur_ref)
    o_bg, o_cg, o_u, o_q, o_k, o_v = _offsets(ODD_SIZES)
    conv = _gated_conv(cur[:, o_bg:o_bg + CONV_CH], cur[:, o_cg:o_cg + CONV_CH], cur[:, o_u:o_u + CONV_CH], w_ref)
    mix_ref[:, 0:CONV_CH] = conv.astype(mix_ref.dtype)
    group = ATT_HEADS // ATT_KV_HEADS
    scale = ATT_DH ** -0.5
    for kv in range(ATT_KV_HEADS):
        k = _rms_norm(cur[:, o_k + kv * ATT_DH:o_k + (kv + 1) * ATT_DH], kg_ref[...])
        v = cur[:, o_v + kv * ATT_DH:o_v + (kv + 1) * ATT_DH]
        ko_ref[:, kv, :] = k
        vo_ref[:, kv, :] = v
        k = k.astype(BF16)
        v = v.astype(BF16)
        for g in range(group):
            h = kv * group + g
            q = _rms_norm(cur[:, o_q + h * ATT_DH:o_q + (h + 1) * ATT_DH], qg_ref[...]).astype(BF16)
            o = _softmax_pv([(_dot_nt(q, k) * scale, v)])
            mix_ref[:, CONV_CH + h * ATT_DH:CONV_CH + (h + 1) * ATT_DH] = o.astype(mix_ref.dtype)
    y = _dot(mix_ref[...], wo_ref[...])
    y_ref[...] = _post_norm(xc_ref[...], y, mod_ref[2:3, :], lng_ref[...], lnb_ref[...])
    cur_ref[...] = nxt_ref[...]


def _odd_ctx(x, mods, w_in, w_out, idx, conv_w, qn_g, kn_g, ln_g, ln_b):
    head, tail, cur, full, w_specs, w_shapes = _ctx_layer_specs(sum(ODD_SIZES), idx)
    kv_shape = (SEQ, ATT_KV_HEADS, ATT_DH)
    return pl.pallas_call(
        _odd_ctx_kernel,
        grid=(BATCH,),
        in_specs=head + [full((3, CONV_CH)), full((1, ATT_DH)), full((1, ATT_DH))] + tail,
        out_specs=[cur((SEQ, D_MODEL)), cur(kv_shape), cur(kv_shape)] + w_specs,
        out_shape=[jax.ShapeDtypeStruct((BATCH * SEQ, D_MODEL), F32),
                   jax.ShapeDtypeStruct((BATCH * SEQ, ATT_KV_HEADS, ATT_DH), F32),
                   jax.ShapeDtypeStruct((BATCH * SEQ, ATT_KV_HEADS, ATT_DH), F32)] + w_shapes,
        scratch_shapes=[pltpu.VMEM((SEQ, sum(ODD_SIZES)), F32),
                        pltpu.VMEM((SEQ, sum(ODD_SIZES)), F32),
                        pltpu.VMEM((SEQ, D_MODEL), BF16)],
        compiler_params=_params(1),
        name="odd_ctx",
    )(x, x, mods, w_in, conv_w, qn_g, kn_g, w_out, ln_g.reshape(1, D_MODEL), ln_b.reshape(1, D_MODEL))


def _rope(x, cos, sin, even_lane):
    swapped = jnp.where(even_lane, pltpu.roll(x, LANES - 1, 1), pltpu.roll(x, 1, 1))
    return x * cos + swapped * sin


def _odd_lat_kernel(x_ref, mod_ref, wi_ref, ck_ref, cv_ref, w_ref, qg_ref, kg_ref, cos_ref, sin_ref,
                    wo_ref, lng_ref, lnb_ref, y_ref, proj_ref, qb_ref, kb_ref, vb_ref, mix_ref):
    length = x_ref.shape[0]
    h_in = (x_ref[...] * (1.0 + mod_ref[1:2, :]) + mod_ref[0:1, :]).astype(BF16)
    proj_ref[...] = _dot(h_in, wi_ref[...])
    o_bg, o_cg, o_u, o_q, o_k, o_v = _offsets(ODD_SIZES)
    conv = _gated_conv(proj_ref[:, o_bg:o_bg + CONV_CH], proj_ref[:, o_cg:o_cg + CONV_CH],
                       proj_ref[:, o_u:o_u + CONV_CH], w_ref)
    mix_ref[:, 0:CONV_CH] = conv.astype(mix_ref.dtype)
    group = ATT_HEADS // ATT_KV_HEADS
    scale = ATT_DH ** -0.5
    even_lane = lax.broadcasted_iota(jnp.int32, (length, ATT_DH), 1) % 2 == 0
    cos = cos_ref[...]
    sin = sin_ref[...]
    for kv in range(ATT_KV_HEADS):
        sl = slice(kv * ATT_DH, (kv + 1) * ATT_DH)
        k = _rms_norm(proj_ref[:, o_k + kv * ATT_DH:o_k + (kv + 1) * ATT_DH], kg_ref[...])
        kb_ref[0:length, sl] = _rope(k, cos, sin, even_lane).astype(BF16)
        vb_ref[0:length, sl] = proj_ref[:, o_v + kv * ATT_DH:o_v + (kv + 1) * ATT_DH].astype(BF16)
        kb_ref[length:, sl] = ck_ref[:, kv, :].astype(BF16)
        vb_ref[length:, sl] = cv_ref[:, kv, :].astype(BF16)
    for h in range(ATT_HEADS):
        q = _rms_norm(proj_ref[:, o_q + h * ATT_DH:o_q + (h + 1) * ATT_DH], qg_ref[...])
        qb_ref[:, h * ATT_DH:(h + 1) * ATT_DH] = _rope(q, cos, sin, even_lane).astype(BF16)

    def q_block(b, carry):
        rows = pl.ds(pl.multiple_of(b * LAT_QBLK, LAT_QBLK), LAT_QBLK)
        for h in range(ATT_HEADS):
            sl = slice((h // group) * ATT_DH, (h // group + 1) * ATT_DH)
            q = qb_ref[rows, h * ATT_DH:(h + 1) * ATT_DH]
            o = _softmax_pv([(_dot_nt(q, kb_ref[:, sl]) * scale, vb_ref[:, sl])])
            mix_ref[rows, CONV_CH + h * ATT_DH:CONV_CH + (h + 1) * ATT_DH] = o.astype(mix_ref.dtype)
        return carry

    lax.fori_loop(0, length // LAT_QBLK, q_block, 0)
    y = _dot(mix_ref[...], wo_ref[...])
    y_ref[...] = _post_norm(x_ref[...], y, mod_ref[2:3, :], lng_ref[...], lnb_ref[...])


def _odd_lat(x, mods, w_in, cache_k, cache_v, idx, conv_w, qn_g, kn_g, cos, sin, w_out, ln_g, ln_b):
    def full(shape):
        return pl.BlockSpec(shape, lambda i: (0,) * len(shape))

    def const(shape):
        return pl.BlockSpec(shape, lambda i: (0,) * len(shape), pipeline_mode=pl.Buffered(1))
    kv_w = ATT_KV_HEADS * ATT_DH
    cache_spec = pl.BlockSpec((None, None, PAST_LEN, ATT_KV_HEADS, ATT_DH), lambda i: (i, idx, 0, 0, 0))
    return pl.pallas_call(
        _odd_lat_kernel,
        grid=(DEC_BATCH,),
        in_specs=[pl.BlockSpec((DEC_SEQ, D_MODEL), lambda i: (i, 0)),
                  pl.BlockSpec((None, N_MOD, D_MODEL), lambda i: (i, 0, 0)),
                  const((D_MODEL, sum(ODD_SIZES))),
                  cache_spec, cache_spec, full((3, CONV_CH)), full((1, ATT_DH)), full((1, ATT_DH)),
                  const((DEC_SEQ, ATT_DH)), const((DEC_SEQ, ATT_DH)),
                  const((D_MODEL, D_MODEL)), full((1, D_MODEL)), full((1, D_MODEL))],
        out_specs=pl.BlockSpec((DEC_SEQ, D_MODEL), lambda i: (i, 0)),
        out_shape=jax.ShapeDtypeStruct((DEC_BATCH * DEC_SEQ, D_MODEL), F32),
        scratch_shapes=[pltpu.VMEM((DEC_SEQ, sum(ODD_SIZES)), F32),
                        pltpu.VMEM((DEC_SEQ, ATT_HEADS * ATT_DH), BF16),
                        pltpu.VMEM((DEC_SEQ + PAST_LEN, kv_w), BF16),
                        pltpu.VMEM((DEC_SEQ + PAST_LEN, kv_w), BF16),
                        pltpu.VMEM((DEC_SEQ, D_MODEL), BF16)],
        compiler_params=_params(1),
        name="odd_lat",
    )(x, mods, w_in, cache_k, cache_v, conv_w, qn_g, kn_g, cos, sin, w_out,
      ln_g.reshape(1, D_MODEL), ln_b.reshape(1, D_MODEL))


def _rope_tables():
    t = jnp.arange(DEC_SEQ)
    row = (t // GRID_W).astype(F32)
    col = (t % GRID_W).astype(F32)
    n_freq = ATT_DH // 4
    freqs = ROPE_THETA ** (-jnp.arange(n_freq, dtype=F32) / n_freq)
    ang = jnp.concatenate([row[:, None] * freqs, col[:, None] * freqs], axis=-1)
    cos = jnp.repeat(jnp.cos(ang), 2, axis=-1)
    sin = jnp.stack([-jnp.sin(ang), jnp.sin(ang)], axis=-1).reshape(DEC_SEQ, ATT_DH)
    return cos, sin


def _route(scores, sel):
    n_tok = sel.shape[1]
    neg = -jnp.inf
    sub = lax.broadcasted_iota(jnp.int32, (GROUP_SIZE, n_tok), 0).astype(F32)
    blocks = [sel[g * GROUP_SIZE:(g + 1) * GROUP_SIZE, :] for g in range(N_GROUPS)]
    grp = []
    for blk in blocks:
        m1 = jnp.max(blk, axis=0, keepdims=True)
        i1 = jnp.min(jnp.where(blk == m1, sub, float(GROUP_SIZE)), axis=0, keepdims=True)
        m2 = jnp.max(jnp.where(sub == i1, neg, blk), axis=0, keepdims=True)
        grp.append(m1 + m2)
    masked = []
    for g in range(N_GROUPS):
        ahead = jnp.zeros((1, n_tok), F32)
        for o in range(N_GROUPS):
            if o == g:
                continue
            wins = grp[o] >= grp[g] if o < g else grp[o] > grp[g]
            ahead = ahead + jnp.where(wins, 1.0, 0.0)
        masked.append(jnp.where(ahead < float(TOPK_GROUPS), blocks[g], neg))
    val = jnp.concatenate(masked, axis=0)
    row = lax.broadcasted_iota(jnp.int32, (N_EXPERTS, n_tok), 0).astype(F32)
    w = jnp.zeros((N_EXPERTS, n_tok), F32)
    for _ in range(TOP_K):
        m = jnp.max(val, axis=0, keepdims=True)
        idx = jnp.min(jnp.where(val == m, row, float(N_EXPERTS)), axis=0, keepdims=True)
        pick = row == idx
        w = jnp.where(pick, scores, w)
        val = jnp.where(pick, neg, val)
    return w / jnp.sum(w, axis=0, keepdims=True) * ROUTED_SCALE


def _moe_kernel(y_ref, mod_ref, rwt_ref, rb_ref, wg_hbm, wu_hbm, wd_hbm, sg_ref, su_ref, sd_ref,
                lng_ref, lnb_ref, o_ref, xb_ref, gate_ref, acc_ref, wg_buf, wu_buf, wd_buf, sem,
                *, layer, n_steps):
    n_tok = y_ref.shape[0]

    def weight_copies(step, slot):
        experts = pl.ds(step * MOE_EPS, MOE_EPS)
        return (pltpu.make_async_copy(wg_hbm.at[layer, experts], wg_buf.at[slot], sem.at[0, slot]),
                pltpu.make_async_copy(wu_hbm.at[layer, experts], wu_buf.at[slot], sem.at[1, slot]),
                pltpu.make_async_copy(wd_hbm.at[layer, experts], wd_buf.at[slot], sem.at[2, slot]))

    for n, cp in enumerate(weight_copies(0, 0)):
        cp.start(priority=n % 2)

    x = y_ref[...] * (1.0 + mod_ref[4:5, :]) + mod_ref[3:4, :]
    x_hi, x_lo = _split(x)
    xb_ref[...] = x_hi
    w_hi, w_lo = _split(rwt_ref[...])
    logits = _dot_nt(w_hi, x_hi) + _dot_nt(w_lo, x_hi) + _dot_nt(w_hi, x_lo)
    scores = 1.0 / (1.0 + jnp.exp(-logits))
    gate_t = _route(scores, scores + rb_ref[...])
    gate_ref[...] = jnp.concatenate([gate_t, jnp.zeros_like(gate_t)], axis=0).T
    sgu = jnp.concatenate([sg_ref[...].astype(BF16), su_ref[...].astype(BF16)], axis=1)
    hs = _dot(x_hi, sgu)
    h_sh = (_silu(hs[:, :EXPERT_HIDDEN]) * hs[:, EXPERT_HIDDEN:]).astype(BF16)
    acc_ref[...] = _dot(h_sh, sd_ref[...].astype(BF16))

    def expert_group(step, slot):
        xb = xb_ref[...]
        g_rot = pltpu.roll(gate_ref[...], jnp.bitwise_and(LANES - step * MOE_EPS, LANES - 1), 1)
        acc = acc_ref[...]
        for p in range(MOE_EPS // 2):
            wg2 = jnp.concatenate([wg_buf[slot, 2 * p].astype(BF16), wg_buf[slot, 2 * p + 1].astype(BF16)], axis=1)
            wu2 = jnp.concatenate([wu_buf[slot, 2 * p].astype(BF16), wu_buf[slot, 2 * p + 1].astype(BF16)], axis=1)
            hg = _dot(xb, wg2)
            hu = _dot(xb, wu2)
            g2 = jnp.concatenate(
                [jnp.broadcast_to(g_rot[:, 2 * p:2 * p + 1], (n_tok, EXPERT_HIDDEN)),
                 jnp.broadcast_to(g_rot[:, 2 * p + 1:2 * p + 2], (n_tok, EXPERT_HIDDEN))], axis=1)
            h = (_silu(hg) * hu * g2).astype(BF16)
            wd2 = wd_buf[slot, 2 * p:2 * p + 2].reshape(2 * EXPERT_HIDDEN, D_MODEL).astype(BF16)
            acc = acc + _dot(h, wd2)
        acc_ref[...] = acc

    def two_groups(k, carry):
        step = 2 * k
        for cp in weight_copies(step, 0):
            cp.wait()
        for n, cp in enumerate(weight_copies(step + 1, 1)):
            cp.start(priority=n % 2)
        expert_group(step, 0)

        for cp in weight_copies(step + 1, 1):
            cp.wait()

        @pl.when(step + 2 < n_steps)
        def _():
            for n, cp in enumerate(weight_copies(step + 2, 0)):
                cp.start(priority=n % 2)

        expert_group(step + 1, 1)
        return carry

    lax.fori_loop(0, n_steps // 2, two_groups, 0)
    o_ref[...] = _post_norm(y_ref[...], acc_ref[...], mod_ref[5:6, :], lng_ref[...], lnb_ref[...])


def _moe(y, mods, layer, rwt, rb, exp_w_gate, exp_w_up, exp_w_down, sg, su, sd, ln_g, ln_b, seq_len):
    t = y.shape[0]
    tile = MOE_TILE
    n_steps = N_EXPERTS // MOE_EPS
    assert n_steps % 2 == 0
    mod_idx = (lambda i: (i * tile // seq_len, 0, 0)) if mods.shape[0] > 1 else (lambda i: (0, 0, 0))

    def full(shape):
        return pl.BlockSpec(shape, lambda i: (0,) * len(shape))
    hbm = pl.BlockSpec(memory_space=pl.ANY)
    return pl.pallas_call(
        functools.partial(_moe_kernel, layer=layer, n_steps=n_steps),
        grid=(t // tile,),
        in_specs=[pl.BlockSpec((tile, D_MODEL), lambda i: (i, 0)),
                  pl.BlockSpec((None, N_MOD, D_MODEL), mod_idx),
                  full((N_EXPERTS, D_MODEL)),
                  full((N_EXPERTS, 1)),
                  hbm, hbm, hbm,
                  full((D_MODEL, EXPERT_HIDDEN)),
                  full((D_MODEL, EXPERT_HIDDEN)),
                  full((EXPERT_HIDDEN, D_MODEL)),
                  full((1, D_MODEL)),
                  full((1, D_MODEL))],
        out_specs=pl.BlockSpec((tile, D_MODEL), lambda i: (i, 0)),
        out_shape=jax.ShapeDtypeStruct((t, D_MODEL), F32),
        scratch_shapes=[pltpu.VMEM((tile, D_MODEL), BF16),
                        pltpu.VMEM((tile, LANES), F32),
                        pltpu.VMEM((tile, D_MODEL), F32),
                        pltpu.VMEM((2, MOE_EPS, D_MODEL, EXPERT_HIDDEN), F32),
                        pltpu.VMEM((2, MOE_EPS, D_MODEL, EXPERT_HIDDEN), F32),
                        pltpu.VMEM((2, MOE_EPS, EXPERT_HIDDEN, D_MODEL), F32),
                        pltpu.SemaphoreType.DMA((3, 2))],
        compiler_params=_params(1),
        name="moe",
    )(y, mods, rwt, rb, exp_w_gate, exp_w_up, exp_w_down, sg, su, sd,
      ln_g.reshape(1, D_MODEL), ln_b.reshape(1, D_MODEL))


def kernel(x_prompt, x_sample, c, state_ret, cache_na_k, cache_na_v, cache_att_k, cache_att_v, c_ctx,
           w_mod, b_mod, ln_g, ln_b, w_in_even, w_out_even, ret_decay_logit, na_rpb,
           w_in_odd, w_out_odd, conv_w, q_norm_g, k_norm_g, router_w, router_b,
           exp_w_gate, exp_w_up, exp_w_down, sh_w_gate, sh_w_up, sh_w_down):
    yp = x_prompt.reshape(BATCH * SEQ, D_MODEL)
    ys = x_sample.reshape(DEC_BATCH * DEC_SEQ, D_MODEL)

    cond = jnp.zeros((COND_ROWS, D_MODEL), F32).at[0].set(c_ctx).at[1:1 + DEC_BATCH].set(c)
    mods = _adaln(cond, w_mod, b_mod).reshape(DEPTH, COND_ROWS, N_MOD, D_MODEL)

    out = {}
    for l in range(DEPTH):
        i = l // 2
        mp = mods[l, 0:1]
        ms = mods[l, 1:1 + DEC_BATCH]
        if l % 2 == 0:
            decay_rows = jnp.broadcast_to(ret_decay_logit[i].reshape(2 * RET_HEADS, 1), (2 * RET_HEADS, LANES))
            dts = (BF16, BF16, BF16, F32, BF16, BF16, BF16)
            yp, st, na_k, na_v, w_in, w_out = _even_ctx(yp, mp, w_in_even, w_out_even, i, decay_rows,
                                                        ln_g[l, 0], ln_b[l, 0])
            sp = _inproj(ys, ms, w_in, EVEN_SIZES, dts, DEC_SEQ)
            mix_s = _even_lat(*sp, state_ret, cache_na_k, cache_na_v, i, decay_rows, na_rpb[i].reshape(-1))
            ys = _outproj(mix_s, ys, ms, w_out, ln_g[l, 0], ln_b[l, 0], DEC_SEQ)
            out.setdefault("st", []).append(st.reshape(BATCH, 2, RET_HEADS, RET_DK, RET_DV))
            out.setdefault("na_k", []).append(na_k.reshape(BATCH, SEQ, NA_HEADS, NA_DH))
            out.setdefault("na_v", []).append(na_v.reshape(BATCH, SEQ, NA_HEADS, NA_DH))
        else:
            qg = q_norm_g[i].reshape(1, ATT_DH)
            kg = k_norm_g[i].reshape(1, ATT_DH)
            yp, k_new, att_v, w_in, w_out = _odd_ctx(yp, mp, w_in_odd, w_out_odd, i, conv_w[i], qg, kg,
                                                     ln_g[l, 0], ln_b[l, 0])
            cos, sin = _rope_tables()
            ys = _odd_lat(ys, ms, w_in, cache_att_k, cache_att_v, i,
                          conv_w[i], qg, kg, cos, sin, w_out, ln_g[l, 0], ln_b[l, 0])
            out.setdefault("att_k", []).append(k_new.reshape(BATCH, SEQ, ATT_KV_HEADS, ATT_DH))
            out.setdefault("att_v", []).append(att_v.reshape(BATCH, SEQ, ATT_KV_HEADS, ATT_DH))
        rwt = router_w[l].T
        rb = router_b[l].reshape(N_EXPERTS, 1)
        moe_w = (rwt, rb, exp_w_gate, exp_w_up, exp_w_down, sh_w_gate[l], sh_w_up[l], sh_w_down[l],
                 ln_g[l, 1], ln_b[l, 1])
        yp = _moe(yp, mp, l, *moe_w, SEQ)
        ys = _moe(ys, ms, l, *moe_w, DEC_SEQ)

    return (yp.reshape(BATCH, SEQ, D_MODEL),
            ys.reshape(DEC_BATCH, DEC_SEQ, D_MODEL),
            jnp.stack(out["st"], axis=1),
            jnp.stack(out["na_k"], axis=1),
            jnp.stack(out["na_v"], axis=1),
            jnp.stack(out["att_k"], axis=1),
            jnp.stack(out["att_v"], axis=1))
```
